```python
import jax, jax.numpy as jnp
from jax import lax
import numpy as np

D_MODEL = 1024
BATCH = 4
SEQ = 8192
DEPTH = 1

N_META = 16
ATTN_HEADS = 4
ATTN_HEAD_DIM = 128
ATTN_W = ATTN_HEADS * ATTN_HEAD_DIM
ROPE_THETA = 500000.0
ROPE_FRACTION = 4
IDX_HEADS = 8
IDX_HEAD_DIM = 64
INDEX_TOPK = 256
CONV_CH = D_MODEL // 2
CONV_GROUPS = 4
CONV_WIDTH = 31
MIX_W = ATTN_W + CONV_CH
D_FF = 4 * D_MODEL
Q_BLOCK = 128
NORM_EPS = 1e-5
NEG = -1e30
IN_SIZES = (ATTN_W, ATTN_W, ATTN_W, IDX_HEADS * IDX_HEAD_DIM, IDX_HEAD_DIM, IDX_HEADS, CONV_CH, CONV_CH)
D_IN = sum(IN_SIZES)

kernel_name = "hymba_dsa_conformer_hybrid"


def rms_norm(x, g):
    xf = x.astype(jnp.float32)
    y = xf * lax.rsqrt(jnp.mean(xf * xf, axis=-1, keepdims=True) + NORM_EPS)
    return (y * g.astype(jnp.float32)).astype(x.dtype)


def rope_tables(T, rot_dim):
    inv = ROPE_THETA ** (-jnp.arange(0, rot_dim, 2, dtype=jnp.float32) / rot_dim)
    ang = jnp.arange(T, dtype=jnp.float32)[:, None] * inv[None, :]
    return jnp.cos(ang), jnp.sin(ang)


def partial_rope(x, rot_dim):
    T = x.shape[1]
    cos, sin = rope_tables(T, rot_dim)
    c = cos[None, :, None, :].astype(x.dtype)
    s = sin[None, :, None, :].astype(x.dtype)
    half = rot_dim // 2
    x1, x2, rest = x[..., :half], x[..., half:rot_dim], x[..., rot_dim:]
    return jnp.concatenate([x1 * c - x2 * s, x2 * c + x1 * s, rest], axis=-1)


def dsa_block(q, iq, iw, qpos, k, v, ik, k_top):
    B, Qb, H, dh = q.shape
    T = k.shape[1]
    logits = jnp.einsum('bqhd,bsd->bqhs', iq.astype(jnp.float32), ik.astype(jnp.float32)) * (IDX_HEAD_DIM ** -0.5)
    score = jnp.einsum('bqhs,bqh->bqs', jax.nn.relu(logits), iw.astype(jnp.float32))
    causal = jnp.arange(T)[None, :] <= qpos[:, None]
    score = jnp.where(causal[None], score, NEG)
    _, idx = lax.top_k(score, k_top)
    valid = idx <= qpos[None, :, None]
    gather = jax.vmap(lambda a, i: a[i])
    ks = gather(k, idx)
    vs = gather(v, idx)
    att = jnp.einsum('bqhd,bqkhd->bhqk', q, ks).astype(jnp.float32) * (ATTN_HEAD_DIM ** -0.5)
    att = jnp.where(valid[:, None], att, NEG)
    p = jax.nn.softmax(att, axis=-1).astype(v.dtype)
    o = jnp.einsum('bhqk,bqkhd->bqhd', p, vs)
    return o.reshape(B, Qb, H * dh)


def conformer_conv(a, g, conv_w, conv_b, ln_g, ln_b):
    u = a * jax.nn.sigmoid(g)
    C = u.shape[-1]
    y = lax.conv_general_dilated(
        u, conv_w[:, None, :].astype(u.dtype), window_strides=(1,),
        padding=[(CONV_WIDTH - 1, 0)], dimension_numbers=('NWC', 'WIO', 'NWC'),
        feature_group_count=C)
    y = y + conv_b.astype(u.dtype)
    B, T, _ = y.shape
    yf = y.astype(jnp.float32).reshape(B, T, CONV_GROUPS, C // CONV_GROUPS)
    mu = jnp.mean(yf, axis=-1, keepdims=True)
    var = jnp.mean(jnp.square(yf - mu), axis=-1, keepdims=True)
    yn = ((yf - mu) * lax.rsqrt(var + NORM_EPS)).reshape(B, T, C)
    yn = yn * ln_g.astype(jnp.float32) + ln_b.astype(jnp.float32)
    return jax.nn.silu(yn).astype(u.dtype)


def setup_inputs(seed: int = 0) -> dict:
    key = jax.random.key(seed)
    ks = jax.random.split(key, 13)
    f32 = jnp.float32
    return {
        "x": jax.random.normal(ks[0], (BATCH, SEQ, D_MODEL), f32),
        "meta_tokens": jax.random.normal(ks[1], (N_META, D_MODEL), f32),
        "attn_norm_g": 1.0 + 0.01 * jax.random.normal(ks[2], (DEPTH, D_MODEL), f32),
        "w_in": jax.random.normal(ks[3], (DEPTH, D_MODEL, D_IN), f32) * D_MODEL ** -0.5,
        "conv_w": jax.random.normal(ks[4], (DEPTH, CONV_WIDTH, CONV_CH), f32) * CONV_WIDTH ** -0.5,
        "conv_b": 0.01 * jax.random.normal(ks[5], (DEPTH, CONV_CH), f32),
        "conv_norm_g": 1.0 + 0.01 * jax.random.normal(ks[6], (DEPTH, CONV_CH), f32),
        "conv_norm_b": 0.01 * jax.random.normal(ks[7], (DEPTH, CONV_CH), f32),
        "w_out": jax.random.normal(ks[8], (DEPTH, MIX_W, D_MODEL), f32) * MIX_W ** -0.5,
        "mlp_norm_g": 1.0 + 0.01 * jax.random.normal(ks[9], (DEPTH, D_MODEL), f32),
        "w_up": jax.random.normal(ks[10], (DEPTH, D_MODEL, D_FF), f32) * D_MODEL ** -0.5,
        "w_down": jax.random.normal(ks[11], (DEPTH, D_FF, D_MODEL), f32) * D_FF ** -0.5,
        "final_norm_g": 1.0 + 0.01 * jax.random.normal(ks[12], (D_MODEL,), f32),
    }


def reference(x, meta_tokens, attn_norm_g, w_in, conv_w, conv_b, conv_norm_g, conv_norm_b,
              w_out, mlp_norm_g, w_up, w_down, final_norm_g):
    B, S, D = x.shape
    h = jnp.concatenate([jnp.broadcast_to(meta_tokens[None].astype(x.dtype), (B, N_META, D)), x], axis=1)
    T = h.shape[1]
    k_top = min(INDEX_TOPK, S // 4)
    n_blocks = S // Q_BLOCK
    offsets = [0]
    for sz in IN_SIZES:
        offsets.append(offsets[-1] + sz)

    for l in range(DEPTH):
        xn = rms_norm(h, attn_norm_g[l])
        proj = jnp.einsum('btd,de->bte', xn, w_in[l])
        q, k, v, iq, ik, iw, ga, gg = [proj[..., offsets[i]:offsets[i + 1]] for i in range(len(IN_SIZES))]
        q = partial_rope(q.reshape(B, T, ATTN_HEADS, ATTN_HEAD_DIM), ATTN_HEAD_DIM // ROPE_FRACTION)
        k = partial_rope(k.reshape(B, T, ATTN_HEADS, ATTN_HEAD_DIM), ATTN_HEAD_DIM // ROPE_FRACTION)
        v = v.reshape(B, T, ATTN_HEADS, ATTN_HEAD_DIM)
        iq = partial_rope(iq.reshape(B, T, IDX_HEADS, IDX_HEAD_DIM), IDX_HEAD_DIM // ROPE_FRACTION)
        ik = partial_rope(ik[:, :, None, :], IDX_HEAD_DIM // ROPE_FRACTION)[:, :, 0, :]
        iw = iw * (IDX_HEADS ** -0.5)

        o_meta = dsa_block(q[:, :N_META], iq[:, :N_META], iw[:, :N_META],
                           jnp.arange(N_META), k, v, ik, k_top)

        def to_blocks(a):
            a = a[:, N_META:].reshape((B, n_blocks, Q_BLOCK) + a.shape[2:])
            return jnp.moveaxis(a, 1, 0)

        qpos_r = N_META + jnp.arange(S).reshape(n_blocks, Q_BLOCK)
        o_real = lax.map(
            lambda xs: dsa_block(xs[0], xs[1], xs[2], xs[3], k, v, ik, k_top),
            (to_blocks(q), to_blocks(iq), to_blocks(iw), qpos_r))
        o_real = jnp.moveaxis(o_real, 0, 1).reshape(B, S, ATTN_W)
        attn_out = jnp.concatenate([o_meta, o_real], axis=1)

        conv_out = conformer_conv(ga, gg, conv_w[l], conv_b[l], conv_norm_g[l], conv_norm_b[l])

        mixed = jnp.concatenate([attn_out, conv_out], axis=-1)
        h = h + jnp.einsum('bte,ed->btd', mixed, w_out[l])

        hn = rms_norm(h, mlp_norm_g[l])
        u = jnp.square(jax.nn.relu(jnp.einsum('btd,df->btf', hn, w_up[l])))
        h = h + jnp.einsum('btf,fd->btd', u, w_down[l])

    out = rms_norm(h, final_norm_g)
    return out[:, N_META:]
```

```python
import functools

import jax
import jax.numpy as jnp
from jax import lax
from jax.experimental import pallas as pl
from jax.experimental.pallas import tpu as pltpu

D_MODEL = 1024
N_META = 16
ATTN_HEADS = 4
ATTN_HEAD_DIM = 128
ATTN_W = ATTN_HEADS * ATTN_HEAD_DIM
ROPE_THETA = 500000.0
ROPE_FRACTION = 4
IDX_HEADS = 8
IDX_HEAD_DIM = 64
INDEX_TOPK = 256
CONV_CH = 512
CONV_GROUPS = 4
CONV_WIDTH = 31
D_FF = 4 * D_MODEL
NORM_EPS = 1e-5
NEG = -1e30

LANES = 128
BLK = 256
PAD_ROWS = BLK - N_META
CARRY = 32
W_IN_COLS = 6 * 512 + LANES
ROW_GROUP = 64
MLP_ROWS = 512
FF_CHUNK = 1024
VMEM_LIMIT = 60 * 1024 * 1024

_NT = (((1,), (1,)), ((), ()))


def _rms(x, g):
    return x * lax.rsqrt(jnp.mean(x * x, axis=-1, keepdims=True) + NORM_EPS) * g


def _rope(x, c, s1, s2, half):
    return x * c + pltpu.roll(x, half, 1) * s1 + pltpu.roll(x, LANES - half, 1) * s2


def _proj_kernel(x_ref, meta_ref, g_ref, w_ref, rq_ref, ri_ref, cw_ref, cb_ref, lg_ref, lb_ref,
                 q_ref, iq_ref, misc_ref, conv_ref, k_ref, v_ref, ika_ref, ikb_ref, u_scr):
    i = pl.program_id(1)
    h = jnp.where(i == 0, meta_ref[...], x_ref[0])
    xn = _rms(h, g_ref[...]).astype(jnp.bfloat16)
    proj = jnp.dot(xn, w_ref[...], preferred_element_type=jnp.float32)

    cq, s1q, s2q = rq_ref[0], rq_ref[1], rq_ref[2]
    ci, s1i, s2i = ri_ref[0], ri_ref[1], ri_ref[2]
    q_half = ATTN_HEAD_DIM // ROPE_FRACTION // 2
    i_half = IDX_HEAD_DIM // ROPE_FRACTION // 2
    for hd in range(ATTN_HEADS):
        sl = slice(hd * LANES, (hd + 1) * LANES)
        qh = _rope(proj[:, sl], cq, s1q, s2q, q_half) * (ATTN_HEAD_DIM ** -0.5)
        q_ref[0, :, sl] = qh.astype(jnp.bfloat16)
        kh = _rope(proj[:, 512 + hd * LANES:512 + (hd + 1) * LANES], cq, s1q, s2q, q_half)
        k_ref[0, :, sl] = kh.astype(jnp.bfloat16)
        ih = _rope(proj[:, 1536 + hd * LANES:1536 + (hd + 1) * LANES], ci, s1i, s2i, i_half)
        iq_ref[0, :, sl] = (ih * (IDX_HEAD_DIM ** -0.5)).astype(jnp.bfloat16)
    v_ref[0] = proj[:, 1024:1536].astype(jnp.bfloat16)

    lane = lax.broadcasted_iota(jnp.int32, (BLK, LANES), 1)
    is_ik = lane < IDX_HEAD_DIM
    slab = proj[:, 3072:3072 + LANES]
    roped = _rope(slab, jnp.where(is_ik, ci, 1.0), jnp.where(is_ik, s1i, 0.0),
                  jnp.where(is_ik, s2i, 0.0), i_half)
    misc_ref[0] = jnp.where(is_ik, roped, slab * (IDX_HEADS ** -0.5))
    ika = jnp.where(is_ik, roped, 0.0)
    ika_ref[0] = ika.astype(jnp.bfloat16)
    ikb_ref[0] = pltpu.roll(ika, IDX_HEAD_DIM, 1).astype(jnp.bfloat16)

    @pl.when(i == 0)
    def _():
        u_scr[0:CARRY, :] = jnp.zeros((CARRY, CONV_CH), jnp.float32)

    ga = proj[:, 2048:2560]
    gg = proj[:, 2560:3072]
    u_scr[CARRY:CARRY + BLK, :] = ga * jax.nn.sigmoid(gg)
    y = jnp.zeros((BLK, CONV_CH), jnp.float32) + cb_ref[...]
    for j in range(CONV_WIDTH):
        off = CARRY - (CONV_WIDTH - 1) + j
        y = y + u_scr[off:off + BLK, :] * cw_ref[j:j + 1, :]
    u_scr[0:CARRY, :] = u_scr[BLK:BLK + CARRY, :]
    gsz = CONV_CH // CONV_GROUPS
    for gi in range(CONV_GROUPS):
        sl = slice(gi * gsz, (gi + 1) * gsz)
        yg = y[:, sl]
        mu = jnp.mean(yg, axis=-1, keepdims=True)
        d = yg - mu
        var = jnp.mean(d * d, axis=-1, keepdims=True)
        yn = d * lax.rsqrt(var + NORM_EPS) * lg_ref[:, sl] + lb_ref[:, sl]
        conv_ref[0, :, sl] = (yn * jax.nn.sigmoid(yn)).astype(jnp.bfloat16)


def _key_to_float(u):
    bits = jnp.where(u < 0, u ^ jnp.int32(-2147483648), ~u)
    return lax.bitcast_convert_type(bits, jnp.float32)


def _dsa_kernel(q_ref, iq_ref, misc_ref, ika_ref, ikb_ref, k_ref, v_ref, o_ref,
                sc_ref, wb_ref, tau_ref, m_ref, l_ref, acc_ref, *, n_rows_total):
    g = pl.program_id(1)
    nt = g + 2
    q0 = (g + 1) * BLK

    misc = misc_ref[0]
    for hd in range(IDX_HEADS):
        wb_ref[hd] = jnp.broadcast_to(misc[:, IDX_HEAD_DIM + hd:IDX_HEAD_DIM + hd + 1], (BLK, LANES))

    def score_tile(j):
        r = pl.multiple_of(j * BLK, BLK)
        ka = ika_ref[0, pl.ds(r, BLK), :]
        kb = ikb_ref[0, pl.ds(r, BLK), :]
        a0 = jnp.zeros((BLK, LANES), jnp.float32)
        a1 = jnp.zeros((BLK, LANES), jnp.float32)
        for p in range(IDX_HEADS // 2):
            slab = iq_ref[0, :, p * LANES:(p + 1) * LANES]
            le = lax.dot_general(slab, ka, _NT, preferred_element_type=jnp.float32)
            lo = lax.dot_general(slab, kb, _NT, preferred_element_type=jnp.float32)
            we = wb_ref[2 * p]
            wo = wb_ref[2 * p + 1]
            a0 = a0 + jnp.maximum(le[:, :LANES], 0.0) * we + jnp.maximum(lo[:, :LANES], 0.0) * wo
            a1 = a1 + jnp.maximum(le[:, LANES:], 0.0) * we + jnp.maximum(lo[:, LANES:], 0.0) * wo
        return jnp.concatenate([a0, a1], axis=1)

    row = lax.broadcasted_iota(jnp.int32, (BLK, BLK), 0)
    col = lax.broadcasted_iota(jnp.int32, (BLK, BLK), 1)
    sc_ref[0] = jnp.where(col >= PAD_ROWS, score_tile(0), -jnp.inf)
    sc_ref[nt - 1] = jnp.where(col <= row, score_tile(nt - 1), -jnp.inf)

    def interior(j, c):
        sc_ref[j] = score_tile(j)
        return c

    lax.fori_loop(1, nt - 1, interior, 0)

    def count_ge(r0, cand):
        cb = jnp.broadcast_to(cand, (ROW_GROUP, LANES))
        cb2 = jnp.concatenate([cb, cb], axis=1)

        def body(j, acc):
            x = sc_ref[j, pl.ds(r0, ROW_GROUP), :]
            return acc + jnp.where(x >= cb2, 1.0, 0.0)

        acc = lax.fori_loop(0, nt, body, jnp.zeros((ROW_GROUP, BLK), jnp.float32))
        return jnp.sum(acc, axis=1, keepdims=True)

    def row_group(rg, c):
        r0 = pl.multiple_of(rg * ROW_GROUP, ROW_GROUP)
        qrow = q0 + r0 + lax.broadcasted_iota(jnp.int32, (ROW_GROUP, 1), 0)
        n_virtual = (n_rows_total - 1 - qrow).astype(jnp.float32)

        def bit_step(it, carry):
            prefix, cnt = carry
            bit = jnp.left_shift(jnp.int32(1), 31 - it)
            trial = prefix | bit
            cand = _key_to_float(trial)
            c_proc = count_ge(r0, cand)
            total = c_proc + jnp.where(cand <= NEG, n_virtual, 0.0)
            ok = total >= INDEX_TOPK
            return jnp.where(ok, trial, prefix), jnp.where(ok, c_proc, cnt)

        prefix, cnt = lax.fori_loop(
            0, 32, bit_step,
            (jnp.zeros((ROW_GROUP, 1), jnp.int32), jnp.zeros((ROW_GROUP, 1), jnp.float32)))
        tau = _key_to_float(prefix)
        tau_ref[pl.ds(r0, ROW_GROUP), :] = jnp.broadcast_to(tau, (ROW_GROUP, LANES))

        tied = cnt > INDEX_TOPK

        @pl.when(jnp.max(jnp.where(tied, 1.0, 0.0)) > 0.0)
        def _():
            c_gt = count_ge(r0, _key_to_float(prefix + 1))
            need = INDEX_TOPK - c_gt
            taub = jnp.broadcast_to(tau, (ROW_GROUP, LANES))
            taub2 = jnp.concatenate([taub, taub], axis=1)
            lcol = lax.broadcasted_iota(jnp.int32, (ROW_GROUP, BLK), 1)

            def idx_step(it, pfx):
                trial = pfx | jnp.left_shift(jnp.int32(1), 13 - it)
                tb = jnp.broadcast_to(trial, (ROW_GROUP, LANES))
                tb2 = jnp.concatenate([tb, tb], axis=1)

                def body(j, acc):
                    x = sc_ref[j, pl.ds(r0, ROW_GROUP), :]
                    hit = (x == taub2) & (lcol + j * BLK < tb2)
                    return acc + jnp.where(hit, 1.0, 0.0)

                acc = lax.fori_loop(0, nt, body, jnp.zeros((ROW_GROUP, BLK), jnp.float32))
                below = jnp.sum(acc, axis=1, keepdims=True)
                return jnp.where(below < need, trial, pfx)

            last = lax.fori_loop(0, 14, idx_step, jnp.zeros((ROW_GROUP, 1), jnp.int32))
            last = jnp.where(tied, last, jnp.int32(1 << 30))
            lb = jnp.broadcast_to(last, (ROW_GROUP, LANES))
            lb2 = jnp.concatenate([lb, lb], axis=1)

            def drop(j, c2):
                x = sc_ref[j, pl.ds(r0, ROW_GROUP), :]
                cut = (x == taub2) & (lcol + j * BLK > lb2)
                sc_ref[j, pl.ds(r0, ROW_GROUP), :] = jnp.where(cut, -jnp.inf, x)
                return c2

            lax.fori_loop(0, nt, drop, 0)

        return c

    lax.fori_loop(0, BLK // ROW_GROUP, row_group, 0)

    m_ref[...] = jnp.full(m_ref.shape, NEG, jnp.float32)
    l_ref[...] = jnp.zeros(l_ref.shape, jnp.float32)
    acc_ref[...] = jnp.zeros(acc_ref.shape, jnp.float32)
    tau = tau_ref[...]
    tau2 = jnp.concatenate([tau, tau], axis=1)

    def attend(j, c):
        r = pl.multiple_of(j * BLK, BLK)
        bias = jnp.where(sc_ref[j] >= tau2, 0.0, NEG)
        for hd in range(ATTN_HEADS):
            sl = slice(hd * LANES, (hd + 1) * LANES)
            kh = k_ref[0, pl.ds(r, BLK), sl]
            vh = v_ref[0, pl.ds(r, BLK), sl]
            s = lax.dot_general(q_ref[0, :, sl], kh, _NT, preferred_element_type=jnp.float32) + bias
            m_old = m_ref[hd]
            m_new = jnp.maximum(m_old, jnp.max(s, axis=1, keepdims=True))
            p = jnp.exp(s - m_new)
            alpha = jnp.exp(m_old - m_new)
            l_ref[hd] = alpha * l_ref[hd] + jnp.sum(p, axis=1, keepdims=True)
            acc_ref[hd] = alpha * acc_ref[hd] + jnp.dot(p.astype(jnp.bfloat16), vh,
                                                        preferred_element_type=jnp.float32)
            m_ref[hd] = m_new
        return c

    lax.fori_loop(0, nt, attend, 0)
    for hd in range(ATTN_HEADS):
        o_ref[0, :, hd * LANES:(hd + 1) * LANES] = (acc_ref[hd] / l_ref[hd]).astype(jnp.bfloat16)


def _mlp_kernel(x_ref, a_ref, c_ref, woa_ref, woc_ref, g2_ref, wu_ref, wd_ref, gf_ref, o_ref):
    h = (x_ref[0]
         + jnp.dot(a_ref[0], woa_ref[...], preferred_element_type=jnp.float32)
         + jnp.dot(c_ref[0], woc_ref[...], preferred_element_type=jnp.float32))
    hn = _rms(h, g2_ref[...]).astype(jnp.bfloat16)
    acc = jnp.zeros_like(h)
    for c in range(D_FF // FF_CHUNK):
        sl = slice(c * FF_CHUNK, (c + 1) * FF_CHUNK)
        u = jnp.dot(hn, wu_ref[:, sl], preferred_element_type=jnp.float32)
        u = jnp.square(jnp.maximum(u, 0.0)).astype(jnp.bfloat16)
        acc = acc + jnp.dot(u, wd_ref[sl, :], preferred_element_type=jnp.float32)
    o_ref[0] = _rms(h + acc, gf_ref[...])


def _rope_tables(n_rows, head_dim, lanes_per_head):
    rot = head_dim // ROPE_FRACTION
    half = rot // 2
    inv = ROPE_THETA ** (-jnp.arange(0, rot, 2, dtype=jnp.float32) / rot)
    pos = jnp.arange(n_rows, dtype=jnp.float32) - float(PAD_ROWS)
    ang = pos[:, None] * inv[None, :]
    cos, sin = jnp.cos(ang), jnp.sin(ang)
    rest = lanes_per_head - rot
    one = jnp.ones((n_rows, rest), jnp.float32)
    zero = jnp.zeros((n_rows, rest), jnp.float32)
    zh = jnp.zeros((n_rows, half), jnp.float32)
    c = jnp.concatenate([cos, cos, one], axis=1)
    s1 = jnp.concatenate([zh, sin, zero], axis=1)
    s2 = jnp.concatenate([-sin, zh, zero], axis=1)
    reps = LANES // lanes_per_head
    return jnp.stack([jnp.tile(t, (1, reps)) for t in (c, s1, s2)])


def _const(shape):
    return pl.BlockSpec(shape, lambda b, i: (0,) * len(shape), pipeline_mode=pl.Buffered(1))


def _project(x, meta_tokens, attn_norm_g, w_in, conv_w, conv_b, conv_norm_g, conv_norm_b):
    B, S, D = x.shape
    nb = S // BLK
    T = S + BLK
    f32, bf16 = jnp.float32, jnp.bfloat16

    w = w_in[0]
    w_rep = jnp.concatenate(
        [w[:, 0:2048], w[:, 2120:3144], w[:, 2048:2120],
         jnp.zeros((D, W_IN_COLS - 3144), f32)], axis=1).astype(bf16)
    meta_pad = jnp.concatenate([jnp.zeros((PAD_ROWS, D), f32), meta_tokens.astype(f32)], axis=0)
    rq = _rope_tables(T, ATTN_HEAD_DIM, LANES)
    ri = _rope_tables(T, IDX_HEAD_DIM, IDX_HEAD_DIM)

    real = lambda b, i: (b, jnp.maximum(i - 1, 0), 0)
    padded = lambda b, i: (b, i, 0)
    return pl.pallas_call(
        _proj_kernel,
        grid=(B, nb + 1),
        in_specs=[
            pl.BlockSpec((1, BLK, D), real),
            _const((BLK, D)),
            _const((1, D)),
            _const((D, W_IN_COLS)),
            pl.BlockSpec((3, BLK, LANES), lambda b, i: (0, i, 0)),
            pl.BlockSpec((3, BLK, LANES), lambda b, i: (0, i, 0)),
            _const((CONV_WIDTH, CONV_CH)),
            _const((1, CONV_CH)),
            _const((1, CONV_CH)),
            _const((1, CONV_CH)),
        ],
        out_specs=[
            pl.BlockSpec((1, BLK, ATTN_W), real),
            pl.BlockSpec((1, BLK, ATTN_W), real),
            pl.BlockSpec((1, BLK, LANES), real),
            pl.BlockSpec((1, BLK, CONV_CH), real),
            pl.BlockSpec((1, BLK, ATTN_W), padded),
            pl.BlockSpec((1, BLK, ATTN_W), padded),
            pl.BlockSpec((1, BLK, LANES), padded),
            pl.BlockSpec((1, BLK, LANES), padded),
        ],
        out_shape=[
            jax.ShapeDtypeStruct((B, S, ATTN_W), bf16),
            jax.ShapeDtypeStruct((B, S, ATTN_W), bf16),
            jax.ShapeDtypeStruct((B, S, LANES), f32),
            jax.ShapeDtypeStruct((B, S, CONV_CH), bf16),
            jax.ShapeDtypeStruct((B, T, ATTN_W), bf16),
            jax.ShapeDtypeStruct((B, T, ATTN_W), bf16),
            jax.ShapeDtypeStruct((B, T, LANES), bf16),
            jax.ShapeDtypeStruct((B, T, LANES), bf16),
        ],
        scratch_shapes=[pltpu.VMEM((CARRY + BLK, CONV_CH), f32)],
        compiler_params=pltpu.CompilerParams(
            dimension_semantics=("arbitrary", "arbitrary"), vmem_limit_bytes=VMEM_LIMIT),
        name="proj_rope_conv",
    )(x, meta_pad, attn_norm_g, w_rep, rq, ri, conv_w[0], conv_b, conv_norm_g, conv_norm_b)


def _attend(q, iq, misc, ika, ikb, k, v):
    B, S, _ = q.shape
    T = k.shape[1]
    nb = S // BLK
    f32, bf16 = jnp.float32, jnp.bfloat16
    blk = lambda b, g: (b, g, 0)
    whole = lambda width: pl.BlockSpec((1, T, width), lambda b, g: (b, 0, 0),
                                       pipeline_mode=pl.Buffered(1))
    return pl.pallas_call(
        functools.partial(_dsa_kernel, n_rows_total=T),
        grid=(B, nb),
        in_specs=[
            pl.BlockSpec((1, BLK, ATTN_W), blk),
            pl.BlockSpec((1, BLK, ATTN_W), blk),
            pl.BlockSpec((1, BLK, LANES), blk),
            whole(LANES), whole(LANES), whole(ATTN_W), whole(ATTN_W),
        ],
        out_specs=pl.BlockSpec((1, BLK, ATTN_W), blk),
        out_shape=jax.ShapeDtypeStruct((B, S, ATTN_W), bf16),
        scratch_shapes=[
            pltpu.VMEM((nb + 1, BLK, BLK), f32),
            pltpu.VMEM((IDX_HEADS, BLK, LANES), f32),
            pltpu.VMEM((BLK, LANES), f32),
            pltpu.VMEM((ATTN_HEADS, BLK, 1), f32),
            pltpu.VMEM((ATTN_HEADS, BLK, 1), f32),
            pltpu.VMEM((ATTN_HEADS, BLK, ATTN_HEAD_DIM), f32),
        ],
        compiler_params=pltpu.CompilerParams(
            dimension_semantics=("arbitrary", "arbitrary"), vmem_limit_bytes=VMEM_LIMIT),
        name="dsa_attention",
    )(q, iq, misc, ika, ikb, k, v)


def _out_mlp(x, attn, conv, w_out, mlp_norm_g, w_up, w_down, final_norm_g):
    B, S, D = x.shape
    bf16 = jnp.bfloat16
    wo = w_out[0].astype(bf16)
    rows = lambda b, i: (b, i, 0)
    return pl.pallas_call(
        _mlp_kernel,
        grid=(B, S // MLP_ROWS),
        in_specs=[
            pl.BlockSpec((1, MLP_ROWS, D), rows),
            pl.BlockSpec((1, MLP_ROWS, ATTN_W), rows),
            pl.BlockSpec((1, MLP_ROWS, CONV_CH), rows),
            _const((ATTN_W, D)),
            _const((CONV_CH, D)),
            _const((1, D)),
            _const((D, D_FF)),
            _const((D_FF, D)),
            _const((1, D)),
        ],
        out_specs=pl.BlockSpec((1, MLP_ROWS, D), rows),
        out_shape=jax.ShapeDtypeStruct((B, S, D), x.dtype),
        compiler_params=pltpu.CompilerParams(
            dimension_semantics=("arbitrary", "arbitrary"), vmem_limit_bytes=VMEM_LIMIT),
        name="out_mlp",
    )(x, attn, conv, wo[:ATTN_W], wo[ATTN_W:], mlp_norm_g, w_up[0].astype(bf16),
      w_down[0].astype(bf16), final_norm_g.reshape(1, D))


def kernel(x, meta_tokens, attn_norm_g, w_in, conv_w, conv_b, conv_norm_g, conv_norm_b,
           w_out, mlp_norm_g, w_up, w_down, final_norm_g):
    B, S, D = x.shape
    assert D == D_MODEL and S % MLP_ROWS == 0 and attn_norm_g.shape[0] == 1
    q, iq, misc, conv, k, v, ika, ikb = _project(
        x, meta_tokens, attn_norm_g, w_in, conv_w, conv_b, conv_norm_g, conv_norm_b)
    attn = _attend(q, iq, misc, ika, ikb, k, v)
    return _out_mlp(x, attn, conv, w_out, mlp_norm_g, w_up, w_down, final_norm_g)
```

```python
import functools

import jax
import jax.numpy as jnp
from jax import lax
from jax.experimental import pallas as pl
from jax.experimental.pallas import tpu as pltpu

D_MODEL = 1024
N_META = 16
ATTN_HEADS = 4
ATTN_HEAD_DIM = 128
ATTN_W = ATTN_HEADS * ATTN_HEAD_DIM
ROPE_THETA = 500000.0
ROPE_FRACTION = 4
IDX_HEADS = 8
IDX_HEAD_DIM = 64
INDEX_TOPK = 256
CONV_CH = 512
CONV_GROUPS = 4
CONV_WIDTH = 31
D_FF = 4 * D_MODEL
NORM_EPS = 1e-5
NEG = -1e30

LANES = 128
BLK = 256
PAD_ROWS = BLK - N_META
CARRY = 32
W_IN_COLS = 6 * 512 + LANES
VALUE_STEPS = 8
BOUND_SLACK = 1.0 + 2.0 ** -6
L_MIN = 1e-30
MLP_ROWS = 512
FF_CHUNK = 1024
VMEM_LIMIT = 60 * 1024 * 1024

_NT = (((1,), (1,)), ((), ()))


def _rms(x, g):
    return x * lax.rsqrt(jnp.mean(x * x, axis=-1, keepdims=True) + NORM_EPS) * g


def _rope(x, c, s1, s2, half):
    return x * c + pltpu.roll(x, half, 1) * s1 + pltpu.roll(x, LANES - half, 1) * s2


def _proj_kernel(x_ref, meta_ref, g_ref, w_ref, rq_ref, ri_ref, cw_ref, cb_ref, lg_ref, lb_ref,
                 q_ref, iq_ref, misc_ref, conv_ref, k_ref, v_ref, ika_ref, ikb_ref, u_scr):
    i = pl.program_id(1)
    h = jnp.where(i == 0, meta_ref[...], x_ref[0])
    xn = _rms(h, g_ref[...]).astype(jnp.bfloat16)
    proj = jnp.dot(xn, w_ref[...], preferred_element_type=jnp.float32)

    cq, s1q, s2q = rq_ref[0], rq_ref[1], rq_ref[2]
    ci, s1i, s2i = ri_ref[0], ri_ref[1], ri_ref[2]
    q_half = ATTN_HEAD_DIM // ROPE_FRACTION // 2
    i_half = IDX_HEAD_DIM // ROPE_FRACTION // 2
    for hd in range(ATTN_HEADS):
        sl = slice(hd * LANES, (hd + 1) * LANES)
        qh = _rope(proj[:, sl], cq, s1q, s2q, q_half) * (ATTN_HEAD_DIM ** -0.5)
        q_ref[0, :, sl] = qh.astype(jnp.bfloat16)
        kh = _rope(proj[:, 512 + hd * LANES:512 + (hd + 1) * LANES], cq, s1q, s2q, q_half)
        k_ref[0, :, sl] = kh.astype(jnp.bfloat16)
        ih = _rope(proj[:, 1536 + hd * LANES:1536 + (hd + 1) * LANES], ci, s1i, s2i, i_half)
        iq_ref[0, :, sl] = (ih * (IDX_HEAD_DIM ** -0.5)).astype(jnp.bfloat16)
    v_ref[0] = proj[:, 1024:1536].astype(jnp.bfloat16)

    lane = lax.broadcasted_iota(jnp.int32, (BLK, LANES), 1)
    is_ik = lane < IDX_HEAD_DIM
    slab = proj[:, 3072:3072 + LANES]
    roped = _rope(slab, jnp.where(is_ik, ci, 1.0), jnp.where(is_ik, s1i, 0.0),
                  jnp.where(is_ik, s2i, 0.0), i_half)
    misc_ref[0] = jnp.where(is_ik, roped, slab * (IDX_HEADS ** -0.5))
    ika = jnp.where(is_ik, roped, 0.0)
    ika_ref[0] = ika.astype(jnp.bfloat16)
    ikb_ref[0] = pltpu.roll(ika, IDX_HEAD_DIM, 1).astype(jnp.bfloat16)

    @pl.when(i == 0)
    def _():
        u_scr[0:CARRY, :] = jnp.zeros((CARRY, CONV_CH), jnp.float32)

    ga = proj[:, 2048:2560]
    gg = proj[:, 2560:3072]
    u_scr[CARRY:CARRY + BLK, :] = ga * jax.nn.sigmoid(gg)
    y = jnp.zeros((BLK, CONV_CH), jnp.float32) + cb_ref[...]
    for j in range(CONV_WIDTH):
        off = CARRY - (CONV_WIDTH - 1) + j
        y = y + u_scr[off:off + BLK, :] * cw_ref[j:j + 1, :]
    u_scr[0:CARRY, :] = u_scr[BLK:BLK + CARRY, :]
    gsz = CONV_CH // CONV_GROUPS
    for gi in range(CONV_GROUPS):
        sl = slice(gi * gsz, (gi + 1) * gsz)
        yg = y[:, sl]
        mu = jnp.mean(yg, axis=-1, keepdims=True)
        d = yg - mu
        var = jnp.mean(d * d, axis=-1, keepdims=True)
        yn = d * lax.rsqrt(var + NORM_EPS) * lg_ref[:, sl] + lb_ref[:, sl]
        conv_ref[0, :, sl] = (yn * jax.nn.sigmoid(yn)).astype(jnp.bfloat16)


def _f2k(f):
    b = lax.bitcast_convert_type(f, jnp.int32)
    return b ^ ((b >> 31) & jnp.int32(0x7FFFFFFF))


def _k2f(k):
    return lax.bitcast_convert_type(k ^ ((k >> 31) & jnp.int32(0x7FFFFFFF)), jnp.float32)


def _any(mask):
    return jnp.max(jnp.where(mask, 1.0, 0.0)) > 0.5


def _dsa_kernel(q_ref, iq_ref, misc_ref, ika_ref, ikb_ref, k_ref, v_ref, o_ref,
                sc_ref, wb_ref, tau_ref, mx_ref, mn_ref, kn_ref, mb_ref, ls_ref, acc_ref,
                m_ref, l_ref, *, n_rows_total):
    f32, bf16 = jnp.float32, jnp.bfloat16
    g = pl.program_id(1)
    nt = g + 2
    q0 = (g + 1) * BLK
    ones = jnp.ones((LANES, LANES), bf16)

    misc = misc_ref[0]
    for hd in range(IDX_HEADS):
        wb_ref[hd] = jnp.broadcast_to(misc[:, IDX_HEAD_DIM + hd:IDX_HEAD_DIM + hd + 1], (BLK, LANES))

    def score_tile(j):
        r = pl.multiple_of(j * BLK, BLK)
        ka = ika_ref[0, pl.ds(r, BLK), :]
        kb = ikb_ref[0, pl.ds(r, BLK), :]
        a0 = jnp.zeros((BLK, LANES), f32)
        a1 = jnp.zeros((BLK, LANES), f32)
        for p in range(IDX_HEADS // 2):
            slab = iq_ref[0, :, p * LANES:(p + 1) * LANES]
            le = lax.dot_general(slab, ka, _NT, preferred_element_type=f32)
            lo = lax.dot_general(slab, kb, _NT, preferred_element_type=f32)
            we = wb_ref[2 * p]
            wo = wb_ref[2 * p + 1]
            a0 = a0 + jnp.maximum(le[:, :LANES], 0.0) * we + jnp.maximum(lo[:, :LANES], 0.0) * wo
            a1 = a1 + jnp.maximum(le[:, LANES:], 0.0) * we + jnp.maximum(lo[:, LANES:], 0.0) * wo
        return a0, a1

    def put_tile(j, a0, a1, v0, v1):
        lo0 = a0 if v0 is None else jnp.where(v0, a0, -jnp.inf)
        lo1 = a1 if v1 is None else jnp.where(v1, a1, -jnp.inf)
        hi0 = a0 if v0 is None else jnp.where(v0, a0, jnp.inf)
        hi1 = a1 if v1 is None else jnp.where(v1, a1, jnp.inf)
        sc_ref[j, :, :LANES] = lo0
        sc_ref[j, :, LANES:] = lo1
        mx_ref[...] = jnp.maximum(mx_ref[...], jnp.maximum(lo0, lo1))
        mn_ref[...] = jnp.minimum(mn_ref[...], jnp.minimum(hi0, hi1))

    mx_ref[...] = jnp.full((BLK, LANES), -jnp.inf, f32)
    mn_ref[...] = jnp.full((BLK, LANES), jnp.inf, f32)
    rowi = lax.broadcasted_iota(jnp.int32, (BLK, LANES), 0)
    lanei = lax.broadcasted_iota(jnp.int32, (BLK, LANES), 1)
    a0, a1 = score_tile(0)
    put_tile(0, a0, a1, lanei < 0, lanei + LANES >= PAD_ROWS)
    a0, a1 = score_tile(nt - 1)
    put_tile(nt - 1, a0, a1, lanei <= rowi, lanei + LANES <= rowi)

    def interior(j, c):
        t0, t1 = score_tile(j)
        put_tile(j, t0, t1, None, None)
        return c

    lax.fori_loop(1, nt - 1, interior, 0)

    kf = float(INDEX_TOPK)
    qrow = q0 + rowi
    n_valid = (qrow - (PAD_ROWS - 1)).astype(f32)
    n_virtual = (n_rows_total - 1 - qrow).astype(f32)
    rowmax = jnp.broadcast_to(jnp.max(mx_ref[...], axis=1, keepdims=True), (BLK, LANES))
    rowmin = jnp.broadcast_to(jnp.min(mn_ref[...], axis=1, keepdims=True), (BLK, LANES))

    def count_ge(cand):
        def body(j, acc):
            return (acc + jnp.where(sc_ref[j, :, :LANES] >= cand, 1.0, 0.0)
                    + jnp.where(sc_ref[j, :, LANES:] >= cand, 1.0, 0.0))

        acc = lax.fori_loop(0, nt, body, jnp.zeros((BLK, LANES), f32))
        return jnp.dot(acc.astype(bf16), ones, preferred_element_type=f32)

    def total_of(c_proc, cand):
        return c_proc + jnp.where(cand <= NEG, n_virtual, 0.0)

    def active_of(klo, khi, clo):
        return (total_of(clo, _k2f(klo)) != kf) & (khi > klo + 1)

    kneg = _f2k(jnp.full((BLK, LANES), NEG, f32))
    wide = n_valid > kf
    klo0 = jnp.where(wide, _f2k(jnp.maximum(rowmin, NEG)), kneg)
    khi0 = jnp.where(wide, _f2k(rowmax) + 1, klo0 + 1)
    clo0 = n_valid

    def search_cond(st):
        return st[0] > 0.5

    def search_step(st):
        _, it, klo, khi, clo = st
        lo_f, hi_f = _k2f(klo), _k2f(khi)
        k_val = _f2k(lo_f + (hi_f - lo_f) * 0.5)
        k_mid = (klo >> 1) + (khi >> 1) + (klo & khi & 1)
        kt = jnp.where(it < VALUE_STEPS, k_val, k_mid)
        kt = jnp.minimum(jnp.maximum(kt, klo + 1), khi - 1)
        cand = _k2f(kt)
        c_proc = count_ge(cand)
        ge = total_of(c_proc, cand) >= kf
        act = active_of(klo, khi, clo)
        up = act & ge
        dn = act & jnp.logical_not(ge)
        klo = jnp.where(up, kt, klo)
        clo = jnp.where(up, c_proc, clo)
        khi = jnp.where(dn, kt, khi)
        left = jnp.max(jnp.where(active_of(klo, khi, clo), 1.0, 0.0))
        return left, it + 1, klo, khi, clo

    left0 = jnp.max(jnp.where(active_of(klo0, khi0, clo0), 1.0, 0.0))
    _, _, klo, _, clo = lax.while_loop(search_cond, search_step,
                                       (left0, jnp.int32(0), klo0, khi0, clo0))
    tau = _k2f(klo)
    tau_ref[...] = tau

    tied = clo > kf

    @pl.when(_any(tied))
    def _():
        need = kf - count_ge(_k2f(klo + 1))

        def idx_step(it, pfx):
            trial = pfx | jnp.left_shift(jnp.int32(1), 13 - it)

            def body(j, acc):
                c0 = lanei + j * BLK
                h0 = (sc_ref[j, :, :LANES] == tau) & (c0 < trial)
                h1 = (sc_ref[j, :, LANES:] == tau) & (c0 + LANES < trial)
                return acc + jnp.where(h0, 1.0, 0.0) + jnp.where(h1, 1.0, 0.0)

            acc = lax.fori_loop(0, nt, body, jnp.zeros((BLK, LANES), f32))
            below = jnp.dot(acc.astype(bf16), ones, preferred_element_type=f32)
            return jnp.where(below < need, trial, pfx)

        last = lax.fori_loop(0, 14, idx_step, jnp.zeros((BLK, LANES), jnp.int32))
        last = jnp.where(tied, last, jnp.int32(1 << 30))

        def drop(j, c):
            c0 = lanei + j * BLK
            x0 = sc_ref[j, :, :LANES]
            x1 = sc_ref[j, :, LANES:]
            sc_ref[j, :, :LANES] = jnp.where((x0 == tau) & (c0 > last), -jnp.inf, x0)
            sc_ref[j, :, LANES:] = jnp.where((x1 == tau) & (c0 + LANES > last), -jnp.inf, x1)
            return c

        lax.fori_loop(0, nt, drop, 0)

    @pl.when(g == 0)
    def _():
        kn_ref[...] = jnp.zeros(kn_ref.shape, f32)

        def norms(j, c):
            r = pl.multiple_of(j * BLK, BLK)
            for hd in range(ATTN_HEADS):
                kh = k_ref[0, pl.ds(r, BLK), hd * LANES:(hd + 1) * LANES].astype(f32)
                n2 = jnp.dot((kh * kh).astype(bf16), ones, preferred_element_type=f32)
                kn_ref[hd] = jnp.maximum(kn_ref[hd], jnp.max(n2, axis=0, keepdims=True))
            return c

        lax.fori_loop(0, n_rows_total // BLK, norms, 0)

    for hd in range(ATTN_HEADS):
        qh = q_ref[0, :, hd * LANES:(hd + 1) * LANES].astype(f32)
        qn2 = jnp.dot((qh * qh).astype(bf16), ones, preferred_element_type=f32)
        mb_ref[hd] = jnp.sqrt(qn2 * kn_ref[hd]) * BOUND_SLACK
    ls_ref[...] = jnp.zeros(ls_ref.shape, f32)
    acc_ref[...] = jnp.zeros(acc_ref.shape, f32)

    def attend(j, c):
        r = pl.multiple_of(j * BLK, BLK)
        t = tau_ref[...]
        b0 = jnp.where(sc_ref[j, :, :LANES] >= t, 0.0, NEG)
        b1 = jnp.where(sc_ref[j, :, LANES:] >= t, 0.0, NEG)
        for hd in range(ATTN_HEADS):
            sl = slice(hd * LANES, (hd + 1) * LANES)
            kh = k_ref[0, pl.ds(r, BLK), sl]
            vh = v_ref[0, pl.ds(r, BLK), sl]
            s = lax.dot_general(q_ref[0, :, sl], kh, _NT, preferred_element_type=f32)
            mb = mb_ref[hd]
            p0 = jnp.exp(s[:, :LANES] - mb + b0)
            p1 = jnp.exp(s[:, LANES:] - mb + b1)
            ls_ref[hd] += p0 + p1
            p = jnp.concatenate([p0, p1], axis=1).astype(bf16)
            acc_ref[hd] += jnp.dot(p, vh, preferred_element_type=f32)
        return c

    lax.fori_loop(0, nt, attend, 0)
    bad = jnp.float32(0.0)
    for hd in range(ATTN_HEADS):
        l = jnp.sum(ls_ref[hd], axis=1, keepdims=True)
        bad = jnp.maximum(bad, jnp.max(jnp.where(l >= L_MIN, 0.0, 1.0)))
        o_ref[0, :, hd * LANES:(hd + 1) * LANES] = (acc_ref[hd] / l).astype(bf16)

    @pl.when(bad > 0.5)
    def _():
        m_ref[...] = jnp.full(m_ref.shape, NEG, f32)
        l_ref[...] = jnp.zeros(l_ref.shape, f32)
        acc_ref[...] = jnp.zeros(acc_ref.shape, f32)

        def attend_online(j, c):
            r = pl.multiple_of(j * BLK, BLK)
            t = tau_ref[...]
            b0 = jnp.where(sc_ref[j, :, :LANES] >= t, 0.0, NEG)
            b1 = jnp.where(sc_ref[j, :, LANES:] >= t, 0.0, NEG)
            bias = jnp.concatenate([b0, b1], axis=1)
            for hd in range(ATTN_HEADS):
                sl = slice(hd * LANES, (hd + 1) * LANES)
                kh = k_ref[0, pl.ds(r, BLK), sl]
                vh = v_ref[0, pl.ds(r, BLK), sl]
                s = lax.dot_general(q_ref[0, :, sl], kh, _NT, preferred_element_type=f32) + bias
                m_old = m_ref[hd]
                m_new = jnp.maximum(m_old, jnp.max(s, axis=1, keepdims=True))
                p = jnp.exp(s - m_new)
                alpha = jnp.exp(m_old - m_new)
                l_ref[hd] = alpha * l_ref[hd] + jnp.sum(p, axis=1, keepdims=True)
                acc_ref[hd] = alpha * acc_ref[hd] + jnp.dot(p.astype(bf16), vh,
                                                            preferred_element_type=f32)
                m_ref[hd] = m_new
            return c

        lax.fori_loop(0, nt, attend_online, 0)
        for hd in range(ATTN_HEADS):
            o_ref[0, :, hd * LANES:(hd + 1) * LANES] = (acc_ref[hd] / l_ref[hd]).astype(bf16)


def _mlp_kernel(x_ref, a_ref, c_ref, woa_ref, woc_ref, g2_ref, wu_ref, wd_ref, gf_ref, o_ref):
    h = (x_ref[0]
         + jnp.dot(a_ref[0], woa_ref[...], preferred_element_type=jnp.float32)
         + jnp.dot(c_ref[0], woc_ref[...], preferred_element_type=jnp.float32))
    hn = _rms(h, g2_ref[...]).astype(jnp.bfloat16)
    acc = jnp.zeros_like(h)
    for c in range(D_FF // FF_CHUNK):
        sl = slice(c * FF_CHUNK, (c + 1) * FF_CHUNK)
        u = jnp.dot(hn, wu_ref[:, sl], preferred_element_type=jnp.float32)
        u = jnp.square(jnp.maximum(u, 0.0)).astype(jnp.bfloat16)
        acc = acc + jnp.dot(u, wd_ref[sl, :], preferred_element_type=jnp.float32)
    o_ref[0] = _rms(h + acc, gf_ref[...])


def _rope_tables(n_rows, head_dim, lanes_per_head):
    rot = head_dim // ROPE_FRACTION
    half = rot // 2
    inv = ROPE_THETA ** (-jnp.arange(0, rot, 2, dtype=jnp.float32) / rot)
    pos = jnp.arange(n_rows, dtype=jnp.float32) - float(PAD_ROWS)
    ang = pos[:, None] * inv[None, :]
    cos, sin = jnp.cos(ang), jnp.sin(ang)
    rest = lanes_per_head - rot
    one = jnp.ones((n_rows, rest), jnp.float32)
    zero = jnp.zeros((n_rows, rest), jnp.float32)
    zh = jnp.zeros((n_rows, half), jnp.float32)
    c = jnp.concatenate([cos, cos, one], axis=1)
    s1 = jnp.concatenate([zh, sin, zero], axis=1)
    s2 = jnp.concatenate([-sin, zh, zero], axis=1)
    reps = LANES // lanes_per_head
    return jnp.stack([jnp.tile(t, (1, reps)) for t in (c, s1, s2)])


def _const(shape):
    return pl.BlockSpec(shape, lambda b, i: (0,) * len(shape), pipeline_mode=pl.Buffered(1))


def _project(x, meta_tokens, attn_norm_g, w_in, conv_w, conv_b, conv_norm_g, conv_norm_b):
    B, S, D = x.shape
    nb = S // BLK
    T = S + BLK
    f32, bf16 = jnp.float32, jnp.bfloat16

    w = w_in[0]
    w_rep = jnp.concatenate(
        [w[:, 0:2048], w[:, 2120:3144], w[:, 2048:2120],
         jnp.zeros((D, W_IN_COLS - 3144), f32)], axis=1).astype(bf16)
    meta_pad = jnp.concatenate([jnp.zeros((PAD_ROWS, D), f32), meta_tokens.astype(f32)], axis=0)
    rq = _rope_tables(T, ATTN_HEAD_DIM, LANES)
    ri = _rope_tables(T, IDX_HEAD_DIM, IDX_HEAD_DIM)

    real = lambda b, i: (b, jnp.maximum(i - 1, 0), 0)
    padded = lambda b, i: (b, i, 0)
    return pl.pallas_call(
        _proj_kernel,
        grid=(B, nb + 1),
        in_specs=[
            pl.BlockSpec((1, BLK, D), real),
            _const((BLK, D)),
            _const((1, D)),
            _const((D, W_IN_COLS)),
            pl.BlockSpec((3, BLK, LANES), lambda b, i: (0, i, 0)),
            pl.BlockSpec((3, BLK, LANES), lambda b, i: (0, i, 0)),
            _const((CONV_WIDTH, CONV_CH)),
            _const((1, CONV_CH)),
            _const((1, CONV_CH)),
            _const((1, CONV_CH)),
        ],
        out_specs=[
            pl.BlockSpec((1, BLK, ATTN_W), real),
            pl.BlockSpec((1, BLK, ATTN_W), real),
            pl.BlockSpec((1, BLK, LANES), real),
            pl.BlockSpec((1, BLK, CONV_CH), real),
            pl.BlockSpec((1, BLK, ATTN_W), padded),
            pl.BlockSpec((1, BLK, ATTN_W), padded),
            pl.BlockSpec((1, BLK, LANES), padded),
            pl.BlockSpec((1, BLK, LANES), padded),
        ],
        out_shape=[
            jax.ShapeDtypeStruct((B, S, ATTN_W), bf16),
            jax.ShapeDtypeStruct((B, S, ATTN_W), bf16),
            jax.ShapeDtypeStruct((B, S, LANES), f32),
            jax.ShapeDtypeStruct((B, S, CONV_CH), bf16),
            jax.ShapeDtypeStruct((B, T, ATTN_W), bf16),
            jax.ShapeDtypeStruct((B, T, ATTN_W), bf16),
            jax.ShapeDtypeStruct((B, T, LANES), bf16),
            jax.ShapeDtypeStruct((B, T, LANES), bf16),
        ],
        scratch_shapes=[pltpu.VMEM((CARRY + BLK, CONV_CH), f32)],
        compiler_params=pltpu.CompilerParams(
            dimension_semantics=("arbitrary", "arbitrary"), vmem_limit_bytes=VMEM_LIMIT),
        name="proj_rope_conv",
    )(x, meta_pad, attn_norm_g, w_rep, rq, ri, conv_w[0], conv_b, conv_norm_g, conv_norm_b)


def _attend(q, iq, misc, ika, ikb, k, v):
    B, S, _ = q.shape
    T = k.shape[1]
    nb = S // BLK
    f32 = jnp.float32
    blk = lambda b, g: (b, g, 0)
    whole = lambda width: pl.BlockSpec((1, T, width), lambda b, g: (b, 0, 0),
                                       pipeline_mode=pl.Buffered(1))
    return pl.pallas_call(
        functools.partial(_dsa_kernel, n_rows_total=T),
        grid=(B, nb),
        in_specs=[
            pl.BlockSpec((1, BLK, ATTN_W), blk),
            pl.BlockSpec((1, BLK, ATTN_W), blk),
            pl.BlockSpec((1, BLK, LANES), blk),
            whole(LANES), whole(LANES), whole(ATTN_W), whole(ATTN_W),
        ],
        out_specs=pl.BlockSpec((1, BLK, ATTN_W), blk),
        out_shape=jax.ShapeDtypeStruct((B, S, ATTN_W), jnp.bfloat16),
        scratch_shapes=[
            pltpu.VMEM((nb + 1, BLK, BLK), f32),
            pltpu.VMEM((IDX_HEADS, BLK, LANES), f32),
            pltpu.VMEM((BLK, LANES), f32),
            pltpu.VMEM((BLK, LANES), f32),
            pltpu.VMEM((BLK, LANES), f32),
            pltpu.VMEM((ATTN_HEADS, 1, LANES), f32),
            pltpu.VMEM((ATTN_HEADS, BLK, LANES), f32),
            pltpu.VMEM((ATTN_HEADS, BLK, LANES), f32),
            pltpu.VMEM((ATTN_HEADS, BLK, ATTN_HEAD_DIM), f32),
            pltpu.VMEM((ATTN_HEADS, BLK, 1), f32),
            pltpu.VMEM((ATTN_HEADS, BLK, 1), f32),
        ],
        compiler_params=pltpu.CompilerParams(
            dimension_semantics=("arbitrary", "arbitrary"), vmem_limit_bytes=VMEM_LIMIT),
        name="dsa_attention",
    )(q, iq, misc, ika, ikb, k, v)


def _out_mlp(x, attn, conv, w_out, mlp_norm_g, w_up, w_down, final_norm_g):
    B, S, D = x.shape
    bf16 = jnp.bfloat16
    wo = w_out[0].astype(bf16)
    rows = lambda b, i: (b, i, 0)
    return pl.pallas_call(
        _mlp_kernel,
        grid=(B, S // MLP_ROWS),
        in_specs=[
            pl.BlockSpec((1, MLP_ROWS, D), rows),
            pl.BlockSpec((1, MLP_ROWS, ATTN_W), rows),
            pl.BlockSpec((1, MLP_ROWS, CONV_CH), rows),
            _const((ATTN_W, D)),
            _const((CONV_CH, D)),
            _const((1, D)),
            _const((D, D_FF)),
            _const((D_FF, D)),
            _const((1, D)),
        ],
        out_specs=pl.BlockSpec((1, MLP_ROWS, D), rows),
        out_shape=jax.ShapeDtypeStruct((B, S, D), x.dtype),
        compiler_params=pltpu.CompilerParams(
            dimension_semantics=("arbitrary", "arbitrary"), vmem_limit_bytes=VMEM_LIMIT),
        name="out_mlp",
    )(x, attn, conv, wo[:ATTN_W], wo[ATTN_W:], mlp_norm_g, w_up[0].astype(bf16),
      w_down[0].astype(bf16), final_norm_g.reshape(1, D))


def kernel(x, meta_tokens, attn_norm_g, w_in, conv_w, conv_b, conv_norm_g, conv_norm_b,
           w_out, mlp_norm_g, w_up, w_down, final_norm_g):
    B, S, D = x.shape
    assert D == D_MODEL and S % MLP_ROWS == 0 and attn_norm_g.shape[0] == 1
    q, iq, misc, conv, k, v, ika, ikb = _project(
        x, meta_tokens, attn_norm_g, w_in, conv_w, conv_b, conv_norm_g, conv_norm_b)
    attn = _attend(q, iq, misc, ika, ikb, k, v)
    return _out_mlp(x, attn, conv, w_out, mlp_norm_g, w_up, w_down, final_norm_g)
```

```python
import functools

import jax
import jax.numpy as jnp
from jax import lax
from jax.experimental import pallas as pl
from jax.experimental.pallas import tpu as pltpu

D_MODEL = 1024
N_META = 16
ATTN_HEADS = 4
ATTN_HEAD_DIM = 128
ATTN_W = ATTN_HEADS * ATTN_HEAD_DIM
ROPE_THETA = 500000.0
ROPE_FRACTION = 4
IDX_HEADS = 8
IDX_HEAD_DIM = 64
INDEX_TOPK = 256
CONV_CH = 512
CONV_GROUPS = 4
CONV_WIDTH = 31
D_FF = 4 * D_MODEL
NORM_EPS = 1e-5
NEG = -1e30

LANES = 128
SUBLANES = 8
BLK = 256
STRIPS = BLK // SUBLANES
PAD_ROWS = BLK - N_META
CARRY = 32
W_IN_COLS = 6 * 512 + LANES
VALUE_STEPS = 8
N_ACC = 4
BOUND_SLACK = 1.0 + 2.0 ** -6
L_MIN = 1e-30
MLP_ROWS = 512
FF_CHUNK = 1024
VMEM_LIMIT = 60 * 1024 * 1024

_NT = (((1,), (1,)), ((), ()))


def _rms(x, g):
    return x * lax.rsqrt(jnp.mean(x * x, axis=-1, keepdims=True) + NORM_EPS) * g


def _rope(x, c, s1, s2, half):
    return x * c + pltpu.roll(x, half, 1) * s1 + pltpu.roll(x, LANES - half, 1) * s2


def _proj_kernel(x_ref, meta_ref, g_ref, w_ref, rq_ref, ri_ref, cw_ref, cb_ref, lg_ref, lb_ref,
                 q_ref, iq_ref, misc_ref, conv_ref, k_ref, v_ref, ika_ref, ikb_ref, u_scr):
    i = pl.program_id(1)
    h = jnp.where(i == 0, meta_ref[...], x_ref[0])
    xn = _rms(h, g_ref[...]).astype(jnp.bfloat16)
    proj = jnp.dot(xn, w_ref[...], preferred_element_type=jnp.float32)

    cq, s1q, s2q = rq_ref[0], rq_ref[1], rq_ref[2]
    ci, s1i, s2i = ri_ref[0], ri_ref[1], ri_ref[2]
    q_half = ATTN_HEAD_DIM // ROPE_FRACTION // 2
    i_half = IDX_HEAD_DIM // ROPE_FRACTION // 2
    for hd in range(ATTN_HEADS):
        sl = slice(hd * LANES, (hd + 1) * LANES)
        qh = _rope(proj[:, sl], cq, s1q, s2q, q_half) * (ATTN_HEAD_DIM ** -0.5)
        q_ref[0, :, sl] = qh.astype(jnp.bfloat16)
        kh = _rope(proj[:, 512 + hd * LANES:512 + (hd + 1) * LANES], cq, s1q, s2q, q_half)
        k_ref[0, :, sl] = kh.astype(jnp.bfloat16)
        ih = _rope(proj[:, 1536 + hd * LANES:1536 + (hd + 1) * LANES], ci, s1i, s2i, i_half)
        iq_ref[0, :, sl] = (ih * (IDX_HEAD_DIM ** -0.5)).astype(jnp.bfloat16)
    v_ref[0] = proj[:, 1024:1536].astype(jnp.bfloat16)

    lane = lax.broadcasted_iota(jnp.int32, (BLK, LANES), 1)
    is_ik = lane < IDX_HEAD_DIM
    slab = proj[:, 3072:3072 + LANES]
    roped = _rope(slab, jnp.where(is_ik, ci, 1.0), jnp.where(is_ik, s1i, 0.0),
                  jnp.where(is_ik, s2i, 0.0), i_half)
    misc_ref[0] = jnp.where(is_ik, roped, slab * (IDX_HEADS ** -0.5))
    ika = jnp.where(is_ik, roped, 0.0)
    ika_ref[0] = ika.astype(jnp.bfloat16)
    ikb_ref[0] = pltpu.roll(ika, IDX_HEAD_DIM, 1).astype(jnp.bfloat16)

    @pl.when(i == 0)
    def _():
        u_scr[0:CARRY, :] = jnp.zeros((CARRY, CONV_CH), jnp.float32)

    ga = proj[:, 2048:2560]
    gg = proj[:, 2560:3072]
    u_scr[CARRY:CARRY + BLK, :] = ga * jax.nn.sigmoid(gg)
    y = jnp.zeros((BLK, CONV_CH), jnp.float32) + cb_ref[...]
    for j in range(CONV_WIDTH):
        off = CARRY - (CONV_WIDTH - 1) + j
        y = y + u_scr[off:off + BLK, :] * cw_ref[j:j + 1, :]
    u_scr[0:CARRY, :] = u_scr[BLK:BLK + CARRY, :]
    gsz = CONV_CH // CONV_GROUPS
    for gi in range(CONV_GROUPS):
        sl = slice(gi * gsz, (gi + 1) * gsz)
        yg = y[:, sl]
        mu = jnp.mean(yg, axis=-1, keepdims=True)
        d = yg - mu
        var = jnp.mean(d * d, axis=-1, keepdims=True)
        yn = d * lax.rsqrt(var + NORM_EPS) * lg_ref[:, sl] + lb_ref[:, sl]
        conv_ref[0, :, sl] = (yn * jax.nn.sigmoid(yn)).astype(jnp.bfloat16)


def _f2k(f):
    b = lax.bitcast_convert_type(f, jnp.int32)
    return b ^ ((b >> 31) & jnp.int32(0x7FFFFFFF))


def _k2f(k):
    return lax.bitcast_convert_type(k ^ ((k >> 31) & jnp.int32(0x7FFFFFFF)), jnp.float32)


def _any(mask):
    return jnp.max(jnp.where(mask, 1.0, 0.0)) > 0.5


def _rows8(x):
    t = jnp.sum(jnp.sum(x, axis=0), axis=0, keepdims=True)
    return jnp.broadcast_to(t, (SUBLANES, BLK))


def _dsa_kernel(q_ref, iq_ref, misc_ref, ika_ref, ikb_ref, k_ref, v_ref, o_ref,
                sc_ref, tau_ref, mx_ref, mn_ref, kn_ref, mb_ref, ls_ref, acc_ref,
                m_ref, l_ref, *, n_rows_total):
    f32, bf16 = jnp.float32, jnp.bfloat16
    g = pl.program_id(1)
    nt = g + 2
    q0 = (g + 1) * BLK
    ones = jnp.ones((LANES, LANES), bf16)
    shape3 = (STRIPS, SUBLANES, BLK)
    keyi = (lax.broadcasted_iota(jnp.int32, shape3, 0) * SUBLANES
            + lax.broadcasted_iota(jnp.int32, shape3, 1))
    qryi = lax.broadcasted_iota(jnp.int32, shape3, 2)

    misc_t = misc_ref[0].T
    w8 = [jnp.broadcast_to(misc_t[IDX_HEAD_DIM + hd:IDX_HEAD_DIM + hd + 1, :], (SUBLANES, BLK))
          for hd in range(IDX_HEADS)]

    def score_tile(j):
        r = pl.multiple_of(j * BLK, BLK)
        ka = ika_ref[0, pl.ds(r, BLK), :]
        kb = ikb_ref[0, pl.ds(r, BLK), :]
        acc = jnp.zeros(shape3, f32)
        for p in range(IDX_HEADS // 2):
            slab = iq_ref[0, :, p * LANES:(p + 1) * LANES]
            le = lax.dot_general(ka, slab, _NT, preferred_element_type=f32).reshape(shape3)
            lo = lax.dot_general(kb, slab, _NT, preferred_element_type=f32).reshape(shape3)
            acc = (acc + jnp.maximum(le, 0.0) * w8[2 * p][None]
                   + jnp.maximum(lo, 0.0) * w8[2 * p + 1][None])
        return acc

    def put_tile(j, s, valid):
        lo = s if valid is None else jnp.where(valid, s, -jnp.inf)
        hi = s if valid is None else jnp.where(valid, s, jnp.inf)
        sc_ref[j] = lo
        mx_ref[...] = jnp.maximum(mx_ref[...], jnp.max(lo, axis=0))
        mn_ref[...] = jnp.minimum(mn_ref[...], jnp.min(hi, axis=0))

    mx_ref[...] = jnp.full((SUBLANES, BLK), -jnp.inf, f32)
    mn_ref[...] = jnp.full((SUBLANES, BLK), jnp.inf, f32)
    put_tile(0, score_tile(0), keyi >= PAD_ROWS)
    put_tile(nt - 1, score_tile(nt - 1), keyi <= qryi)

    def interior(j, c):
        put_tile(j, score_tile(j), None)
        return c

    lax.fori_loop(1, nt - 1, interior, 0)

    kf = float(INDEX_TOPK)
    qrow = q0 + lax.broadcasted_iota(jnp.int32, (SUBLANES, BLK), 1)
    n_valid = (qrow - (PAD_ROWS - 1)).astype(f32)
    n_virtual = (n_rows_total - 1 - qrow).astype(f32)
    rowmax = jnp.broadcast_to(jnp.max(mx_ref[...], axis=0, keepdims=True), (SUBLANES, BLK))
    rowmin = jnp.broadcast_to(jnp.min(mn_ref[...], axis=0, keepdims=True), (SUBLANES, BLK))
    zacc = jnp.zeros((N_ACC, SUBLANES, BLK), f32)

    def partial(hit):
        return jnp.sum(jnp.where(hit, 1.0, 0.0).reshape(STRIPS // N_ACC, N_ACC, SUBLANES, BLK), axis=0)

    def count_ge(cand):
        def body(j, acc):
            return acc + partial(sc_ref[j] >= cand[None])

        return _rows8(lax.fori_loop(0, nt, body, zacc))

    def total_of(c_proc, cand):
        return c_proc + jnp.where(cand <= NEG, n_virtual, 0.0)

    def active_of(klo, khi, clo):
        return (total_of(clo, _k2f(klo)) != kf) & (khi > klo + 1)

    kneg = _f2k(jnp.full((SUBLANES, BLK), NEG, f32))
    wide = n_valid > kf
    klo = jnp.where(wide, _f2k(jnp.maximum(rowmin, NEG)), kneg)
    khi = jnp.where(wide, _f2k(rowmax) + 1, klo + 1)
    clo = n_valid
    chi = jnp.zeros((SUBLANES, BLK), f32)

    def zero_body(j, accs):
        x = sc_ref[j]
        return accs[0] + partial(x >= 0.0), accs[1] + partial(x > 0.0)

    age, agt = lax.fori_loop(0, nt, zero_body, (zacc, zacc))
    c_ge0, c_gt0 = _rows8(age), _rows8(agt)
    kzero = _f2k(jnp.zeros((SUBLANES, BLK), f32))
    inside = (klo < kzero) & (kzero < khi)
    up = inside & (c_ge0 >= kf)
    dn = inside & (c_ge0 < kf)
    pin = up & (c_gt0 < kf)
    klo = jnp.where(up, kzero, klo)
    clo = jnp.where(up, c_ge0, clo)
    khi = jnp.where(dn, kzero, jnp.where(pin, kzero + 1, khi))
    chi = jnp.where(dn, c_ge0, jnp.where(pin, c_gt0, chi))

    def search_cond(st):
        return st[0] > 0.5

    def search_step(st):
        _, it, klo, khi, clo, chi = st
        lo_f, hi_f = _k2f(klo), _k2f(khi)
        k_val = _f2k(lo_f + (hi_f - lo_f) * 0.5)
        k_mid = (klo >> 1) + (khi >> 1) + (klo & khi & 1)
        kt = jnp.where(it < VALUE_STEPS, k_val, k_mid)
        kt = jnp.minimum(jnp.maximum(kt, klo + 1), khi - 1)
        cand = _k2f(kt)
        c_proc = count_ge(cand)
        ge = total_of(c_proc, cand) >= kf
        act = active_of(klo, khi, clo)
        up = act & ge
        dn = act & jnp.logical_not(ge)
        klo = jnp.where(up, kt, klo)
        clo = jnp.where(up, c_proc, clo)
        khi = jnp.where(dn, kt, khi)
        chi = jnp.where(dn, c_proc, chi)
        left = jnp.max(jnp.where(active_of(klo, khi, clo), 1.0, 0.0))
        return left, it + 1, klo, khi, clo, chi

    left0 = jnp.max(jnp.where(active_of(klo, khi, clo), 1.0, 0.0))
    _, _, klo, khi, clo, chi = lax.while_loop(
        search_cond, search_step, (left0, jnp.int32(0), klo, khi, clo, chi))
    tau = _k2f(klo)
    tau_ref[...] = tau

    tied = clo > kf

    @pl.when(_any(tied))
    def _():
        need = (kf - chi)[0:1, :]
        tied1 = tied[0:1, :]
        tau1 = tau[0:1, :]
        tri = (lax.broadcasted_iota(jnp.int32, (BLK, BLK), 0)
               >= lax.broadcasted_iota(jnp.int32, (BLK, BLK), 1)).astype(bf16)

        def fix(j, before):
            x = sc_ref[j].reshape(BLK, BLK)
            eq = x == tau1
            rank = jnp.dot(tri, jnp.where(eq, 1.0, 0.0).astype(bf16),
                           preferred_element_type=f32) + before
            cut = eq & (rank > need) & tied1
            sc_ref[j] = jnp.where(cut, -jnp.inf, x).reshape(shape3)
            return rank[BLK - 1:BLK, :]

        lax.fori_loop(0, nt, fix, jnp.zeros((1, BLK), f32))

    @pl.when(g == 0)
    def _():
        kn_ref[...] = jnp.zeros(kn_ref.shape, f32)

        def norms(j, c):
            r = pl.multiple_of(j * BLK, BLK)
            for hd in range(ATTN_HEADS):
                kh = k_ref[0, pl.ds(r, BLK), hd * LANES:(hd + 1) * LANES].astype(f32)
                n2 = jnp.dot((kh * kh).astype(bf16), ones, preferred_element_type=f32)
                kn_ref[hd] = jnp.maximum(kn_ref[hd], jnp.max(n2, axis=0, keepdims=True))
            return c

        lax.fori_loop(0, n_rows_total // BLK, norms, 0)

    for hd in range(ATTN_HEADS):
        qh = q_ref[0, :, hd * LANES:(hd + 1) * LANES].astype(f32)
        qn2 = jnp.dot((qh * qh).astype(bf16), ones, preferred_element_type=f32)
        mb_ref[hd] = jnp.sqrt(qn2 * kn_ref[hd]) * BOUND_SLACK
    ls_ref[...] = jnp.zeros(ls_ref.shape, f32)
    acc_ref[...] = jnp.zeros(acc_ref.shape, f32)

    def bias_tile(j):
        sel = sc_ref[j] >= tau_ref[...][None]
        return jnp.where(sel, 0.0, NEG).reshape(BLK, BLK).T

    def attend(j, c):
        r = pl.multiple_of(j * BLK, BLK)
        bias = bias_tile(j)
        b0 = bias[:, :LANES]
        b1 = bias[:, LANES:]
        for hd in range(ATTN_HEADS):
            sl = slice(hd * LANES, (hd + 1) * LANES)
            kh = k_ref[0, pl.ds(r, BLK), sl]
            vh = v_ref[0, pl.ds(r, BLK), sl]
            s = lax.dot_general(q_ref[0, :, sl], kh, _NT, preferred_element_type=f32)
            mb = mb_ref[hd]
            p0 = jnp.exp(s[:, :LANES] - mb + b0)
            p1 = jnp.exp(s[:, LANES:] - mb + b1)
            ls_ref[hd] += p0 + p1
            p = jnp.concatenate([p0, p1], axis=1).astype(bf16)
            acc_ref[hd] += jnp.dot(p, vh, preferred_element_type=f32)
        return c

    lax.fori_loop(0, nt, attend, 0)
    bad = jnp.float32(0.0)
    for hd in range(ATTN_HEADS):
        l = jnp.sum(ls_ref[hd], axis=1, keepdims=True)
        bad = jnp.maximum(bad, jnp.max(jnp.where(l >= L_MIN, 0.0, 1.0)))
        o_ref[0, :, hd * LANES:(hd + 1) * LANES] = (acc_ref[hd] / l).astype(bf16)

    @pl.when(bad > 0.5)
    def _():
        m_ref[...] = jnp.full(m_ref.shape, NEG, f32)
        l_ref[...] = jnp.zeros(l_ref.shape, f32)
        acc_ref[...] = jnp.zeros(acc_ref.shape, f32)

        def attend_online(j, c):
            r = pl.multiple_of(j * BLK, BLK)
            bias = bias_tile(j)
            for hd in range(ATTN_HEADS):
                sl = slice(hd * LANES, (hd + 1) * LANES)
                kh = k_ref[0, pl.ds(r, BLK), sl]
                vh = v_ref[0, pl.ds(r, BLK), sl]
                s = lax.dot_general(q_ref[0, :, sl], kh, _NT, preferred_element_type=f32) + bias
                m_old = m_ref[hd]
                m_new = jnp.maximum(m_old, jnp.max(s, axis=1, keepdims=True))
                p = jnp.exp(s - m_new)
                alpha = jnp.exp(m_old - m_new)
                l_ref[hd] = alpha * l_ref[hd] + jnp.sum(p, axis=1, keepdims=True)
                acc_ref[hd] = alpha * acc_ref[hd] + jnp.dot(p.astype(bf16), vh,
                                                            preferred_element_type=f32)
                m_ref[hd] = m_new
            return c

        lax.fori_loop(0, nt, attend_online, 0)
        for hd in range(ATTN_HEADS):
            o_ref[0, :, hd * LANES:(hd + 1) * LANES] = (acc_ref[hd] / l_ref[hd]).astype(bf16)


def _mlp_kernel(x_ref, a_ref, c_ref, woa_ref, woc_ref, g2_ref, wu_ref, wd_ref, gf_ref, o_ref):
    h = (x_ref[0]
         + jnp.dot(a_ref[0], woa_ref[...], preferred_element_type=jnp.float32)
         + jnp.dot(c_ref[0], woc_ref[...], preferred_element_type=jnp.float32))
    hn = _rms(h, g2_ref[...]).astype(jnp.bfloat16)
    acc = jnp.zeros_like(h)
    for c in range(D_FF // FF_CHUNK):
        sl = slice(c * FF_CHUNK, (c + 1) * FF_CHUNK)
        u = jnp.dot(hn, wu_ref[:, sl], preferred_element_type=jnp.float32)
        u = jnp.square(jnp.maximum(u, 0.0)).astype(jnp.bfloat16)
        acc = acc + jnp.dot(u, wd_ref[sl, :], preferred_element_type=jnp.float32)
    o_ref[0] = _rms(h + acc, gf_ref[...])


def _rope_tables(n_rows, head_dim, lanes_per_head):
    rot = head_dim // ROPE_FRACTION
    half = rot // 2
    inv = ROPE_THETA ** (-jnp.arange(0, rot, 2, dtype=jnp.float32) / rot)
    pos = jnp.arange(n_rows, dtype=jnp.float32) - float(PAD_ROWS)
    ang = pos[:, None] * inv[None, :]
    cos, sin = jnp.cos(ang), jnp.sin(ang)
    rest = lanes_per_head - rot
    one = jnp.ones((n_rows, rest), jnp.float32)
    zero = jnp.zeros((n_rows, rest), jnp.float32)
    zh = jnp.zeros((n_rows, half), jnp.float32)
    c = jnp.concatenate([cos, cos, one], axis=1)
    s1 = jnp.concatenate([zh, sin, zero], axis=1)
    s2 = jnp.concatenate([-sin, zh, zero], axis=1)
    reps = LANES // lanes_per_head
    return jnp.stack([jnp.tile(t, (1, reps)) for t in (c, s1, s2)])


def _const(shape):
    return pl.BlockSpec(shape, lambda b, i: (0,) * len(shape), pipeline_mode=pl.Buffered(1))


def _project(x, meta_tokens, attn_norm_g, w_in, conv_w, conv_b, conv_norm_g, conv_norm_b):
    B, S, D = x.shape
    nb = S // BLK
    T = S + BLK
    f32, bf16 = jnp.float32, jnp.bfloat16

    w = w_in[0]
    w_rep = jnp.concatenate(
        [w[:, 0:2048], w[:, 2120:3144], w[:, 2048:2120],
         jnp.zeros((D, W_IN_COLS - 3144), f32)], axis=1).astype(bf16)
    meta_pad = jnp.concatenate([jnp.zeros((PAD_ROWS, D), f32), meta_tokens.astype(f32)], axis=0)
    rq = _rope_tables(T, ATTN_HEAD_DIM, LANES)
    ri = _rope_tables(T, IDX_HEAD_DIM, IDX_HEAD_DIM)

    real = lambda b, i: (b, jnp.maximum(i - 1, 0), 0)
    padded = lambda b, i: (b, i, 0)
    return pl.pallas_call(
        _proj_kernel,
        grid=(B, nb + 1),
        in_specs=[
            pl.BlockSpec((1, BLK, D), real),
            _const((BLK, D)),
            _const((1, D)),
            _const((D, W_IN_COLS)),
            pl.BlockSpec((3, BLK, LANES), lambda b, i: (0, i, 0)),
            pl.BlockSpec((3, BLK, LANES), lambda b, i: (0, i, 0)),
            _const((CONV_WIDTH, CONV_CH)),
            _const((1, CONV_CH)),
            _const((1, CONV_CH)),
            _const((1, CONV_CH)),
        ],
        out_specs=[
            pl.BlockSpec((1, BLK, ATTN_W), real),
            pl.BlockSpec((1, BLK, ATTN_W), real),
            pl.BlockSpec((1, BLK, LANES), real),
            pl.BlockSpec((1, BLK, CONV_CH), real),
            pl.BlockSpec((1, BLK, ATTN_W), padded),
            pl.BlockSpec((1, BLK, ATTN_W), padded),
            pl.BlockSpec((1, BLK, LANES), padded),
            pl.BlockSpec((1, BLK, LANES), padded),
        ],
        out_shape=[
            jax.ShapeDtypeStruct((B, S, ATTN_W), bf16),
            jax.ShapeDtypeStruct((B, S, ATTN_W), bf16),
            jax.ShapeDtypeStruct((B, S, LANES), f32),
            jax.ShapeDtypeStruct((B, S, CONV_CH), bf16),
            jax.ShapeDtypeStruct((B, T, ATTN_W), bf16),
            jax.ShapeDtypeStruct((B, T, ATTN_W), bf16),
            jax.ShapeDtypeStruct((B, T, LANES), bf16),
            jax.ShapeDtypeStruct((B, T, LANES), bf16),
        ],
        scratch_shapes=[pltpu.VMEM((CARRY + BLK, CONV_CH), f32)],
        compiler_params=pltpu.CompilerParams(
            dimension_semantics=("arbitrary", "arbitrary"), vmem_limit_bytes=VMEM_LIMIT),
        name="proj_rope_conv",
    )(x, meta_pad, attn_norm_g, w_rep, rq, ri, conv_w[0], conv_b, conv_norm_g, conv_norm_b)


def _attend(q, iq, misc, ika, ikb, k, v):
    B, S, _ = q.shape
    T = k.shape[1]
    nb = S // BLK
    f32 = jnp.float32
    blk = lambda b, g: (b, g, 0)
    whole = lambda width: pl.BlockSpec((1, T, width), lambda b, g: (b, 0, 0),
                                       pipeline_mode=pl.Buffered(1))
    return pl.pallas_call(
        functools.partial(_dsa_kernel, n_rows_total=T),
        grid=(B, nb),
        in_specs=[
            pl.BlockSpec((1, BLK, ATTN_W), blk),
            pl.BlockSpec((1, BLK, ATTN_W), blk),
            pl.BlockSpec((1, BLK, LANES), blk),
            whole(LANES), whole(LANES), whole(ATTN_W), whole(ATTN_W),
        ],
        out_specs=pl.BlockSpec((1, BLK, ATTN_W), blk),
        out_shape=jax.ShapeDtypeStruct((B, S, ATTN_W), jnp.bfloat16),
        scratch_shapes=[
            pltpu.VMEM((nb + 1, STRIPS, SUBLANES, BLK), f32),
            pltpu.VMEM((SUBLANES, BLK), f32),
            pltpu.VMEM((SUBLANES, BLK), f32),
            pltpu.VMEM((SUBLANES, BLK), f32),
            pltpu.VMEM((ATTN_HEADS, 1, LANES), f32),
            pltpu.VMEM((ATTN_HEADS, BLK, LANES), f32),
            pltpu.VMEM((ATTN_HEADS, BLK, LANES), f32),
            pltpu.VMEM((ATTN_HEADS, BLK, ATTN_HEAD_DIM), f32),
            pltpu.VMEM((ATTN_HEADS, BLK, 1), f32),
            pltpu.VMEM((ATTN_HEADS, BLK, 1), f32),
        ],
        compiler_params=pltpu.CompilerParams(
            dimension_semantics=("arbitrary", "arbitrary"), vmem_limit_bytes=VMEM_LIMIT),
        name="dsa_attention",
    )(q, iq, misc, ika, ikb, k, v)


def _out_mlp(x, attn, conv, w_out, mlp_norm_g, w_up, w_down, final_norm_g):
    B, S, D = x.shape
    bf16 = jnp.bfloat16
    wo = w_out[0].astype(bf16)
    rows = lambda b, i: (b, i, 0)
    return pl.pallas_call(
        _mlp_kernel,
        grid=(B, S // MLP_ROWS),
        in_specs=[
            pl.BlockSpec((1, MLP_ROWS, D), rows),
            pl.BlockSpec((1, MLP_ROWS, ATTN_W), rows),
            pl.BlockSpec((1, MLP_ROWS, CONV_CH), rows),
            _const((ATTN_W, D)),
            _const((CONV_CH, D)),
            _const((1, D)),
            _const((D, D_FF)),
            _const((D_FF, D)),
            _const((1, D)),
        ],
        out_specs=pl.BlockSpec((1, MLP_ROWS, D), rows),
        out_shape=jax.ShapeDtypeStruct((B, S, D), x.dtype),
        compiler_params=pltpu.CompilerParams(
            dimension_semantics=("arbitrary", "arbitrary"), vmem_limit_bytes=VMEM_LIMIT),
        name="out_mlp",
    )(x, attn, conv, wo[:ATTN_W], wo[ATTN_W:], mlp_norm_g, w_up[0].astype(bf16),
      w_down[0].astype(bf16), final_norm_g.reshape(1, D))


def kernel(x, meta_tokens, attn_norm_g, w_in, conv_w, conv_b, conv_norm_g, conv_norm_b,
           w_out, mlp_norm_g, w_up, w_down, final_norm_g):
    B, S, D = x.shape
    assert D == D_MODEL and S % MLP_ROWS == 0 and attn_norm_g.shape[0] == 1
    q, iq, misc, conv, k, v, ika, ikb = _project(
        x, meta_tokens, attn_norm_g, w_in, conv_w, conv_b, conv_norm_g, conv_norm_b)
    attn = _attend(q, iq, misc, ika, ikb, k, v)
    return _out_mlp(x, attn, conv, w_out, mlp_norm_g, w_up, w_down, final_norm_g)
```

```python
import functools

import jax
import jax.numpy as jnp
from jax import lax
from jax.experimental import pallas as pl
from jax.experimental.pallas import tpu as pltpu

D_MODEL = 1024
N_META = 16
ATTN_HEADS = 4
ATTN_HEAD_DIM = 128
ATTN_W = ATTN_HEADS * ATTN_HEAD_DIM
ROPE_THETA = 500000.0
ROPE_FRACTION = 4
IDX_HEADS = 8
IDX_HEAD_DIM = 64
INDEX_TOPK = 256
CONV_CH = 512
CONV_GROUPS = 4
CONV_WIDTH = 31
D_FF = 4 * D_MODEL
NORM_EPS = 1e-5
NEG = -1e30

LANES = 128
SUBLANES = 8
BLK = 256
STRIPS = BLK // SUBLANES
PAD_ROWS = BLK - N_META
CARRY = 32
W_IN_COLS = 6 * 512 + LANES
VALUE_STEPS = 8
N_ACC = 4
TILE_UNROLL = 4
BOUND_SLACK = 1.0 + 2.0 ** -6
L_MIN = 1e-30
MLP_ROWS = 512
FF_CHUNK = 1024
VMEM_LIMIT = 60 * 1024 * 1024

_NT = (((1,), (1,)), ((), ()))


def _rms(x, g):
    return x * lax.rsqrt(jnp.mean(x * x, axis=-1, keepdims=True) + NORM_EPS) * g


def _rope(x, c, s1, s2, half):
    return x * c + pltpu.roll(x, half, 1) * s1 + pltpu.roll(x, LANES - half, 1) * s2


def _proj_kernel(x_ref, meta_ref, g_ref, w_ref, rq_ref, ri_ref, cw_ref, cb_ref, lg_ref, lb_ref,
                 q_ref, iq_ref, misc_ref, conv_ref, k_ref, v_ref, ika_ref, ikb_ref, u_scr):
    i = pl.program_id(1)
    h = jnp.where(i == 0, meta_ref[...], x_ref[0])
    xn = _rms(h, g_ref[...]).astype(jnp.bfloat16)
    proj = jnp.dot(xn, w_ref[...], preferred_element_type=jnp.float32)

    cq, s1q, s2q = rq_ref[0], rq_ref[1], rq_ref[2]
    ci, s1i, s2i = ri_ref[0], ri_ref[1], ri_ref[2]
    q_half = ATTN_HEAD_DIM // ROPE_FRACTION // 2
    i_half = IDX_HEAD_DIM // ROPE_FRACTION // 2
    for hd in range(ATTN_HEADS):
        sl = slice(hd * LANES, (hd + 1) * LANES)
        qh = _rope(proj[:, sl], cq, s1q, s2q, q_half) * (ATTN_HEAD_DIM ** -0.5)
        q_ref[0, :, sl] = qh.astype(jnp.bfloat16)
        kh = _rope(proj[:, 512 + hd * LANES:512 + (hd + 1) * LANES], cq, s1q, s2q, q_half)
        k_ref[0, :, sl] = kh.astype(jnp.bfloat16)
        ih = _rope(proj[:, 1536 + hd * LANES:1536 + (hd + 1) * LANES], ci, s1i, s2i, i_half)
        iq_ref[0, :, sl] = (ih * (IDX_HEAD_DIM ** -0.5)).astype(jnp.bfloat16)
    v_ref[0] = proj[:, 1024:1536].astype(jnp.bfloat16)

    lane = lax.broadcasted_iota(jnp.int32, (BLK, LANES), 1)
    is_ik = lane < IDX_HEAD_DIM
    slab = proj[:, 3072:3072 + LANES]
    roped = _rope(slab, jnp.where(is_ik, ci, 1.0), jnp.where(is_ik, s1i, 0.0),
                  jnp.where(is_ik, s2i, 0.0), i_half)
    misc_ref[0] = jnp.where(is_ik, roped, slab * (IDX_HEADS ** -0.5))
    ika = jnp.where(is_ik, roped, 0.0)
    ika_ref[0] = ika.astype(jnp.bfloat16)
    ikb_ref[0] = pltpu.roll(ika, IDX_HEAD_DIM, 1).astype(jnp.bfloat16)

    @pl.when(i == 0)
    def _():
        u_scr[0:CARRY, :] = jnp.zeros((CARRY, CONV_CH), jnp.float32)

    ga = proj[:, 2048:2560]
    gg = proj[:, 2560:3072]
    u_scr[CARRY:CARRY + BLK, :] = ga * jax.nn.sigmoid(gg)
    y = jnp.zeros((BLK, CONV_CH), jnp.float32) + cb_ref[...]
    for j in range(CONV_WIDTH):
        off = CARRY - (CONV_WIDTH - 1) + j
        y = y + u_scr[off:off + BLK, :] * cw_ref[j:j + 1, :]
    u_scr[0:CARRY, :] = u_scr[BLK:BLK + CARRY, :]
    gsz = CONV_CH // CONV_GROUPS
    for gi in range(CONV_GROUPS):
        sl = slice(gi * gsz, (gi + 1) * gsz)
        yg = y[:, sl]
        mu = jnp.mean(yg, axis=-1, keepdims=True)
        d = yg - mu
        var = jnp.mean(d * d, axis=-1, keepdims=True)
        yn = d * lax.rsqrt(var + NORM_EPS) * lg_ref[:, sl] + lb_ref[:, sl]
        conv_ref[0, :, sl] = (yn * jax.nn.sigmoid(yn)).astype(jnp.bfloat16)


def _f2k(f):
    b = lax.bitcast_convert_type(f, jnp.int32)
    return b ^ ((b >> 31) & jnp.int32(0x7FFFFFFF))


def _k2f(k):
    return lax.bitcast_convert_type(k ^ ((k >> 31) & jnp.int32(0x7FFFFFFF)), jnp.float32)


def _any(mask):
    return jnp.max(jnp.where(mask, 1.0, 0.0)) > 0.5


def _tile_loop(lo, hi, fn):
    n_main = (hi - lo) // TILE_UNROLL
    base = lo + TILE_UNROLL * n_main
    left = hi - base

    def main(i, c):
        for u in range(TILE_UNROLL):
            fn(lo + TILE_UNROLL * i + u)
        return c

    lax.fori_loop(0, n_main, main, 0)

    @pl.when(left >= 2)
    def _():
        fn(base)
        fn(base + 1)

    @pl.when(left % 2 == 1)
    def _():
        fn(hi - 1)


def _rows8(x):
    t = jnp.sum(jnp.sum(x, axis=0), axis=0, keepdims=True)
    return jnp.broadcast_to(t, (SUBLANES, BLK))


def _dsa_kernel(q_ref, iq_ref, misc_ref, ika_ref, ikb_ref, k_ref, v_ref, o_ref,
                sc_ref, tb_ref, tau_ref, mx_ref, mn_ref, kn_ref, mb_ref, ls_ref, acc_ref,
                m_ref, l_ref, *, n_rows_total):
    f32, bf16 = jnp.float32, jnp.bfloat16
    g = pl.program_id(1)
    nt = g + 2
    q0 = (g + 1) * BLK
    ones = jnp.ones((LANES, LANES), bf16)
    shape3 = (STRIPS, SUBLANES, BLK)
    keyi = (lax.broadcasted_iota(jnp.int32, shape3, 0) * SUBLANES
            + lax.broadcasted_iota(jnp.int32, shape3, 1))
    qryi = lax.broadcasted_iota(jnp.int32, shape3, 2)

    misc_t = misc_ref[0].T
    w8 = [jnp.broadcast_to(misc_t[IDX_HEAD_DIM + hd:IDX_HEAD_DIM + hd + 1, :], (SUBLANES, BLK))
          for hd in range(IDX_HEADS)]

    def score_tile(j):
        r = pl.multiple_of(j * BLK, BLK)
        ka = ika_ref[0, pl.ds(r, BLK), :]
        kb = ikb_ref[0, pl.ds(r, BLK), :]
        acc = jnp.zeros(shape3, f32)
        for p in range(IDX_HEADS // 2):
            slab = iq_ref[0, :, p * LANES:(p + 1) * LANES]
            le = lax.dot_general(ka, slab, _NT, preferred_element_type=f32).reshape(shape3)
            lo = lax.dot_general(kb, slab, _NT, preferred_element_type=f32).reshape(shape3)
            acc = (acc + jnp.maximum(le, 0.0) * w8[2 * p][None]
                   + jnp.maximum(lo, 0.0) * w8[2 * p + 1][None])
        return acc

    def put_tile(j, s, valid):
        lo = s if valid is None else jnp.where(valid, s, -jnp.inf)
        hi = s if valid is None else jnp.where(valid, s, jnp.inf)
        sc_ref[j] = lo
        mx_ref[...] = jnp.maximum(mx_ref[...], jnp.max(lo, axis=0))
        mn_ref[...] = jnp.minimum(mn_ref[...], jnp.min(hi, axis=0))

    mx_ref[...] = jnp.full((SUBLANES, BLK), -jnp.inf, f32)
    mn_ref[...] = jnp.full((SUBLANES, BLK), jnp.inf, f32)
    put_tile(0, score_tile(0), keyi >= PAD_ROWS)
    put_tile(nt - 1, score_tile(nt - 1), keyi <= qryi)

    _tile_loop(1, nt - 1, lambda j: put_tile(j, score_tile(j), None))

    kf = float(INDEX_TOPK)
    qrow = q0 + lax.broadcasted_iota(jnp.int32, (SUBLANES, BLK), 1)
    n_valid = (qrow - (PAD_ROWS - 1)).astype(f32)
    n_virtual = (n_rows_total - 1 - qrow).astype(f32)
    rowmax = jnp.broadcast_to(jnp.max(mx_ref[...], axis=0, keepdims=True), (SUBLANES, BLK))
    rowmin = jnp.broadcast_to(jnp.min(mn_ref[...], axis=0, keepdims=True), (SUBLANES, BLK))
    zacc = jnp.zeros((N_ACC, SUBLANES, BLK), f32)

    def partial(hit):
        return jnp.sum(jnp.where(hit, 1.0, 0.0).reshape(STRIPS // N_ACC, N_ACC, SUBLANES, BLK), axis=0)

    def count_ge(cand):
        def body(j, acc):
            return acc + partial(sc_ref[j] >= cand[None])

        return _rows8(lax.fori_loop(0, nt, body, zacc))

    def total_of(c_proc, cand):
        return c_proc + jnp.where(cand <= NEG, n_virtual, 0.0)

    def active_of(klo, khi, clo):
        return (total_of(clo, _k2f(klo)) != kf) & (khi > klo + 1)

    kneg = _f2k(jnp.full((SUBLANES, BLK), NEG, f32))
    wide = n_valid > kf
    klo = jnp.where(wide, _f2k(jnp.maximum(rowmin, NEG)), kneg)
    khi = jnp.where(wide, _f2k(rowmax) + 1, klo + 1)
    clo = n_valid
    chi = jnp.zeros((SUBLANES, BLK), f32)

    def zero_body(j, accs):
        x = sc_ref[j]
        return accs[0] + partial(x >= 0.0), accs[1] + partial(x > 0.0)

    age, agt = lax.fori_loop(0, nt, zero_body, (zacc, zacc))
    c_ge0, c_gt0 = _rows8(age), _rows8(agt)
    kzero = _f2k(jnp.zeros((SUBLANES, BLK), f32))
    inside = (klo < kzero) & (kzero < khi)
    up = inside & (c_ge0 >= kf)
    dn = inside & (c_ge0 < kf)
    pin = up & (c_gt0 < kf)
    klo = jnp.where(up, kzero, klo)
    clo = jnp.where(up, c_ge0, clo)
    khi = jnp.where(dn, kzero, jnp.where(pin, kzero + 1, khi))
    chi = jnp.where(dn, c_ge0, jnp.where(pin, c_gt0, chi))

    def search_cond(st):
        return st[0] > 0.5

    def search_step(st):
        _, it, klo, khi, clo, chi = st
        lo_f, hi_f = _k2f(klo), _k2f(khi)
        k_val = _f2k(lo_f + (hi_f - lo_f) * 0.5)
        k_mid = (klo >> 1) + (khi >> 1) + (klo & khi & 1)
        kt = jnp.where(it < VALUE_STEPS, k_val, k_mid)
        kt = jnp.minimum(jnp.maximum(kt, klo + 1), khi - 1)
        cand = _k2f(kt)
        c_proc = count_ge(cand)
        ge = total_of(c_proc, cand) >= kf
        act = active_of(klo, khi, clo)
        up = act & ge
        dn = act & jnp.logical_not(ge)
        klo = jnp.where(up, kt, klo)
        clo = jnp.where(up, c_proc, clo)
        khi = jnp.where(dn, kt, khi)
        chi = jnp.where(dn, c_proc, chi)
        left = jnp.max(jnp.where(active_of(klo, khi, clo), 1.0, 0.0))
        return left, it + 1, klo, khi, clo, chi

    left0 = jnp.max(jnp.where(active_of(klo, khi, clo), 1.0, 0.0))
    _, _, klo, khi, clo, chi = lax.while_loop(
        search_cond, search_step, (left0, jnp.int32(0), klo, khi, clo, chi))
    tau = _k2f(klo)
    tau_ref[...] = tau

    tied = clo > kf

    @pl.when(_any(tied))
    def _():
        need = kf - chi

        def scan(j, st):
            before, cross = st
            tb_ref[j] = before
            after = before + _rows8(partial(sc_ref[j] == tau[None]))
            here = (before < need) & (need <= after)
            return after, jnp.where(here, j.astype(f32), cross)

        zero8 = jnp.zeros((SUBLANES, BLK), f32)
        _, cross = lax.fori_loop(0, nt, scan, (zero8, zero8))
        j_lo = jnp.min(jnp.where(tied, cross, float(n_rows_total // BLK))).astype(jnp.int32)
        j_hi = jnp.max(jnp.where(tied, cross, 0.0)).astype(jnp.int32)
        need1 = need[0:1, :]
        tied1 = tied[0:1, :]
        tau1 = tau[0:1, :]
        tri = (lax.broadcasted_iota(jnp.int32, (BLK, BLK), 0)
               >= lax.broadcasted_iota(jnp.int32, (BLK, BLK), 1)).astype(bf16)

        def fix(j, c):
            x = sc_ref[j].reshape(BLK, BLK)
            eq = x == tau1
            rank = jnp.dot(tri, jnp.where(eq, 1.0, 0.0).astype(bf16),
                           preferred_element_type=f32) + tb_ref[j][0:1, :]
            cut = eq & (rank > need1) & tied1
            sc_ref[j] = jnp.where(cut, -jnp.inf, x).reshape(shape3)
            return c

        lax.fori_loop(j_lo, j_hi + 1, fix, 0)

        def drop(j, c):
            x = sc_ref[j]
            sc_ref[j] = jnp.where((x == tau[None]) & tied[None], -jnp.inf, x)
            return c

        lax.fori_loop(j_hi + 1, nt, drop, 0)

    @pl.when(g == 0)
    def _():
        kn_ref[...] = jnp.zeros(kn_ref.shape, f32)

        def norms(j, c):
            r = pl.multiple_of(j * BLK, BLK)
            for hd in range(ATTN_HEADS):
                kh = k_ref[0, pl.ds(r, BLK), hd * LANES:(hd + 1) * LANES].astype(f32)
                n2 = jnp.dot((kh * kh).astype(bf16), ones, preferred_element_type=f32)
                kn_ref[hd] = jnp.maximum(kn_ref[hd], jnp.max(n2, axis=0, keepdims=True))
            return c

        lax.fori_loop(0, n_rows_total // BLK, norms, 0)

    for hd in range(ATTN_HEADS):
        qh = q_ref[0, :, hd * LANES:(hd + 1) * LANES].astype(f32)
        qn2 = jnp.dot((qh * qh).astype(bf16), ones, preferred_element_type=f32)
        mb_ref[hd] = jnp.sqrt(qn2 * kn_ref[hd]) * BOUND_SLACK
    ls_ref[...] = jnp.zeros(ls_ref.shape, f32)
    acc_ref[...] = jnp.zeros(acc_ref.shape, f32)

    def bias_tile(j):
        sel = sc_ref[j] >= tau_ref[...][None]
        return jnp.where(sel, 0.0, NEG).reshape(BLK, BLK).T

    def attend_tile(j):
        r = pl.multiple_of(j * BLK, BLK)
        bias = bias_tile(j)
        b0 = bias[:, :LANES]
        b1 = bias[:, LANES:]
        for hd in range(ATTN_HEADS):
            sl = slice(hd * LANES, (hd + 1) * LANES)
            kh = k_ref[0, pl.ds(r, BLK), sl]
            vh = v_ref[0, pl.ds(r, BLK), sl]
            s = lax.dot_general(q_ref[0, :, sl], kh, _NT, preferred_element_type=f32)
            mb = mb_ref[hd]
            p0 = jnp.exp(s[:, :LANES] - mb + b0)
            p1 = jnp.exp(s[:, LANES:] - mb + b1)
            ls_ref[hd] += p0 + p1
            p = jnp.concatenate([p0, p1], axis=1).astype(bf16)
            acc_ref[hd] += jnp.dot(p, vh, preferred_element_type=f32)

    _tile_loop(0, nt, attend_tile)
    bad = jnp.float32(0.0)
    for hd in range(ATTN_HEADS):
        l = jnp.sum(ls_ref[hd], axis=1, keepdims=True)
        bad = jnp.maximum(bad, jnp.max(jnp.where(l >= L_MIN, 0.0, 1.0)))
        o_ref[0, :, hd * LANES:(hd + 1) * LANES] = (acc_ref[hd] / l).astype(bf16)

    @pl.when(bad > 0.5)
    def _():
        m_ref[...] = jnp.full(m_ref.shape, NEG, f32)
        l_ref[...] = jnp.zeros(l_ref.shape, f32)
        acc_ref[...] = jnp.zeros(acc_ref.shape, f32)

        def attend_online(j, c):
            r = pl.multiple_of(j * BLK, BLK)
            bias = bias_tile(j)
            for hd in range(ATTN_HEADS):
                sl = slice(hd * LANES, (hd + 1) * LANES)
                kh = k_ref[0, pl.ds(r, BLK), sl]
                vh = v_ref[0, pl.ds(r, BLK), sl]
                s = lax.dot_general(q_ref[0, :, sl], kh, _NT, preferred_element_type=f32) + bias
                m_old = m_ref[hd]
                m_new = jnp.maximum(m_old, jnp.max(s, axis=1, keepdims=True))
                p = jnp.exp(s - m_new)
                alpha = jnp.exp(m_old - m_new)
                l_ref[hd] = alpha * l_ref[hd] + jnp.sum(p, axis=1, keepdims=True)
                acc_ref[hd] = alpha * acc_ref[hd] + jnp.dot(p.astype(bf16), vh,
                                                            preferred_element_type=f32)
                m_ref[hd] = m_new
            return c

        lax.fori_loop(0, nt, attend_online, 0)
        for hd in range(ATTN_HEADS):
            o_ref[0, :, hd * LANES:(hd + 1) * LANES] = (acc_ref[hd] / l_ref[hd]).astype(bf16)


def _mlp_kernel(x_ref, a_ref, c_ref, woa_ref, woc_ref, g2_ref, wu_ref, wd_ref, gf_ref, o_ref):
    h = (x_ref[0]
         + jnp.dot(a_ref[0], woa_ref[...], preferred_element_type=jnp.float32)
         + jnp.dot(c_ref[0], woc_ref[...], preferred_element_type=jnp.float32))
    hn = _rms(h, g2_ref[...]).astype(jnp.bfloat16)
    acc = jnp.zeros_like(h)
    for c in range(D_FF // FF_CHUNK):
        sl = slice(c * FF_CHUNK, (c + 1) * FF_CHUNK)
        u = jnp.dot(hn, wu_ref[:, sl], preferred_element_type=jnp.float32)
        u = jnp.square(jnp.maximum(u, 0.0)).astype(jnp.bfloat16)
        acc = acc + jnp.dot(u, wd_ref[sl, :], preferred_element_type=jnp.float32)
    o_ref[0] = _rms(h + acc, gf_ref[...])


def _rope_tables(n_rows, head_dim, lanes_per_head):
    rot = head_dim // ROPE_FRACTION
    half = rot // 2
    inv = ROPE_THETA ** (-jnp.arange(0, rot, 2, dtype=jnp.float32) / rot)
    pos = jnp.arange(n_rows, dtype=jnp.float32) - float(PAD_ROWS)
    ang = pos[:, None] * inv[None, :]
    cos, sin = jnp.cos(ang), jnp.sin(ang)
    rest = lanes_per_head - rot
    one = jnp.ones((n_rows, rest), jnp.float32)
    zero = jnp.zeros((n_rows, rest), jnp.float32)
    zh = jnp.zeros((n_rows, half), jnp.float32)
    c = jnp.concatenate([cos, cos, one], axis=1)
    s1 = jnp.concatenate([zh, sin, zero], axis=1)
    s2 = jnp.concatenate([-sin, zh, zero], axis=1)
    reps = LANES // lanes_per_head
    return jnp.stack([jnp.tile(t, (1, reps)) for t in (c, s1, s2)])


def _const(shape):
    return pl.BlockSpec(shape, lambda b, i: (0,) * len(shape), pipeline_mode=pl.Buffered(1))


def _project(x, meta_tokens, attn_norm_g, w_in, conv_w, conv_b, conv_norm_g, conv_norm_b):
    B, S, D = x.shape
    nb = S // BLK
    T = S + BLK
    f32, bf16 = jnp.float32, jnp.bfloat16

    w = w_in[0]
    w_rep = jnp.concatenate(
        [w[:, 0:2048], w[:, 2120:3144], w[:, 2048:2120],
         jnp.zeros((D, W_IN_COLS - 3144), f32)], axis=1).astype(bf16)
    meta_pad = jnp.concatenate([jnp.zeros((PAD_ROWS, D), f32), meta_tokens.astype(f32)], axis=0)
    rq = _rope_tables(T, ATTN_HEAD_DIM, LANES)
    ri = _rope_tables(T, IDX_HEAD_DIM, IDX_HEAD_DIM)

    real = lambda b, i: (b, jnp.maximum(i - 1, 0), 0)
    padded = lambda b, i: (b, i, 0)
    return pl.pallas_call(
        _proj_kernel,
        grid=(B, nb + 1),
        in_specs=[
            pl.BlockSpec((1, BLK, D), real),
            _const((BLK, D)),
            _const((1, D)),
            _const((D, W_IN_COLS)),
            pl.BlockSpec((3, BLK, LANES), lambda b, i: (0, i, 0)),
            pl.BlockSpec((3, BLK, LANES), lambda b, i: (0, i, 0)),
            _const((CONV_WIDTH, CONV_CH)),
            _const((1, CONV_CH)),
            _const((1, CONV_CH)),
            _const((1, CONV_CH)),
        ],
        out_specs=[
            pl.BlockSpec((1, BLK, ATTN_W), real),
            pl.BlockSpec((1, BLK, ATTN_W), real),
            pl.BlockSpec((1, BLK, LANES), real),
            pl.BlockSpec((1, BLK, CONV_CH), real),
            pl.BlockSpec((1, BLK, ATTN_W), padded),
            pl.BlockSpec((1, BLK, ATTN_W), padded),
            pl.BlockSpec((1, BLK, LANES), padded),
            pl.BlockSpec((1, BLK, LANES), padded),
        ],
        out_shape=[
            jax.ShapeDtypeStruct((B, S, ATTN_W), bf16),
            jax.ShapeDtypeStruct((B, S, ATTN_W), bf16),
            jax.ShapeDtypeStruct((B, S, LANES), f32),
            jax.ShapeDtypeStruct((B, S, CONV_CH), bf16),
            jax.ShapeDtypeStruct((B, T, ATTN_W), bf16),
            jax.ShapeDtypeStruct((B, T, ATTN_W), bf16),
            jax.ShapeDtypeStruct((B, T, LANES), bf16),
            jax.ShapeDtypeStruct((B, T, LANES), bf16),
        ],
        scratch_shapes=[pltpu.VMEM((CARRY + BLK, CONV_CH), f32)],
        compiler_params=pltpu.CompilerParams(
            dimension_semantics=("arbitrary", "arbitrary"), vmem_limit_bytes=VMEM_LIMIT),
        name="proj_rope_conv",
    )(x, meta_pad, attn_norm_g, w_rep, rq, ri, conv_w[0], conv_b, conv_norm_g, conv_norm_b)


def _attend(q, iq, misc, ika, ikb, k, v):
    B, S, _ = q.shape
    T = k.shape[1]
    nb = S // BLK
    f32 = jnp.float32
    blk = lambda b, g: (b, g, 0)
    whole = lambda width: pl.BlockSpec((1, T, width), lambda b, g: (b, 0, 0),
                                       pipeline_mode=pl.Buffered(1))
    return pl.pallas_call(
        functools.partial(_dsa_kernel, n_rows_total=T),
        grid=(B, nb),
        in_specs=[
            pl.BlockSpec((1, BLK, ATTN_W), blk),
            pl.BlockSpec((1, BLK, ATTN_W), blk),
            pl.BlockSpec((1, BLK, LANES), blk),
            whole(LANES), whole(LANES), whole(ATTN_W), whole(ATTN_W),
        ],
        out_specs=pl.BlockSpec((1, BLK, ATTN_W), blk),
        out_shape=jax.ShapeDtypeStruct((B, S, ATTN_W), jnp.bfloat16),
        scratch_shapes=[
            pltpu.VMEM((nb + 1, STRIPS, SUBLANES, BLK), f32),
            pltpu.VMEM((nb + 1, SUBLANES, BLK), f32),
            pltpu.VMEM((SUBLANES, BLK), f32),
            pltpu.VMEM((SUBLANES, BLK), f32),
            pltpu.VMEM((SUBLANES, BLK), f32),
            pltpu.VMEM((ATTN_HEADS, 1, LANES), f32),
            pltpu.VMEM((ATTN_HEADS, BLK, LANES), f32),
            pltpu.VMEM((ATTN_HEADS, BLK, LANES), f32),
            pltpu.VMEM((ATTN_HEADS, BLK, ATTN_HEAD_DIM), f32),
            pltpu.VMEM((ATTN_HEADS, BLK, 1), f32),
            pltpu.VMEM((ATTN_HEADS, BLK, 1), f32),
        ],
        compiler_params=pltpu.CompilerParams(
            dimension_semantics=("arbitrary", "arbitrary"), vmem_limit_bytes=VMEM_LIMIT),
        name="dsa_attention",
    )(q, iq, misc, ika, ikb, k, v)


def _out_mlp(x, attn, conv, w_out, mlp_norm_g, w_up, w_down, final_norm_g):
    B, S, D = x.shape
    bf16 = jnp.bfloat16
    wo = w_out[0].astype(bf16)
    rows = lambda b, i: (b, i, 0)
    return pl.pallas_call(
        _mlp_kernel,
        grid=(B, S // MLP_ROWS),
        in_specs=[
            pl.BlockSpec((1, MLP_ROWS, D), rows),
            pl.BlockSpec((1, MLP_ROWS, ATTN_W), rows),
            pl.BlockSpec((1, MLP_ROWS, CONV_CH), rows),
            _const((ATTN_W, D)),
            _const((CONV_CH, D)),
            _const((1, D)),
            _const((D, D_FF)),
            _const((D_FF, D)),
            _const((1, D)),
        ],
        out_specs=pl.BlockSpec((1, MLP_ROWS, D), rows),
        out_shape=jax.ShapeDtypeStruct((B, S, D), x.dtype),
        compiler_params=pltpu.CompilerParams(
            dimension_semantics=("arbitrary", "arbitrary"), vmem_limit_bytes=VMEM_LIMIT),
        name="out_mlp",
    )(x, attn, conv, wo[:ATTN_W], wo[ATTN_W:], mlp_norm_g, w_up[0].astype(bf16),
      w_down[0].astype(bf16), final_norm_g.reshape(1, D))


def kernel(x, meta_tokens, attn_norm_g, w_in, conv_w, conv_b, conv_norm_g, conv_norm_b,
           w_out, mlp_norm_g, w_up, w_down, final_norm_g):
    B, S, D = x.shape
    assert D == D_MODEL and S % MLP_ROWS == 0 and attn_norm_g.shape[0] == 1
    q, iq, misc, conv, k, v, ika, ikb = _project(
        x, meta_tokens, attn_norm_g, w_in, conv_w, conv_b, conv_norm_g, conv_norm_b)
    attn = _attend(q, iq, misc, ika, ikb, k, v)
    return _out_mlp(x, attn, conv, w_out, mlp_norm_g, w_up, w_down, final_norm_g)
```

```python
import functools

import jax
import jax.numpy as jnp
from jax import lax
from jax.experimental import pallas as pl
from jax.experimental.pallas import tpu as pltpu

D_MODEL = 1024
N_META = 16
ATTN_HEADS = 4
ATTN_HEAD_DIM = 128
ATTN_W = ATTN_HEADS * ATTN_HEAD_DIM
ROPE_THETA = 500000.0
ROPE_FRACTION = 4
IDX_HEADS = 8
IDX_HEAD_DIM = 64
INDEX_TOPK = 256
CONV_CH = 512
CONV_GROUPS = 4
CONV_WIDTH = 31
D_FF = 4 * D_MODEL
NORM_EPS = 1e-5
NEG = -1e30

LANES = 128
SUBLANES = 8
BLK = 256
STRIPS = BLK // SUBLANES
PAD_ROWS = BLK - N_META
CARRY = 32
W_IN_COLS = 6 * 512 + LANES
VALUE_STEPS = 8
N_ACC = 4
TILE_UNROLL = 4
BOUND_SLACK = 1.0 + 2.0 ** -6
L_MIN = 1e-30
MLP_ROWS = 512
FF_CHUNK = 1024
VMEM_LIMIT = 60 * 1024 * 1024

_NT = (((1,), (1,)), ((), ()))


def _rms(x, g):
    return x * lax.rsqrt(jnp.mean(x * x, axis=-1, keepdims=True) + NORM_EPS) * g


def _rope(x, c, s1, s2, half):
    return x * c + pltpu.roll(x, half, 1) * s1 + pltpu.roll(x, LANES - half, 1) * s2


def _proj_kernel(x_ref, meta_ref, g_ref, w_ref, rq_ref, ri_ref, cw_ref, cb_ref, lg_ref, lb_ref,
                 q_ref, iq_ref, misc_ref, conv_ref, k_ref, v_ref, ika_ref, ikb_ref, u_scr, win_scr):
    i = pl.program_id(1)
    h = jnp.where(i == 0, meta_ref[...], x_ref[0])
    xn = _rms(h, g_ref[...]).astype(jnp.bfloat16)
    proj = jnp.dot(xn, w_ref[...], preferred_element_type=jnp.float32)

    cq, s1q, s2q = rq_ref[0], rq_ref[1], rq_ref[2]
    ci, s1i, s2i = ri_ref[0], ri_ref[1], ri_ref[2]
    q_half = ATTN_HEAD_DIM // ROPE_FRACTION // 2
    i_half = IDX_HEAD_DIM // ROPE_FRACTION // 2
    for hd in range(ATTN_HEADS):
        sl = slice(hd * LANES, (hd + 1) * LANES)
        qh = _rope(proj[:, sl], cq, s1q, s2q, q_half) * (ATTN_HEAD_DIM ** -0.5)
        q_ref[0, :, sl] = qh.astype(jnp.bfloat16)
        kh = _rope(proj[:, 512 + hd * LANES:512 + (hd + 1) * LANES], cq, s1q, s2q, q_half)
        k_ref[0, :, sl] = kh.astype(jnp.bfloat16)
        ih = _rope(proj[:, 1536 + hd * LANES:1536 + (hd + 1) * LANES], ci, s1i, s2i, i_half)
        iq_ref[0, :, sl] = (ih * (IDX_HEAD_DIM ** -0.5)).astype(jnp.bfloat16)
    v_ref[0] = proj[:, 1024:1536].astype(jnp.bfloat16)

    lane = lax.broadcasted_iota(jnp.int32, (BLK, LANES), 1)
    is_ik = lane < IDX_HEAD_DIM
    slab = proj[:, 3072:3072 + LANES]
    roped = _rope(slab, jnp.where(is_ik, ci, 1.0), jnp.where(is_ik, s1i, 0.0),
                  jnp.where(is_ik, s2i, 0.0), i_half)
    misc_ref[0] = jnp.where(is_ik, roped, slab * (IDX_HEADS ** -0.5))
    ika = jnp.where(is_ik, roped, 0.0)
    ika_ref[0] = ika.astype(jnp.bfloat16)
    ikb_ref[0] = pltpu.roll(ika, IDX_HEAD_DIM, 1).astype(jnp.bfloat16)

    @pl.when(i == 0)
    def _():
        u_scr[0:CARRY, :] = jnp.zeros((CARRY, CONV_CH), jnp.float32)

    ga = proj[:, 2048:2560]
    gg = proj[:, 2560:3072]
    u_scr[CARRY:CARRY + BLK, :] = ga * jax.nn.sigmoid(gg)
    y = jnp.zeros((BLK, CONV_CH), jnp.float32) + cb_ref[...]
    for b in range(SUBLANES):
        base = CARRY - (CONV_WIDTH - 1) + b
        n_a = (CONV_WIDTH - 1 - b) // SUBLANES + 1
        rows = BLK + SUBLANES * (n_a - 1)
        win_scr[0:rows, :] = u_scr[base:base + rows, :]
        for a in range(n_a):
            j = SUBLANES * a + b
            y = y + win_scr[SUBLANES * a:SUBLANES * a + BLK, :] * cw_ref[j:j + 1, :]
    u_scr[0:CARRY, :] = u_scr[BLK:BLK + CARRY, :]
    gsz = CONV_CH // CONV_GROUPS
    for gi in range(CONV_GROUPS):
        sl = slice(gi * gsz, (gi + 1) * gsz)
        yg = y[:, sl]
        mu = jnp.mean(yg, axis=-1, keepdims=True)
        d = yg - mu
        var = jnp.mean(d * d, axis=-1, keepdims=True)
        yn = d * lax.rsqrt(var + NORM_EPS) * lg_ref[:, sl] + lb_ref[:, sl]
        conv_ref[0, :, sl] = (yn * jax.nn.sigmoid(yn)).astype(jnp.bfloat16)


def _f2k(f):
    b = lax.bitcast_convert_type(f, jnp.int32)
    return b ^ ((b >> 31) & jnp.int32(0x7FFFFFFF))


def _k2f(k):
    return lax.bitcast_convert_type(k ^ ((k >> 31) & jnp.int32(0x7FFFFFFF)), jnp.float32)


def _any(mask):
    return jnp.max(jnp.where(mask, 1.0, 0.0)) > 0.5


def _tile_loop(lo, hi, fn):
    n_main = (hi - lo) // TILE_UNROLL
    base = lo + TILE_UNROLL * n_main
    left = hi - base

    def main(i, c):
        for u in range(TILE_UNROLL):
            fn(lo + TILE_UNROLL * i + u)
        return c

    lax.fori_loop(0, n_main, main, 0)

    @pl.when(left >= 2)
    def _():
        fn(base)
        fn(base + 1)

    @pl.when(left % 2 == 1)
    def _():
        fn(hi - 1)


def _rows8(x):
    t = jnp.sum(jnp.sum(x, axis=0), axis=0, keepdims=True)
    return jnp.broadcast_to(t, (SUBLANES, BLK))


def _dsa_kernel(q_ref, iq_ref, misc_ref, ika_ref, ikb_ref, k_ref, v_ref, o_ref,
                sc_ref, tb_ref, tau_ref, mx_ref, mn_ref, kn_ref, mb_ref, ls_ref, acc_ref,
                m_ref, l_ref, *, n_rows_total):
    f32, bf16 = jnp.float32, jnp.bfloat16
    g = pl.program_id(1)
    nt = g + 2
    q0 = (g + 1) * BLK
    ones = jnp.ones((LANES, LANES), bf16)
    shape3 = (STRIPS, SUBLANES, BLK)
    keyi = (lax.broadcasted_iota(jnp.int32, shape3, 0) * SUBLANES
            + lax.broadcasted_iota(jnp.int32, shape3, 1))
    qryi = lax.broadcasted_iota(jnp.int32, shape3, 2)

    misc_t = misc_ref[0].T
    w8 = [jnp.broadcast_to(misc_t[IDX_HEAD_DIM + hd:IDX_HEAD_DIM + hd + 1, :], (SUBLANES, BLK))
          for hd in range(IDX_HEADS)]

    def score_tile(j):
        r = pl.multiple_of(j * BLK, BLK)
        ka = ika_ref[0, pl.ds(r, BLK), :]
        kb = ikb_ref[0, pl.ds(r, BLK), :]
        acc = jnp.zeros(shape3, f32)
        for p in range(IDX_HEADS // 2):
            slab = iq_ref[0, :, p * LANES:(p + 1) * LANES]
            le = lax.dot_general(ka, slab, _NT, preferred_element_type=f32).reshape(shape3)
            lo = lax.dot_general(kb, slab, _NT, preferred_element_type=f32).reshape(shape3)
            acc = (acc + jnp.maximum(le, 0.0) * w8[2 * p][None]
                   + jnp.maximum(lo, 0.0) * w8[2 * p + 1][None])
        return acc

    def put_tile(j, s, valid):
        lo = s if valid is None else jnp.where(valid, s, -jnp.inf)
        hi = s if valid is None else jnp.where(valid, s, jnp.inf)
        sc_ref[j] = lo
        mx_ref[...] = jnp.maximum(mx_ref[...], jnp.max(lo, axis=0))
        mn_ref[...] = jnp.minimum(mn_ref[...], jnp.min(hi, axis=0))

    mx_ref[...] = jnp.full((SUBLANES, BLK), -jnp.inf, f32)
    mn_ref[...] = jnp.full((SUBLANES, BLK), jnp.inf, f32)
    put_tile(0, score_tile(0), keyi >= PAD_ROWS)
    put_tile(nt - 1, score_tile(nt - 1), keyi <= qryi)

    _tile_loop(1, nt - 1, lambda j: put_tile(j, score_tile(j), None))

    kf = float(INDEX_TOPK)
    qrow = q0 + lax.broadcasted_iota(jnp.int32, (SUBLANES, BLK), 1)
    n_valid = (qrow - (PAD_ROWS - 1)).astype(f32)
    n_virtual = (n_rows_total - 1 - qrow).astype(f32)
    rowmax = jnp.broadcast_to(jnp.max(mx_ref[...], axis=0, keepdims=True), (SUBLANES, BLK))
    rowmin = jnp.broadcast_to(jnp.min(mn_ref[...], axis=0, keepdims=True), (SUBLANES, BLK))
    zacc = jnp.zeros((N_ACC, SUBLANES, BLK), f32)

    def partial(hit):
        return jnp.sum(jnp.where(hit, 1.0, 0.0).reshape(STRIPS // N_ACC, N_ACC, SUBLANES, BLK), axis=0)

    def bump(acc, hit):
        hit = hit.reshape(STRIPS // N_ACC, N_ACC, SUBLANES, BLK)
        for s in range(STRIPS // N_ACC):
            acc = jnp.where(hit[s], acc + 1.0, acc)
        return acc

    def count_ge(cand):
        def pair(i, acc):
            acc = bump(acc, sc_ref[2 * i] >= cand[None])
            return bump(acc, sc_ref[2 * i + 1] >= cand[None])

        def single(j, acc):
            return bump(acc, sc_ref[j] >= cand[None])

        acc = lax.fori_loop(0, nt // 2, pair, zacc)
        return _rows8(lax.fori_loop(2 * (nt // 2), nt, single, acc))

    def total_of(c_proc, cand):
        return c_proc + jnp.where(cand <= NEG, n_virtual, 0.0)

    def active_of(klo, khi, clo):
        return (total_of(clo, _k2f(klo)) != kf) & (khi > klo + 1)

    kneg = _f2k(jnp.full((SUBLANES, BLK), NEG, f32))
    wide = n_valid > kf
    klo = jnp.where(wide, _f2k(jnp.maximum(rowmin, NEG)), kneg)
    khi = jnp.where(wide, _f2k(rowmax) + 1, klo + 1)
    clo = n_valid
    chi = jnp.zeros((SUBLANES, BLK), f32)

    def zero_body(j, accs):
        x = sc_ref[j]
        return accs[0] + partial(x >= 0.0), accs[1] + partial(x > 0.0)

    age, agt = lax.fori_loop(0, nt, zero_body, (zacc, zacc))
    c_ge0, c_gt0 = _rows8(age), _rows8(agt)
    kzero = _f2k(jnp.zeros((SUBLANES, BLK), f32))
    inside = (klo < kzero) & (kzero < khi)
    up = inside & (c_ge0 >= kf)
    dn = inside & (c_ge0 < kf)
    pin = up & (c_gt0 < kf)
    klo = jnp.where(up, kzero, klo)
    clo = jnp.where(up, c_ge0, clo)
    khi = jnp.where(dn, kzero, jnp.where(pin, kzero + 1, khi))
    chi = jnp.where(dn, c_ge0, jnp.where(pin, c_gt0, chi))

    def search_cond(st):
        return st[0] > 0.5

    def search_step(st):
        _, it, klo, khi, clo, chi = st
        lo_f, hi_f = _k2f(klo), _k2f(khi)
        k_val = _f2k(lo_f + (hi_f - lo_f) * 0.5)
        k_mid = (klo >> 1) + (khi >> 1) + (klo & khi & 1)
        kt = jnp.where(it < VALUE_STEPS, k_val, k_mid)
        kt = jnp.minimum(jnp.maximum(kt, klo + 1), khi - 1)
        cand = _k2f(kt)
        c_proc = count_ge(cand)
        ge = total_of(c_proc, cand) >= kf
        act = active_of(klo, khi, clo)
        up = act & ge
        dn = act & jnp.logical_not(ge)
        klo = jnp.where(up, kt, klo)
        clo = jnp.where(up, c_proc, clo)
        khi = jnp.where(dn, kt, khi)
        chi = jnp.where(dn, c_proc, chi)
        left = jnp.max(jnp.where(active_of(klo, khi, clo), 1.0, 0.0))
        return left, it + 1, klo, khi, clo, chi

    left0 = jnp.max(jnp.where(active_of(klo, khi, clo), 1.0, 0.0))
    _, _, klo, khi, clo, chi = lax.while_loop(
        search_cond, search_step, (left0, jnp.int32(0), klo, khi, clo, chi))
    tau = _k2f(klo)
    tau_ref[...] = tau

    tied = clo > kf

    @pl.when(_any(tied))
    def _():
        need = kf - chi

        def scan(j, st):
            before, cross = st
            tb_ref[j] = before
            after = before + _rows8(partial(sc_ref[j] == tau[None]))
            here = (before < need) & (need <= after)
            return after, jnp.where(here, j.astype(f32), cross)

        zero8 = jnp.zeros((SUBLANES, BLK), f32)
        _, cross = lax.fori_loop(0, nt, scan, (zero8, zero8))
        j_lo = jnp.min(jnp.where(tied, cross, float(n_rows_total // BLK))).astype(jnp.int32)
        j_hi = jnp.max(jnp.where(tied, cross, 0.0)).astype(jnp.int32)
        need1 = need[0:1, :]
        tied1 = tied[0:1, :]
        tau1 = tau[0:1, :]
        tri = (lax.broadcasted_iota(jnp.int32, (BLK, BLK), 0)
               >= lax.broadcasted_iota(jnp.int32, (BLK, BLK), 1)).astype(bf16)

        def fix(j, c):
            x = sc_ref[j].reshape(BLK, BLK)
            eq = x == tau1
            rank = jnp.dot(tri, jnp.where(eq, 1.0, 0.0).astype(bf16),
                           preferred_element_type=f32) + tb_ref[j][0:1, :]
            cut = eq & (rank > need1) & tied1
            sc_ref[j] = jnp.where(cut, -jnp.inf, x).reshape(shape3)
            return c

        lax.fori_loop(j_lo, j_hi + 1, fix, 0)

        tau_tied = jnp.where(tied, tau, jnp.nan)[None]

        def drop(j, c):
            x = sc_ref[j]
            sc_ref[j] = jnp.where(x == tau_tied, -jnp.inf, x)
            return c

        lax.fori_loop(j_hi + 1, nt, drop, 0)

    @pl.when(g == 0)
    def _():
        kn_ref[...] = jnp.zeros(kn_ref.shape, f32)

        def norms(j, c):
            r = pl.multiple_of(j * BLK, BLK)
            for hd in range(ATTN_HEADS):
                kh = k_ref[0, pl.ds(r, BLK), hd * LANES:(hd + 1) * LANES].astype(f32)
                n2 = jnp.dot((kh * kh).astype(bf16), ones, preferred_element_type=f32)
                kn_ref[hd] = jnp.maximum(kn_ref[hd], jnp.max(n2, axis=0, keepdims=True))
            return c

        lax.fori_loop(0, n_rows_total // BLK, norms, 0)

    for hd in range(ATTN_HEADS):
        qh = q_ref[0, :, hd * LANES:(hd + 1) * LANES].astype(f32)
        qn2 = jnp.dot((qh * qh).astype(bf16), ones, preferred_element_type=f32)
        mb_ref[hd] = jnp.sqrt(qn2 * kn_ref[hd]) * BOUND_SLACK
    ls_ref[...] = jnp.zeros(ls_ref.shape, f32)
    acc_ref[...] = jnp.zeros(acc_ref.shape, f32)

    def bias_tile(j):
        sel = sc_ref[j] >= tau_ref[...][None]
        return jnp.where(sel, 0.0, NEG).reshape(BLK, BLK).T

    def attend_tile(j):
        r = pl.multiple_of(j * BLK, BLK)
        bias = bias_tile(j)
        b0 = bias[:, :LANES]
        b1 = bias[:, LANES:]
        for hd in range(ATTN_HEADS):
            sl = slice(hd * LANES, (hd + 1) * LANES)
            kh = k_ref[0, pl.ds(r, BLK), sl]
            vh = v_ref[0, pl.ds(r, BLK), sl]
            s = lax.dot_general(q_ref[0, :, sl], kh, _NT, preferred_element_type=f32)
            mb = mb_ref[hd]
            p0 = jnp.exp(s[:, :LANES] - mb + b0)
            p1 = jnp.exp(s[:, LANES:] - mb + b1)
            ls_ref[hd] += p0 + p1
            p = jnp.concatenate([p0, p1], axis=1).astype(bf16)
            acc_ref[hd] += jnp.dot(p, vh, preferred_element_type=f32)

    _tile_loop(0, nt, attend_tile)
    bad = jnp.float32(0.0)
    for hd in range(ATTN_HEADS):
        l = jnp.sum(ls_ref[hd], axis=1, keepdims=True)
        bad = jnp.maximum(bad, jnp.max(jnp.where(l >= L_MIN, 0.0, 1.0)))
        o_ref[0, :, hd * LANES:(hd + 1) * LANES] = (acc_ref[hd] / l).astype(bf16)

    @pl.when(bad > 0.5)
    def _():
        m_ref[...] = jnp.full(m_ref.shape, NEG, f32)
        l_ref[...] = jnp.zeros(l_ref.shape, f32)
        acc_ref[...] = jnp.zeros(acc_ref.shape, f32)

        def attend_online(j, c):
            r = pl.multiple_of(j * BLK, BLK)
            bias = bias_tile(j)
            for hd in range(ATTN_HEADS):
                sl = slice(hd * LANES, (hd + 1) * LANES)
                kh = k_ref[0, pl.ds(r, BLK), sl]
                vh = v_ref[0, pl.ds(r, BLK), sl]
                s = lax.dot_general(q_ref[0, :, sl], kh, _NT, preferred_element_type=f32) + bias
                m_old = m_ref[hd]
                m_new = jnp.maximum(m_old, jnp.max(s, axis=1, keepdims=True))
                p = jnp.exp(s - m_new)
                alpha = jnp.exp(m_old - m_new)
                l_ref[hd] = alpha * l_ref[hd] + jnp.sum(p, axis=1, keepdims=True)
                acc_ref[hd] = alpha * acc_ref[hd] + jnp.dot(p.astype(bf16), vh,
                                                            preferred_element_type=f32)
                m_ref[hd] = m_new
            return c

        lax.fori_loop(0, nt, attend_online, 0)
        for hd in range(ATTN_HEADS):
            o_ref[0, :, hd * LANES:(hd + 1) * LANES] = (acc_ref[hd] / l_ref[hd]).astype(bf16)


def _mlp_kernel(x_ref, a_ref, c_ref, woa_ref, woc_ref, g2_ref, wu_ref, wd_ref, gf_ref, o_ref):
    h = (x_ref[0]
         + jnp.dot(a_ref[0], woa_ref[...], preferred_element_type=jnp.float32)
         + jnp.dot(c_ref[0], woc_ref[...], preferred_element_type=jnp.float32))
    hn = _rms(h, g2_ref[...]).astype(jnp.bfloat16)
    acc = jnp.zeros_like(h)
    for c in range(D_FF // FF_CHUNK):
        sl = slice(c * FF_CHUNK, (c + 1) * FF_CHUNK)
        u = jnp.dot(hn, wu_ref[:, sl], preferred_element_type=jnp.float32)
        u = jnp.square(jnp.maximum(u, 0.0)).astype(jnp.bfloat16)
        acc = acc + jnp.dot(u, wd_ref[sl, :], preferred_element_type=jnp.float32)
    o_ref[0] = _rms(h + acc, gf_ref[...])


def _rope_tables(n_rows, head_dim, lanes_per_head):
    rot = head_dim // ROPE_FRACTION
    half = rot // 2
    inv = ROPE_THETA ** (-jnp.arange(0, rot, 2, dtype=jnp.float32) / rot)
    pos = jnp.arange(n_rows, dtype=jnp.float32) - float(PAD_ROWS)
    ang = pos[:, None] * inv[None, :]
    cos, sin = jnp.cos(ang), jnp.sin(ang)
    rest = lanes_per_head - rot
    one = jnp.ones((n_rows, rest), jnp.float32)
    zero = jnp.zeros((n_rows, rest), jnp.float32)
    zh = jnp.zeros((n_rows, half), jnp.float32)
    c = jnp.concatenate([cos, cos, one], axis=1)
    s1 = jnp.concatenate([zh, sin, zero], axis=1)
    s2 = jnp.concatenate([-sin, zh, zero], axis=1)
    reps = LANES // lanes_per_head
    return jnp.stack([jnp.tile(t, (1, reps)) for t in (c, s1, s2)])


def _const(shape):
    return pl.BlockSpec(shape, lambda b, i: (0,) * len(shape), pipeline_mode=pl.Buffered(1))


def _project(x, meta_tokens, attn_norm_g, w_in, conv_w, conv_b, conv_norm_g, conv_norm_b):
    B, S, D = x.shape
    nb = S // BLK
    T = S + BLK
    f32, bf16 = jnp.float32, jnp.bfloat16

    w = w_in[0]
    w_rep = jnp.concatenate(
        [w[:, 0:2048], w[:, 2120:3144], w[:, 2048:2120],
         jnp.zeros((D, W_IN_COLS - 3144), f32)], axis=1).astype(bf16)
    meta_pad = jnp.concatenate([jnp.zeros((PAD_ROWS, D), f32), meta_tokens.astype(f32)], axis=0)
    rq = _rope_tables(T, ATTN_HEAD_DIM, LANES)
    ri = _rope_tables(T, IDX_HEAD_DIM, IDX_HEAD_DIM)

    real = lambda b, i: (b, jnp.maximum(i - 1, 0), 0)
    padded = lambda b, i: (b, i, 0)
    return pl.pallas_call(
        _proj_kernel,
        grid=(B, nb + 1),
        in_specs=[
            pl.BlockSpec((1, BLK, D), real),
            _const((BLK, D)),
            _const((1, D)),
            _const((D, W_IN_COLS)),
            pl.BlockSpec((3, BLK, LANES), lambda b, i: (0, i, 0)),
            pl.BlockSpec((3, BLK, LANES), lambda b, i: (0, i, 0)),
            _const((CONV_WIDTH, CONV_CH)),
            _const((1, CONV_CH)),
            _const((1, CONV_CH)),
            _const((1, CONV_CH)),
        ],
        out_specs=[
            pl.BlockSpec((1, BLK, ATTN_W), real),
            pl.BlockSpec((1, BLK, ATTN_W), real),
            pl.BlockSpec((1, BLK, LANES), real),
            pl.BlockSpec((1, BLK, CONV_CH), real),
            pl.BlockSpec((1, BLK, ATTN_W), padded),
            pl.BlockSpec((1, BLK, ATTN_W), padded),
            pl.BlockSpec((1, BLK, LANES), padded),
            pl.BlockSpec((1, BLK, LANES), padded),
        ],
        out_shape=[
            jax.ShapeDtypeStruct((B, S, ATTN_W), bf16),
            jax.ShapeDtypeStruct((B, S, ATTN_W), bf16),
            jax.ShapeDtypeStruct((B, S, LANES), f32),
            jax.ShapeDtypeStruct((B, S, CONV_CH), bf16),
            jax.ShapeDtypeStruct((B, T, ATTN_W), bf16),
            jax.ShapeDtypeStruct((B, T, ATTN_W), bf16),
            jax.ShapeDtypeStruct((B, T, LANES), bf16),
            jax.ShapeDtypeStruct((B, T, LANES), bf16),
        ],
        scratch_shapes=[pltpu.VMEM((CARRY + BLK, CONV_CH), f32),
                        pltpu.VMEM((CARRY + BLK, CONV_CH), f32)],
        compiler_params=pltpu.CompilerParams(
            dimension_semantics=("arbitrary", "arbitrary"), vmem_limit_bytes=VMEM_LIMIT),
        name="proj_rope_conv",
    )(x, meta_pad, attn_norm_g, w_rep, rq, ri, conv_w[0], conv_b, conv_norm_g, conv_norm_b)


def _attend(q, iq, misc, ika, ikb, k, v):
    B, S, _ = q.shape
    T = k.shape[1]
    nb = S // BLK
    f32 = jnp.float32
    blk = lambda b, g: (b, g, 0)
    whole = lambda width: pl.BlockSpec((1, T, width), lambda b, g: (b, 0, 0),
                                       pipeline_mode=pl.Buffered(1))
    return pl.pallas_call(
        functools.partial(_dsa_kernel, n_rows_total=T),
        grid=(B, nb),
        in_specs=[
            pl.BlockSpec((1, BLK, ATTN_W), blk),
            pl.BlockSpec((1, BLK, ATTN_W), blk),
            pl.BlockSpec((1, BLK, LANES), blk),
            whole(LANES), whole(LANES), whole(ATTN_W), whole(ATTN_W),
        ],
        out_specs=pl.BlockSpec((1, BLK, ATTN_W), blk),
        out_shape=jax.ShapeDtypeStruct((B, S, ATTN_W), jnp.bfloat16),
        scratch_shapes=[
            pltpu.VMEM((nb + 1, STRIPS, SUBLANES, BLK), f32),
            pltpu.VMEM((nb + 1, SUBLANES, BLK), f32),
            pltpu.VMEM((SUBLANES, BLK), f32),
            pltpu.VMEM((SUBLANES, BLK), f32),
            pltpu.VMEM((SUBLANES, BLK), f32),
            pltpu.VMEM((ATTN_HEADS, 1, LANES), f32),
            pltpu.VMEM((ATTN_HEADS, BLK, LANES), f32),
            pltpu.VMEM((ATTN_HEADS, BLK, LANES), f32),
            pltpu.VMEM((ATTN_HEADS, BLK, ATTN_HEAD_DIM), f32),
            pltpu.VMEM((ATTN_HEADS, BLK, 1), f32),
            pltpu.VMEM((ATTN_HEADS, BLK, 1), f32),
        ],
        compiler_params=pltpu.CompilerParams(
            dimension_semantics=("arbitrary", "arbitrary"), vmem_limit_bytes=VMEM_LIMIT),
        name="dsa_attention",
    )(q, iq, misc, ika, ikb, k, v)


def _out_mlp(x, attn, conv, w_out, mlp_norm_g, w_up, w_down, final_norm_g):
    B, S, D = x.shape
    bf16 = jnp.bfloat16
    wo = w_out[0].astype(bf16)
    rows = lambda b, i: (b, i, 0)
    return pl.pallas_call(
        _mlp_kernel,
        grid=(B, S // MLP_ROWS),
        in_specs=[
            pl.BlockSpec((1, MLP_ROWS, D), rows),
            pl.BlockSpec((1, MLP_ROWS, ATTN_W), rows),
            pl.BlockSpec((1, MLP_ROWS, CONV_CH), rows),
            _const((ATTN_W, D)),
            _const((CONV_CH, D)),
            _const((1, D)),
            _const((D, D_FF)),
            _const((D_FF, D)),
            _const((1, D)),
        ],
        out_specs=pl.BlockSpec((1, MLP_ROWS, D), rows),
        out_shape=jax.ShapeDtypeStruct((B, S, D), x.dtype),
        compiler_params=pltpu.CompilerParams(
            dimension_semantics=("arbitrary", "arbitrary"), vmem_limit_bytes=VMEM_LIMIT),
        name="out_mlp",
    )(x, attn, conv, wo[:ATTN_W], wo[ATTN_W:], mlp_norm_g, w_up[0].astype(bf16),
      w_down[0].astype(bf16), final_norm_g.reshape(1, D))


def kernel(x, meta_tokens, attn_norm_g, w_in, conv_w, conv_b, conv_norm_g, conv_norm_b,
           w_out, mlp_norm_g, w_up, w_down, final_norm_g):
    B, S, D = x.shape
    assert D == D_MODEL and S % MLP_ROWS == 0 and attn_norm_g.shape[0] == 1
    q, iq, misc, conv, k, v, ika, ikb = _project(
        x, meta_tokens, attn_norm_g, w_in, conv_w, conv_b, conv_norm_g, conv_norm_b)
    attn = _attend(q, iq, misc, ika, ikb, k, v)
    return _out_mlp(x, attn, conv, w_out, mlp_norm_g, w_up, w_down, final_norm_g)
```

```python
import functools

import jax
import jax.numpy as jnp
from jax import lax
from jax.experimental import pallas as pl
from jax.experimental.pallas import tpu as pltpu

D_MODEL = 1024
N_META = 16
ATTN_HEADS = 4
ATTN_HEAD_DIM = 128
ATTN_W = ATTN_HEADS * ATTN_HEAD_DIM
ROPE_THETA = 500000.0
ROPE_FRACTION = 4
IDX_HEADS = 8
IDX_HEAD_DIM = 64
INDEX_TOPK = 256
CONV_CH = 512
CONV_GROUPS = 4
CONV_WIDTH = 31
D_FF = 4 * D_MODEL
NORM_EPS = 1e-5
NEG = -1e30

LANES = 128
SUBLANES = 8
BLK = 256
STRIPS = BLK // SUBLANES
PAD_ROWS = BLK - N_META
CARRY = 32
W_IN_COLS = 6 * 512 + LANES
VALUE_STEPS = 24
BLIND_STEPS = 12
N_ACC = 4
TILE_UNROLL = 8
BOUND_SLACK = 1.0 + 2.0 ** -6
L_MIN = 1e-30
MLP_ROWS = 512
FF_CHUNK = 1024
VMEM_LIMIT = 60 * 1024 * 1024

_NT = (((1,), (1,)), ((), ()))


def _rms(x, g):
    return x * lax.rsqrt(jnp.mean(x * x, axis=-1, keepdims=True) + NORM_EPS) * g


def _rope(x, c, s1, s2, half):
    return x * c + pltpu.roll(x, half, 1) * s1 + pltpu.roll(x, LANES - half, 1) * s2


def _proj_kernel(x_ref, meta_ref, g_ref, w_ref, rq_ref, ri_ref, cw_ref, cb_ref, lg_ref, lb_ref,
                 q_ref, iq_ref, misc_ref, conv_ref, k_ref, v_ref, ika_ref, ikb_ref, u_scr, win_scr):
    i = pl.program_id(1)
    h = jnp.where(i == 0, meta_ref[...], x_ref[0])
    xn = _rms(h, g_ref[...]).astype(jnp.bfloat16)
    proj = jnp.dot(xn, w_ref[...], preferred_element_type=jnp.float32)

    cq, s1q, s2q = rq_ref[0], rq_ref[1], rq_ref[2]
    ci, s1i, s2i = ri_ref[0], ri_ref[1], ri_ref[2]
    q_half = ATTN_HEAD_DIM // ROPE_FRACTION // 2
    i_half = IDX_HEAD_DIM // ROPE_FRACTION // 2
    for hd in range(ATTN_HEADS):
        sl = slice(hd * LANES, (hd + 1) * LANES)
        qh = _rope(proj[:, sl], cq, s1q, s2q, q_half) * (ATTN_HEAD_DIM ** -0.5)
        q_ref[0, :, sl] = qh.astype(jnp.bfloat16)
        kh = _rope(proj[:, 512 + hd * LANES:512 + (hd + 1) * LANES], cq, s1q, s2q, q_half)
        k_ref[0, :, sl] = kh.astype(jnp.bfloat16)
        ih = _rope(proj[:, 1536 + hd * LANES:1536 + (hd + 1) * LANES], ci, s1i, s2i, i_half)
        iq_ref[0, :, sl] = (ih * (IDX_HEAD_DIM ** -0.5)).astype(jnp.bfloat16)
    v_ref[0] = proj[:, 1024:1536].astype(jnp.bfloat16)

    lane = lax.broadcasted_iota(jnp.int32, (BLK, LANES), 1)
    is_ik = lane < IDX_HEAD_DIM
    slab = proj[:, 3072:3072 + LANES]
    roped = _rope(slab, jnp.where(is_ik, ci, 1.0), jnp.where(is_ik, s1i, 0.0),
                  jnp.where(is_ik, s2i, 0.0), i_half)
    misc_ref[0] = jnp.where(is_ik, roped, slab * (IDX_HEADS ** -0.5))
    ika = jnp.where(is_ik, roped, 0.0)
    ika_ref[0] = ika.astype(jnp.bfloat16)
    ikb_ref[0] = pltpu.roll(ika, IDX_HEAD_DIM, 1).astype(jnp.bfloat16)

    @pl.when(i == 0)
    def _():
        u_scr[0:CARRY, :] = jnp.zeros((CARRY, CONV_CH), jnp.float32)

    ga = proj[:, 2048:2560]
    gg = proj[:, 2560:3072]
    u_scr[CARRY:CARRY + BLK, :] = ga * jax.nn.sigmoid(gg)
    y = jnp.zeros((BLK, CONV_CH), jnp.float32) + cb_ref[...]
    for b in range(SUBLANES):
        base = CARRY - (CONV_WIDTH - 1) + b
        n_a = (CONV_WIDTH - 1 - b) // SUBLANES + 1
        rows = BLK + SUBLANES * (n_a - 1)
        win_scr[0:rows, :] = u_scr[base:base + rows, :]
        for a in range(n_a):
            j = SUBLANES * a + b
            y = y + win_scr[SUBLANES * a:SUBLANES * a + BLK, :] * cw_ref[j:j + 1, :]
    u_scr[0:CARRY, :] = u_scr[BLK:BLK + CARRY, :]
    gsz = CONV_CH // CONV_GROUPS
    for gi in range(CONV_GROUPS):
        sl = slice(gi * gsz, (gi + 1) * gsz)
        yg = y[:, sl]
        mu = jnp.mean(yg, axis=-1, keepdims=True)
        d = yg - mu
        var = jnp.mean(d * d, axis=-1, keepdims=True)
        yn = d * lax.rsqrt(var + NORM_EPS) * lg_ref[:, sl] + lb_ref[:, sl]
        conv_ref[0, :, sl] = (yn * jax.nn.sigmoid(yn)).astype(jnp.bfloat16)


def _f2k(f):
    b = lax.bitcast_convert_type(f, jnp.int32)
    return b ^ ((b >> 31) & jnp.int32(0x7FFFFFFF))


def _k2f(k):
    return lax.bitcast_convert_type(k ^ ((k >> 31) & jnp.int32(0x7FFFFFFF)), jnp.float32)


def _any(mask):
    return jnp.max(jnp.where(mask, 1.0, 0.0)) > 0.5


def _tile_loop(lo, hi, fn):
    n_main = (hi - lo) // TILE_UNROLL

    def main(i, c):
        for u in range(TILE_UNROLL):
            fn(lo + TILE_UNROLL * i + u)
        return c

    lax.fori_loop(0, n_main, main, 0)
    left = (hi - lo) - TILE_UNROLL * n_main
    size = TILE_UNROLL // 2
    while size >= 1:
        start = hi - (left & (2 * size - 1))

        @pl.when((left & size) != 0)
        def _(start=start, size=size):
            for u in range(size):
                fn(start + u)

        size //= 2


def _rows8(x):
    t = jnp.sum(jnp.sum(x, axis=0), axis=0, keepdims=True)
    return jnp.broadcast_to(t, (SUBLANES, BLK))


def _dsa_kernel(q_ref, iq_ref, misc_ref, ika_ref, ikb_ref, k_ref, v_ref, o_ref,
                sc_ref, tb_ref, tau_ref, mx_ref, mn_ref, kn_ref, mb_ref, ls_ref, acc_ref,
                m_ref, l_ref, *, n_rows_total):
    f32, bf16 = jnp.float32, jnp.bfloat16
    g = pl.program_id(1)
    nt = g + 2
    q0 = (g + 1) * BLK
    ones = jnp.ones((LANES, LANES), bf16)
    shape3 = (STRIPS, SUBLANES, BLK)
    keyi = (lax.broadcasted_iota(jnp.int32, shape3, 0) * SUBLANES
            + lax.broadcasted_iota(jnp.int32, shape3, 1))
    qryi = lax.broadcasted_iota(jnp.int32, shape3, 2)

    misc_t = misc_ref[0].T
    w8 = [jnp.broadcast_to(misc_t[IDX_HEAD_DIM + hd:IDX_HEAD_DIM + hd + 1, :], (SUBLANES, BLK))
          for hd in range(IDX_HEADS)]

    def score_tile(j):
        r = pl.multiple_of(j * BLK, BLK)
        ka = ika_ref[0, pl.ds(r, BLK), :]
        kb = ikb_ref[0, pl.ds(r, BLK), :]
        acc = jnp.zeros(shape3, f32)
        for p in range(IDX_HEADS // 2):
            slab = iq_ref[0, :, p * LANES:(p + 1) * LANES]
            le = lax.dot_general(ka, slab, _NT, preferred_element_type=f32).reshape(shape3)
            lo = lax.dot_general(kb, slab, _NT, preferred_element_type=f32).reshape(shape3)
            acc = (acc + jnp.maximum(le, 0.0) * w8[2 * p][None]
                   + jnp.maximum(lo, 0.0) * w8[2 * p + 1][None])
        return acc

    def put_tile(j, s, valid):
        lo = s if valid is None else jnp.where(valid, s, -jnp.inf)
        hi = s if valid is None else jnp.where(valid, s, jnp.inf)
        sc_ref[j] = lo
        mx_ref[...] = jnp.maximum(mx_ref[...], jnp.max(lo, axis=0))
        mn_ref[...] = jnp.minimum(mn_ref[...], jnp.min(hi, axis=0))

    mx_ref[...] = jnp.full((SUBLANES, BLK), -jnp.inf, f32)
    mn_ref[...] = jnp.full((SUBLANES, BLK), jnp.inf, f32)
    put_tile(0, score_tile(0), keyi >= PAD_ROWS)
    put_tile(nt - 1, score_tile(nt - 1), keyi <= qryi)

    _tile_loop(1, nt - 1, lambda j: put_tile(j, score_tile(j), None))

    kf = float(INDEX_TOPK)
    qrow = q0 + lax.broadcasted_iota(jnp.int32, (SUBLANES, BLK), 1)
    n_valid = (qrow - (PAD_ROWS - 1)).astype(f32)
    n_virtual = (n_rows_total - 1 - qrow).astype(f32)
    rowmax = jnp.broadcast_to(jnp.max(mx_ref[...], axis=0, keepdims=True), (SUBLANES, BLK))
    rowmin = jnp.broadcast_to(jnp.min(mn_ref[...], axis=0, keepdims=True), (SUBLANES, BLK))
    zacc = jnp.zeros((N_ACC, SUBLANES, BLK), f32)

    def partial(hit):
        return jnp.sum(jnp.where(hit, 1.0, 0.0).reshape(STRIPS // N_ACC, N_ACC, SUBLANES, BLK), axis=0)

    def bump(acc, hit):
        hit = hit.reshape(STRIPS // N_ACC, N_ACC, SUBLANES, BLK)
        for s in range(STRIPS // N_ACC):
            acc = jnp.where(hit[s], acc + 1.0, acc)
        return acc

    def count_ge(cand):
        def pair(i, acc):
            acc = bump(acc, sc_ref[2 * i] >= cand[None])
            return bump(acc, sc_ref[2 * i + 1] >= cand[None])

        def single(j, acc):
            return bump(acc, sc_ref[j] >= cand[None])

        acc = lax.fori_loop(0, nt // 2, pair, zacc)
        return _rows8(lax.fori_loop(2 * (nt // 2), nt, single, acc))

    def total_of(c_proc, cand):
        return c_proc + jnp.where(cand <= NEG, n_virtual, 0.0)

    def active_of(klo, khi, clo):
        return (total_of(clo, _k2f(klo)) != kf) & (khi > klo + 1)

    kneg = _f2k(jnp.full((SUBLANES, BLK), NEG, f32))
    wide = n_valid > kf
    klo = jnp.where(wide, _f2k(jnp.maximum(rowmin, NEG)), kneg)
    khi = jnp.where(wide, _f2k(rowmax) + 1, klo + 1)
    clo = n_valid
    chi = jnp.zeros((SUBLANES, BLK), f32)

    def zero_body(j, accs):
        x = sc_ref[j]
        return accs[0] + partial(x >= 0.0), accs[1] + partial(x > 0.0)

    age, agt = lax.fori_loop(0, nt, zero_body, (zacc, zacc))
    c_ge0, c_gt0 = _rows8(age), _rows8(agt)
    kzero = _f2k(jnp.zeros((SUBLANES, BLK), f32))
    inside = (klo < kzero) & (kzero < khi)
    up = inside & (c_ge0 >= kf)
    dn = inside & (c_ge0 < kf)
    pin = up & (c_gt0 < kf)
    klo = jnp.where(up, kzero, klo)
    clo = jnp.where(up, c_ge0, clo)
    khi = jnp.where(dn, kzero, jnp.where(pin, kzero + 1, khi))
    chi = jnp.where(dn, c_ge0, jnp.where(pin, c_gt0, chi))

    def narrow(it, klo, khi, clo, chi):
        lo_f, hi_f = _k2f(klo), _k2f(khi)
        k_val = _f2k(lo_f + (hi_f - lo_f) * 0.5)
        k_mid = (klo >> 1) + (khi >> 1) + (klo & khi & 1)
        kt = jnp.where(it < VALUE_STEPS, k_val, k_mid)
        kt = jnp.minimum(jnp.maximum(kt, klo + 1), khi - 1)
        cand = _k2f(kt)
        c_proc = count_ge(cand)
        ge = total_of(c_proc, cand) >= kf
        act = active_of(klo, khi, clo)
        up = act & ge
        dn = act & jnp.logical_not(ge)
        return (jnp.where(up, kt, klo), jnp.where(dn, kt, khi),
                jnp.where(up, c_proc, clo), jnp.where(dn, c_proc, chi))

    klo, khi, clo, chi = lax.fori_loop(
        0, BLIND_STEPS, lambda it, st: narrow(it, *st), (klo, khi, clo, chi))

    def rows_left(klo, khi, clo):
        return jnp.max(jnp.where(active_of(klo, khi, clo), 1.0, 0.0))

    def search_step(st):
        _, it, klo, khi, clo, chi = st
        klo, khi, clo, chi = narrow(it, klo, khi, clo, chi)
        return rows_left(klo, khi, clo), it + 1, klo, khi, clo, chi

    _, _, klo, khi, clo, chi = lax.while_loop(
        lambda st: st[0] > 0.5, search_step,
        (rows_left(klo, khi, clo), jnp.int32(BLIND_STEPS), klo, khi, clo, chi))
    tau = _k2f(klo)
    tau_ref[...] = tau

    tied = clo > kf

    @pl.when(_any(tied))
    def _():
        need = kf - chi

        def scan(j, st):
            before, cross = st
            tb_ref[j] = before
            after = before + _rows8(partial(sc_ref[j] == tau[None]))
            here = (before < need) & (need <= after)
            return after, jnp.where(here, j.astype(f32), cross)

        zero8 = jnp.zeros((SUBLANES, BLK), f32)
        _, cross = lax.fori_loop(0, nt, scan, (zero8, zero8))
        j_lo = jnp.min(jnp.where(tied, cross, float(n_rows_total // BLK))).astype(jnp.int32)
        j_hi = jnp.max(jnp.where(tied, cross, 0.0)).astype(jnp.int32)
        need1 = need[0:1, :]
        tied1 = tied[0:1, :]
        tau1 = tau[0:1, :]
        tri = (lax.broadcasted_iota(jnp.int32, (BLK, BLK), 0)
               >= lax.broadcasted_iota(jnp.int32, (BLK, BLK), 1)).astype(bf16)

        def fix(j, c):
            x = sc_ref[j].reshape(BLK, BLK)
            eq = x == tau1
            rank = jnp.dot(tri, jnp.where(eq, 1.0, 0.0).astype(bf16),
                           preferred_element_type=f32) + tb_ref[j][0:1, :]
            cut = eq & (rank > need1) & tied1
            sc_ref[j] = jnp.where(cut, -jnp.inf, x).reshape(shape3)
            return c

        lax.fori_loop(j_lo, j_hi + 1, fix, 0)

        tau_tied = jnp.where(tied, tau, jnp.nan)[None]

        def drop(j, c):
            x = sc_ref[j]
            sc_ref[j] = jnp.where(x == tau_tied, -jnp.inf, x)
            return c

        lax.fori_loop(j_hi + 1, nt, drop, 0)

    @pl.when(g == 0)
    def _():
        kn_ref[...] = jnp.zeros(kn_ref.shape, f32)

        def norms(j, c):
            r = pl.multiple_of(j * BLK, BLK)
            for hd in range(ATTN_HEADS):
                kh = k_ref[0, pl.ds(r, BLK), hd * LANES:(hd + 1) * LANES].astype(f32)
                n2 = jnp.dot((kh * kh).astype(bf16), ones, preferred_element_type=f32)
                kn_ref[hd] = jnp.maximum(kn_ref[hd], jnp.max(n2, axis=0, keepdims=True))
            return c

        lax.fori_loop(0, n_rows_total // BLK, norms, 0)

    for hd in range(ATTN_HEADS):
        qh = q_ref[0, :, hd * LANES:(hd + 1) * LANES].astype(f32)
        qn2 = jnp.dot((qh * qh).astype(bf16), ones, preferred_element_type=f32)
        mb_ref[hd] = jnp.sqrt(qn2 * kn_ref[hd]) * BOUND_SLACK
    ls_ref[...] = jnp.zeros(ls_ref.shape, f32)
    acc_ref[...] = jnp.zeros(acc_ref.shape, f32)

    def bias_tile(j):
        sel = sc_ref[j] >= tau_ref[...][None]
        return jnp.where(sel, 0.0, NEG).reshape(BLK, BLK).T

    def attend_tile(j):
        r = pl.multiple_of(j * BLK, BLK)
        bias = bias_tile(j)
        b0 = bias[:, :LANES]
        b1 = bias[:, LANES:]
        for hd in range(ATTN_HEADS):
            sl = slice(hd * LANES, (hd + 1) * LANES)
            kh = k_ref[0, pl.ds(r, BLK), sl]
            vh = v_ref[0, pl.ds(r, BLK), sl]
            s = lax.dot_general(q_ref[0, :, sl], kh, _NT, preferred_element_type=f32)
            mb = mb_ref[hd]
            p0 = jnp.exp(s[:, :LANES] - mb + b0)
            p1 = jnp.exp(s[:, LANES:] - mb + b1)
            ls_ref[hd] += p0 + p1
            p = jnp.concatenate([p0, p1], axis=1).astype(bf16)
            acc_ref[hd] += jnp.dot(p, vh, preferred_element_type=f32)

    _tile_loop(0, nt, attend_tile)
    bad = jnp.float32(0.0)
    for hd in range(ATTN_HEADS):
        l = jnp.sum(ls_ref[hd], axis=1, keepdims=True)
        bad = jnp.maximum(bad, jnp.max(jnp.where(l >= L_MIN, 0.0, 1.0)))
        o_ref[0, :, hd * LANES:(hd + 1) * LANES] = (acc_ref[hd] / l).astype(bf16)

    @pl.when(bad > 0.5)
    def _():
        m_ref[...] = jnp.full(m_ref.shape, NEG, f32)
        l_ref[...] = jnp.zeros(l_ref.shape, f32)
        acc_ref[...] = jnp.zeros(acc_ref.shape, f32)

        def attend_online(j, c):
            r = pl.multiple_of(j * BLK, BLK)
            bias = bias_tile(j)
            for hd in range(ATTN_HEADS):
                sl = slice(hd * LANES, (hd + 1) * LANES)
                kh = k_ref[0, pl.ds(r, BLK), sl]
                vh = v_ref[0, pl.ds(r, BLK), sl]
                s = lax.dot_general(q_ref[0, :, sl], kh, _NT, preferred_element_type=f32) + bias
                m_old = m_ref[hd]
                m_new = jnp.maximum(m_old, jnp.max(s, axis=1, keepdims=True))
                p = jnp.exp(s - m_new)
                alpha = jnp.exp(m_old - m_new)
                l_ref[hd] = alpha * l_ref[hd] + jnp.sum(p, axis=1, keepdims=True)
                acc_ref[hd] = alpha * acc_ref[hd] + jnp.dot(p.astype(bf16), vh,
                                                            preferred_element_type=f32)
                m_ref[hd] = m_new
            return c

        lax.fori_loop(0, nt, attend_online, 0)
        for hd in range(ATTN_HEADS):
            o_ref[0, :, hd * LANES:(hd + 1) * LANES] = (acc_ref[hd] / l_ref[hd]).astype(bf16)


def _mlp_kernel(x_ref, a_ref, c_ref, woa_ref, woc_ref, g2_ref, wu_ref, wd_ref, gf_ref, o_ref):
    h = (x_ref[0]
         + jnp.dot(a_ref[0], woa_ref[...], preferred_element_type=jnp.float32)
         + jnp.dot(c_ref[0], woc_ref[...], preferred_element_type=jnp.float32))
    hn = _rms(h, g2_ref[...]).astype(jnp.bfloat16)
    acc = jnp.zeros_like(h)
    for c in range(D_FF // FF_CHUNK):
        sl = slice(c * FF_CHUNK, (c + 1) * FF_CHUNK)
        u = jnp.dot(hn, wu_ref[:, sl], preferred_element_type=jnp.float32)
        u = jnp.square(jnp.maximum(u, 0.0)).astype(jnp.bfloat16)
        acc = acc + jnp.dot(u, wd_ref[sl, :], preferred_element_type=jnp.float32)
    o_ref[0] = _rms(h + acc, gf_ref[...])


def _rope_tables(n_rows, head_dim, lanes_per_head):
    rot = head_dim // ROPE_FRACTION
    half = rot // 2
    inv = ROPE_THETA ** (-jnp.arange(0, rot, 2, dtype=jnp.float32) / rot)
    pos = jnp.arange(n_rows, dtype=jnp.float32) - float(PAD_ROWS)
    ang = pos[:, None] * inv[None, :]
    cos, sin = jnp.cos(ang), jnp.sin(ang)
    rest = lanes_per_head - rot
    one = jnp.ones((n_rows, rest), jnp.float32)
    zero = jnp.zeros((n_rows, rest), jnp.float32)
    zh = jnp.zeros((n_rows, half), jnp.float32)
    c = jnp.concatenate([cos, cos, one], axis=1)
    s1 = jnp.concatenate([zh, sin, zero], axis=1)
    s2 = jnp.concatenate([-sin, zh, zero], axis=1)
    reps = LANES // lanes_per_head
    return jnp.stack([jnp.tile(t, (1, reps)) for t in (c, s1, s2)])


def _const(shape):
    return pl.BlockSpec(shape, lambda b, i: (0,) * len(shape), pipeline_mode=pl.Buffered(1))


def _project(x, meta_tokens, attn_norm_g, w_in, conv_w, conv_b, conv_norm_g, conv_norm_b):
    B, S, D = x.shape
    nb = S // BLK
    T = S + BLK
    f32, bf16 = jnp.float32, jnp.bfloat16

    w = w_in[0]
    w_rep = jnp.concatenate(
        [w[:, 0:2048], w[:, 2120:3144], w[:, 2048:2120],
         jnp.zeros((D, W_IN_COLS - 3144), f32)], axis=1).astype(bf16)
    meta_pad = jnp.concatenate([jnp.zeros((PAD_ROWS, D), f32), meta_tokens.astype(f32)], axis=0)
    rq = _rope_tables(T, ATTN_HEAD_DIM, LANES)
    ri = _rope_tables(T, IDX_HEAD_DIM, IDX_HEAD_DIM)

    real = lambda b, i: (b, jnp.maximum(i - 1, 0), 0)
    padded = lambda b, i: (b, i, 0)
    return pl.pallas_call(
        _proj_kernel,
        grid=(B, nb + 1),
        in_specs=[
            pl.BlockSpec((1, BLK, D), real),
            _const((BLK, D)),
            _const((1, D)),
            _const((D, W_IN_COLS)),
            pl.BlockSpec((3, BLK, LANES), lambda b, i: (0, i, 0)),
            pl.BlockSpec((3, BLK, LANES), lambda b, i: (0, i, 0)),
            _const((CONV_WIDTH, CONV_CH)),
            _const((1, CONV_CH)),
            _const((1, CONV_CH)),
            _const((1, CONV_CH)),
        ],
        out_specs=[
            pl.BlockSpec((1, BLK, ATTN_W), real),
            pl.BlockSpec((1, BLK, ATTN_W), real),
            pl.BlockSpec((1, BLK, LANES), real),
            pl.BlockSpec((1, BLK, CONV_CH), real),
            pl.BlockSpec((1, BLK, ATTN_W), padded),
            pl.BlockSpec((1, BLK, ATTN_W), padded),
            pl.BlockSpec((1, BLK, LANES), padded),
            pl.BlockSpec((1, BLK, LANES), padded),
        ],
        out_shape=[
            jax.ShapeDtypeStruct((B, S, ATTN_W), bf16),
            jax.ShapeDtypeStruct((B, S, ATTN_W), bf16),
            jax.ShapeDtypeStruct((B, S, LANES), f32),
            jax.ShapeDtypeStruct((B, S, CONV_CH), bf16),
            jax.ShapeDtypeStruct((B, T, ATTN_W), bf16),
            jax.ShapeDtypeStruct((B, T, ATTN_W), bf16),
            jax.ShapeDtypeStruct((B, T, LANES), bf16),
            jax.ShapeDtypeStruct((B, T, LANES), bf16),
        ],
        scratch_shapes=[pltpu.VMEM((CARRY + BLK, CONV_CH), f32),
                        pltpu.VMEM((CARRY + BLK, CONV_CH), f32)],
        compiler_params=pltpu.CompilerParams(
            dimension_semantics=("arbitrary", "arbitrary"), vmem_limit_bytes=VMEM_LIMIT),
        name="proj_rope_conv",
    )(x, meta_pad, attn_norm_g, w_rep, rq, ri, conv_w[0], conv_b, conv_norm_g, conv_norm_b)


def _attend(q, iq, misc, ika, ikb, k, v):
    B, S, _ = q.shape
    T = k.shape[1]
    nb = S // BLK
    f32 = jnp.float32
    blk = lambda b, g: (b, g, 0)
    whole = lambda width: pl.BlockSpec((1, T, width), lambda b, g: (b, 0, 0),
                                       pipeline_mode=pl.Buffered(1))
    return pl.pallas_call(
        functools.partial(_dsa_kernel, n_rows_total=T),
        grid=(B, nb),
        in_specs=[
            pl.BlockSpec((1, BLK, ATTN_W), blk),
            pl.BlockSpec((1, BLK, ATTN_W), blk),
            pl.BlockSpec((1, BLK, LANES), blk),
            whole(LANES), whole(LANES), whole(ATTN_W), whole(ATTN_W),
        ],
        out_specs=pl.BlockSpec((1, BLK, ATTN_W), blk),
        out_shape=jax.ShapeDtypeStruct((B, S, ATTN_W), jnp.bfloat16),
        scratch_shapes=[
            pltpu.VMEM((nb + 1, STRIPS, SUBLANES, BLK), f32),
            pltpu.VMEM((nb + 1, SUBLANES, BLK), f32),
            pltpu.VMEM((SUBLANES, BLK), f32),
            pltpu.VMEM((SUBLANES, BLK), f32),
            pltpu.VMEM((SUBLANES, BLK), f32),
            pltpu.VMEM((ATTN_HEADS, 1, LANES), f32),
            pltpu.VMEM((ATTN_HEADS, BLK, LANES), f32),
            pltpu.VMEM((ATTN_HEADS, BLK, LANES), f32),
            pltpu.VMEM((ATTN_HEADS, BLK, ATTN_HEAD_DIM), f32),
            pltpu.VMEM((ATTN_HEADS, BLK, 1), f32),
            pltpu.VMEM((ATTN_HEADS, BLK, 1), f32),
        ],
        compiler_params=pltpu.CompilerParams(
            dimension_semantics=("arbitrary", "arbitrary"), vmem_limit_bytes=VMEM_LIMIT),
        name="dsa_attention",
    )(q, iq, misc, ika, ikb, k, v)


def _out_mlp(x, attn, conv, w_out, mlp_norm_g, w_up, w_down, final_norm_g):
    B, S, D = x.shape
    bf16 = jnp.bfloat16
    wo = w_out[0].astype(bf16)
    rows = lambda b, i: (b, i, 0)
    return pl.pallas_call(
        _mlp_kernel,
        grid=(B, S // MLP_ROWS),
        in_specs=[
            pl.BlockSpec((1, MLP_ROWS, D), rows),
            pl.BlockSpec((1, MLP_ROWS, ATTN_W), rows),
            pl.BlockSpec((1, MLP_ROWS, CONV_CH), rows),
            _const((ATTN_W, D)),
            _const((CONV_CH, D)),
            _const((1, D)),
            _const((D, D_FF)),
            _const((D_FF, D)),
            _const((1, D)),
        ],
        out_specs=pl.BlockSpec((1, MLP_ROWS, D), rows),
        out_shape=jax.ShapeDtypeStruct((B, S, D), x.dtype),
        compiler_params=pltpu.CompilerParams(
            dimension_semantics=("arbitrary", "arbitrary"), vmem_limit_bytes=VMEM_LIMIT),
        name="out_mlp",
    )(x, attn, conv, wo[:ATTN_W], wo[ATTN_W:], mlp_norm_g, w_up[0].astype(bf16),
      w_down[0].astype(bf16), final_norm_g.reshape(1, D))


def kernel(x, meta_tokens, attn_norm_g, w_in, conv_w, conv_b, conv_norm_g, conv_norm_b,
           w_out, mlp_norm_g, w_up, w_down, final_norm_g):
    B, S, D = x.shape
    assert D == D_MODEL and S % MLP_ROWS == 0 and attn_norm_g.shape[0] == 1
    q, iq, misc, conv, k, v, ika, ikb = _project(
        x, meta_tokens, attn_norm_g, w_in, conv_w, conv_b, conv_norm_g, conv_norm_b)
    attn = _attend(q, iq, misc, ika, ikb, k, v)
    return _out_mlp(x, attn, conv, w_out, mlp_norm_g, w_up, w_down, final_norm_g)
```

```python
import functools

import jax
import jax.numpy as jnp
from jax import lax
from jax.experimental import pallas as pl
from jax.experimental.pallas import tpu as pltpu

D_MODEL = 1024
N_META = 16
ATTN_HEADS = 4
ATTN_HEAD_DIM = 128
ATTN_W = ATTN_HEADS * ATTN_HEAD_DIM
ROPE_THETA = 500000.0
ROPE_FRACTION = 4
IDX_HEADS = 8
IDX_HEAD_DIM = 64
INDEX_TOPK = 256
CONV_CH = 512
CONV_GROUPS = 4
CONV_WIDTH = 31
D_FF = 4 * D_MODEL
NORM_EPS = 1e-5
NEG = -1e30

LANES = 128
SUBLANES = 8
BLK = 256
STRIPS = BLK // SUBLANES
PAD_ROWS = BLK - N_META
CARRY = 32
W_IN_COLS = 6 * 512 + LANES
VALUE_STEPS = 24
BLIND_STEPS = 12
N_ACC = 4
TILE_UNROLL = 8
BOUND_SLACK = 1.0 + 2.0 ** -6
L_MIN = 1e-30
MLP_ROWS = 512
FF_CHUNK = 1024
VMEM_LIMIT = 60 * 1024 * 1024

_NT = (((1,), (1,)), ((), ()))


def _rms(x, g):
    return x * lax.rsqrt(jnp.mean(x * x, axis=-1, keepdims=True) + NORM_EPS) * g


def _rope(x, c, s1, s2, half):
    return x * c + pltpu.roll(x, half, 1) * s1 + pltpu.roll(x, LANES - half, 1) * s2


def _proj_kernel(x_ref, meta_ref, g_ref, w_ref, rq_ref, ri_ref, cw_ref, cb_ref, lg_ref, lb_ref,
                 q_ref, iq_ref, misc_ref, conv_ref, k_ref, v_ref, ika_ref, ikb_ref, u_scr, win_scr):
    i = pl.program_id(1)
    h = jnp.where(i == 0, meta_ref[...], x_ref[0])
    xn = _rms(h, g_ref[...]).astype(jnp.bfloat16)
    proj = jnp.dot(xn, w_ref[...], preferred_element_type=jnp.float32)

    cq, s1q, s2q = rq_ref[0], rq_ref[1], rq_ref[2]
    ci, s1i, s2i = ri_ref[0], ri_ref[1], ri_ref[2]
    q_half = ATTN_HEAD_DIM // ROPE_FRACTION // 2
    i_half = IDX_HEAD_DIM // ROPE_FRACTION // 2
    for hd in range(ATTN_HEADS):
        sl = slice(hd * LANES, (hd + 1) * LANES)
        qh = _rope(proj[:, sl], cq, s1q, s2q, q_half) * (ATTN_HEAD_DIM ** -0.5)
        q_ref[0, :, sl] = qh.astype(jnp.bfloat16)
        kh = _rope(proj[:, 512 + hd * LANES:512 + (hd + 1) * LANES], cq, s1q, s2q, q_half)
        k_ref[0, :, sl] = kh.astype(jnp.bfloat16)
        ih = _rope(proj[:, 1536 + hd * LANES:1536 + (hd + 1) * LANES], ci, s1i, s2i, i_half)
        iq_ref[0, :, sl] = (ih * (IDX_HEAD_DIM ** -0.5)).astype(jnp.bfloat16)
    v_ref[0] = proj[:, 1024:1536].astype(jnp.bfloat16)

    lane = lax.broadcasted_iota(jnp.int32, (BLK, LANES), 1)
    is_ik = lane < IDX_HEAD_DIM
    slab = proj[:, 3072:3072 + LANES]
    roped = _rope(slab, jnp.where(is_ik, ci, 1.0), jnp.where(is_ik, s1i, 0.0),
                  jnp.where(is_ik, s2i, 0.0), i_half)
    misc_ref[0] = jnp.where(is_ik, roped, slab * (IDX_HEADS ** -0.5))
    ika = jnp.where(is_ik, roped, 0.0)
    ika_ref[0] = ika.astype(jnp.bfloat16)
    ikb_ref[0] = pltpu.roll(ika, IDX_HEAD_DIM, 1).astype(jnp.bfloat16)

    @pl.when(i == 0)
    def _():
        u_scr[0:CARRY, :] = jnp.zeros((CARRY, CONV_CH), jnp.float32)

    ga = proj[:, 2048:2560]
    gg = proj[:, 2560:3072]
    u_scr[CARRY:CARRY + BLK, :] = ga * jax.nn.sigmoid(gg)
    y = jnp.zeros((BLK, CONV_CH), jnp.float32) + cb_ref[...]
    for b in range(SUBLANES):
        base = CARRY - (CONV_WIDTH - 1) + b
        n_a = (CONV_WIDTH - 1 - b) // SUBLANES + 1
        rows = BLK + SUBLANES * (n_a - 1)
        win_scr[0:rows, :] = u_scr[base:base + rows, :]
        for a in range(n_a):
            j = SUBLANES * a + b
            y = y + win_scr[SUBLANES * a:SUBLANES * a + BLK, :] * cw_ref[j:j + 1, :]
    u_scr[0:CARRY, :] = u_scr[BLK:BLK + CARRY, :]
    gsz = CONV_CH // CONV_GROUPS
    for gi in range(CONV_GROUPS):
        sl = slice(gi * gsz, (gi + 1) * gsz)
        yg = y[:, sl]
        mu = jnp.mean(yg, axis=-1, keepdims=True)
        d = yg - mu
        var = jnp.mean(d * d, axis=-1, keepdims=True)
        yn = d * lax.rsqrt(var + NORM_EPS) * lg_ref[:, sl] + lb_ref[:, sl]
        conv_ref[0, :, sl] = (yn * jax.nn.sigmoid(yn)).astype(jnp.bfloat16)


def _f2k(f):
    b = lax.bitcast_convert_type(f, jnp.int32)
    return b ^ ((b >> 31) & jnp.int32(0x7FFFFFFF))


def _k2f(k):
    return lax.bitcast_convert_type(k ^ ((k >> 31) & jnp.int32(0x7FFFFFFF)), jnp.float32)


def _any(mask):
    return jnp.max(jnp.where(mask, 1.0, 0.0)) > 0.5


def _tile_loop(lo, hi, fn):
    n_main = (hi - lo) // TILE_UNROLL

    def main(i, c):
        for u in range(TILE_UNROLL):
            fn(lo + TILE_UNROLL * i + u)
        return c

    lax.fori_loop(0, n_main, main, 0)
    left = (hi - lo) - TILE_UNROLL * n_main
    size = TILE_UNROLL // 2
    while size >= 1:
        start = hi - (left & (2 * size - 1))

        @pl.when((left & size) != 0)
        def _(start=start, size=size):
            for u in range(size):
                fn(start + u)

        size //= 2


def _rows8(x):
    t = jnp.sum(jnp.sum(x, axis=0), axis=0, keepdims=True)
    return jnp.broadcast_to(t, (SUBLANES, BLK))


def _dsa_kernel(q_ref, iq_ref, misc_ref, ika_ref, ikb_ref, k_ref, v_ref, o_ref,
                sc_ref, tb_ref, tau_ref, mx_ref, mn_ref, ge_ref, gt_ref, kn_ref, mb_ref, ls_ref, acc_ref,
                m_ref, l_ref, *, n_rows_total):
    f32, bf16 = jnp.float32, jnp.bfloat16
    g = pl.program_id(1)
    nt = g + 2
    q0 = (g + 1) * BLK
    ones = jnp.ones((LANES, LANES), bf16)
    shape3 = (STRIPS, SUBLANES, BLK)
    keyi = (lax.broadcasted_iota(jnp.int32, shape3, 0) * SUBLANES
            + lax.broadcasted_iota(jnp.int32, shape3, 1))
    qryi = lax.broadcasted_iota(jnp.int32, shape3, 2)

    misc_t = misc_ref[0].T
    w8 = [jnp.broadcast_to(misc_t[IDX_HEAD_DIM + hd:IDX_HEAD_DIM + hd + 1, :], (SUBLANES, BLK))
          for hd in range(IDX_HEADS)]

    def partial(hit):
        return jnp.sum(jnp.where(hit, 1.0, 0.0).reshape(STRIPS // N_ACC, N_ACC, SUBLANES, BLK), axis=0)

    def score_tile(j):
        r = pl.multiple_of(j * BLK, BLK)
        ka = ika_ref[0, pl.ds(r, BLK), :]
        kb = ikb_ref[0, pl.ds(r, BLK), :]
        acc = jnp.zeros(shape3, f32)
        for p in range(IDX_HEADS // 2):
            slab = iq_ref[0, :, p * LANES:(p + 1) * LANES]
            le = lax.dot_general(ka, slab, _NT, preferred_element_type=f32).reshape(shape3)
            lo = lax.dot_general(kb, slab, _NT, preferred_element_type=f32).reshape(shape3)
            acc = (acc + jnp.maximum(le, 0.0) * w8[2 * p][None]
                   + jnp.maximum(lo, 0.0) * w8[2 * p + 1][None])
        return acc

    def put_tile(j, s, valid):
        lo = s if valid is None else jnp.where(valid, s, -jnp.inf)
        hi = s if valid is None else jnp.where(valid, s, jnp.inf)
        sc_ref[j] = lo
        mx_ref[...] = jnp.maximum(mx_ref[...], jnp.max(lo, axis=0))
        mn_ref[...] = jnp.minimum(mn_ref[...], jnp.min(hi, axis=0))
        ge_ref[...] += partial(lo >= 0.0)
        gt_ref[...] += partial(lo > 0.0)

    mx_ref[...] = jnp.full((SUBLANES, BLK), -jnp.inf, f32)
    mn_ref[...] = jnp.full((SUBLANES, BLK), jnp.inf, f32)
    ge_ref[...] = jnp.zeros(ge_ref.shape, f32)
    gt_ref[...] = jnp.zeros(gt_ref.shape, f32)
    put_tile(0, score_tile(0), keyi >= PAD_ROWS)
    put_tile(nt - 1, score_tile(nt - 1), keyi <= qryi)

    _tile_loop(1, nt - 1, lambda j: put_tile(j, score_tile(j), None))

    kf = float(INDEX_TOPK)
    qrow = q0 + lax.broadcasted_iota(jnp.int32, (SUBLANES, BLK), 1)
    n_valid = (qrow - (PAD_ROWS - 1)).astype(f32)
    n_virtual = (n_rows_total - 1 - qrow).astype(f32)
    rowmax = jnp.broadcast_to(jnp.max(mx_ref[...], axis=0, keepdims=True), (SUBLANES, BLK))
    rowmin = jnp.broadcast_to(jnp.min(mn_ref[...], axis=0, keepdims=True), (SUBLANES, BLK))
    zacc = jnp.zeros((N_ACC, SUBLANES, BLK), f32)

    def bump(acc, hit):
        hit = hit.reshape(STRIPS // N_ACC, N_ACC, SUBLANES, BLK)
        for s in range(STRIPS // N_ACC):
            acc = jnp.where(hit[s], acc + 1.0, acc)
        return acc

    def count_ge(cand):
        def pair(i, acc):
            acc = bump(acc, sc_ref[2 * i] >= cand[None])
            return bump(acc, sc_ref[2 * i + 1] >= cand[None])

        def single(j, acc):
            return bump(acc, sc_ref[j] >= cand[None])

        acc = lax.fori_loop(0, nt // 2, pair, zacc)
        return _rows8(lax.fori_loop(2 * (nt // 2), nt, single, acc))

    def total_of(c_proc, cand):
        return c_proc + jnp.where(cand <= NEG, n_virtual, 0.0)

    def active_of(klo, khi, clo):
        return (total_of(clo, _k2f(klo)) != kf) & (khi > klo + 1)

    kneg = _f2k(jnp.full((SUBLANES, BLK), NEG, f32))
    wide = n_valid > kf
    klo = jnp.where(wide, _f2k(jnp.maximum(rowmin, NEG)), kneg)
    khi = jnp.where(wide, _f2k(rowmax) + 1, klo + 1)
    clo = n_valid
    chi = jnp.zeros((SUBLANES, BLK), f32)

    c_ge0, c_gt0 = _rows8(ge_ref[...]), _rows8(gt_ref[...])
    kzero = _f2k(jnp.zeros((SUBLANES, BLK), f32))
    inside = (klo < kzero) & (kzero < khi)
    up = inside & (c_ge0 >= kf)
    dn = inside & (c_ge0 < kf)
    pin = up & (c_gt0 < kf)
    klo = jnp.where(up, kzero, klo)
    clo = jnp.where(up, c_ge0, clo)
    khi = jnp.where(dn, kzero, jnp.where(pin, kzero + 1, khi))
    chi = jnp.where(dn, c_ge0, jnp.where(pin, c_gt0, chi))

    def narrow(it, klo, khi, clo, chi):
        lo_f, hi_f = _k2f(klo), _k2f(khi)
        k_val = _f2k(lo_f + (hi_f - lo_f) * 0.5)
        k_mid = (klo >> 1) + (khi >> 1) + (klo & khi & 1)
        kt = jnp.where(it < VALUE_STEPS, k_val, k_mid)
        kt = jnp.minimum(jnp.maximum(kt, klo + 1), khi - 1)
        cand = _k2f(kt)
        c_proc = count_ge(cand)
        ge = total_of(c_proc, cand) >= kf
        act = active_of(klo, khi, clo)
        up = act & ge
        dn = act & jnp.logical_not(ge)
        return (jnp.where(up, kt, klo), jnp.where(dn, kt, khi),
                jnp.where(up, c_proc, clo), jnp.where(dn, c_proc, chi))

    klo, khi, clo, chi = lax.fori_loop(
        0, BLIND_STEPS, lambda it, st: narrow(it, *st), (klo, khi, clo, chi))

    def rows_left(klo, khi, clo):
        return jnp.max(jnp.where(active_of(klo, khi, clo), 1.0, 0.0))

    def search_step(st):
        _, it, klo, khi, clo, chi = st
        klo, khi, clo, chi = narrow(it, klo, khi, clo, chi)
        return rows_left(klo, khi, clo), it + 1, klo, khi, clo, chi

    _, _, klo, khi, clo, chi = lax.while_loop(
        lambda st: st[0] > 0.5, search_step,
        (rows_left(klo, khi, clo), jnp.int32(BLIND_STEPS), klo, khi, clo, chi))
    tau = _k2f(klo)
    tau_ref[...] = tau

    tied = clo > kf

    @pl.when(_any(tied))
    def _():
        need = kf - chi

        def scan(j, st):
            before, cross = st
            tb_ref[j] = before
            after = before + _rows8(partial(sc_ref[j] == tau[None]))
            here = (before < need) & (need <= after)
            return after, jnp.where(here, j.astype(f32), cross)

        zero8 = jnp.zeros((SUBLANES, BLK), f32)
        _, cross = lax.fori_loop(0, nt, scan, (zero8, zero8))
        j_lo = jnp.min(jnp.where(tied, cross, float(n_rows_total // BLK))).astype(jnp.int32)
        j_hi = jnp.max(jnp.where(tied, cross, 0.0)).astype(jnp.int32)
        need1 = need[0:1, :]
        tied1 = tied[0:1, :]
        tau1 = tau[0:1, :]
        tri = (lax.broadcasted_iota(jnp.int32, (BLK, BLK), 0)
               >= lax.broadcasted_iota(jnp.int32, (BLK, BLK), 1)).astype(bf16)

        def fix(j, c):
            x = sc_ref[j].reshape(BLK, BLK)
            eq = x == tau1
            rank = jnp.dot(tri, jnp.where(eq, 1.0, 0.0).astype(bf16),
                           preferred_element_type=f32) + tb_ref[j][0:1, :]
            cut = eq & (rank > need1) & tied1
            sc_ref[j] = jnp.where(cut, -jnp.inf, x).reshape(shape3)
            return c

        lax.fori_loop(j_lo, j_hi + 1, fix, 0)

        tau_tied = jnp.where(tied, tau, jnp.nan)[None]

        def drop(j, c):
            x = sc_ref[j]
            sc_ref[j] = jnp.where(x == tau_tied, -jnp.inf, x)
            return c

        lax.fori_loop(j_hi + 1, nt, drop, 0)

    @pl.when(g == 0)
    def _():
        kn_ref[...] = jnp.zeros(kn_ref.shape, f32)

        def norms(j, c):
            r = pl.multiple_of(j * BLK, BLK)
            for hd in range(ATTN_HEADS):
                kh = k_ref[0, pl.ds(r, BLK), hd * LANES:(hd + 1) * LANES].astype(f32)
                n2 = jnp.dot((kh * kh).astype(bf16), ones, preferred_element_type=f32)
                kn_ref[hd] = jnp.maximum(kn_ref[hd], jnp.max(n2, axis=0, keepdims=True))
            return c

        lax.fori_loop(0, n_rows_total // BLK, norms, 0)

    for hd in range(ATTN_HEADS):
        qh = q_ref[0, :, hd * LANES:(hd + 1) * LANES].astype(f32)
        qn2 = jnp.dot((qh * qh).astype(bf16), ones, preferred_element_type=f32)
        mb_ref[hd] = jnp.sqrt(qn2 * kn_ref[hd]) * BOUND_SLACK
    ls_ref[...] = jnp.zeros(ls_ref.shape, f32)
    acc_ref[...] = jnp.zeros(acc_ref.shape, f32)

    def bias_tile(j):
        sel = sc_ref[j] >= tau_ref[...][None]
        return jnp.where(sel, 0.0, NEG).reshape(BLK, BLK).T

    def attend_tile(j):
        r = pl.multiple_of(j * BLK, BLK)
        bias = bias_tile(j)
        b0 = bias[:, :LANES]
        b1 = bias[:, LANES:]
        for hd in range(ATTN_HEADS):
            sl = slice(hd * LANES, (hd + 1) * LANES)
            kh = k_ref[0, pl.ds(r, BLK), sl]
            vh = v_ref[0, pl.ds(r, BLK), sl]
            s = lax.dot_general(q_ref[0, :, sl], kh, _NT, preferred_element_type=f32)
            mb = mb_ref[hd]
            p0 = jnp.exp(s[:, :LANES] - mb + b0)
            p1 = jnp.exp(s[:, LANES:] - mb + b1)
            ls_ref[hd] += p0 + p1
            p = jnp.concatenate([p0, p1], axis=1).astype(bf16)
            acc_ref[hd] += jnp.dot(p, vh, preferred_element_type=f32)

    _tile_loop(0, nt, attend_tile)
    bad = jnp.float32(0.0)
    for hd in range(ATTN_HEADS):
        l = jnp.sum(ls_ref[hd], axis=1, keepdims=True)
        bad = jnp.maximum(bad, jnp.max(jnp.where(l >= L_MIN, 0.0, 1.0)))
        o_ref[0, :, hd * LANES:(hd + 1) * LANES] = (acc_ref[hd] / l).astype(bf16)

    @pl.when(bad > 0.5)
    def _():
        m_ref[...] = jnp.full(m_ref.shape, NEG, f32)
        l_ref[...] = jnp.zeros(l_ref.shape, f32)
        acc_ref[...] = jnp.zeros(acc_ref.shape, f32)

        def attend_online(j, c):
            r = pl.multiple_of(j * BLK, BLK)
            bias = bias_tile(j)
            for hd in range(ATTN_HEADS):
                sl = slice(hd * LANES, (hd + 1) * LANES)
                kh = k_ref[0, pl.ds(r, BLK), sl]
                vh = v_ref[0, pl.ds(r, BLK), sl]
                s = lax.dot_general(q_ref[0, :, sl], kh, _NT, preferred_element_type=f32) + bias
                m_old = m_ref[hd]
                m_new = jnp.maximum(m_old, jnp.max(s, axis=1, keepdims=True))
                p = jnp.exp(s - m_new)
                alpha = jnp.exp(m_old - m_new)
                l_ref[hd] = alpha * l_ref[hd] + jnp.sum(p, axis=1, keepdims=True)
                acc_ref[hd] = alpha * acc_ref[hd] + jnp.dot(p.astype(bf16), vh,
                                                            preferred_element_type=f32)
                m_ref[hd] = m_new
            return c

        lax.fori_loop(0, nt, attend_online, 0)
        for hd in range(ATTN_HEADS):
            o_ref[0, :, hd * LANES:(hd + 1) * LANES] = (acc_ref[hd] / l_ref[hd]).astype(bf16)


def _mlp_kernel(x_ref, a_ref, c_ref, woa_ref, woc_ref, g2_ref, wu_ref, wd_ref, gf_ref, o_ref):
    h = (x_ref[0]
         + jnp.dot(a_ref[0], woa_ref[...], preferred_element_type=jnp.float32)
         + jnp.dot(c_ref[0], woc_ref[...], preferred_element_type=jnp.float32))
    hn = _rms(h, g2_ref[...]).astype(jnp.bfloat16)
    acc = jnp.zeros_like(h)
    for c in range(D_FF // FF_CHUNK):
        sl = slice(c * FF_CHUNK, (c + 1) * FF_CHUNK)
        u = jnp.dot(hn, wu_ref[:, sl], preferred_element_type=jnp.float32)
        u = jnp.square(jnp.maximum(u, 0.0)).astype(jnp.bfloat16)
        acc = acc + jnp.dot(u, wd_ref[sl, :], preferred_element_type=jnp.float32)
    o_ref[0] = _rms(h + acc, gf_ref[...])


def _rope_tables(n_rows, head_dim, lanes_per_head):
    rot = head_dim // ROPE_FRACTION
    half = rot // 2
    inv = ROPE_THETA ** (-jnp.arange(0, rot, 2, dtype=jnp.float32) / rot)
    pos = jnp.arange(n_rows, dtype=jnp.float32) - float(PAD_ROWS)
    ang = pos[:, None] * inv[None, :]
    cos, sin = jnp.cos(ang), jnp.sin(ang)
    rest = lanes_per_head - rot
    one = jnp.ones((n_rows, rest), jnp.float32)
    zero = jnp.zeros((n_rows, rest), jnp.float32)
    zh = jnp.zeros((n_rows, half), jnp.float32)
    c = jnp.concatenate([cos, cos, one], axis=1)
    s1 = jnp.concatenate([zh, sin, zero], axis=1)
    s2 = jnp.concatenate([-sin, zh, zero], axis=1)
    reps = LANES // lanes_per_head
    return jnp.stack([jnp.tile(t, (1, reps)) for t in (c, s1, s2)])


def _const(shape):
    return pl.BlockSpec(shape, lambda b, i: (0,) * len(shape), pipeline_mode=pl.Buffered(1))


def _project(x, meta_tokens, attn_norm_g, w_in, conv_w, conv_b, conv_norm_g, conv_norm_b):
    B, S, D = x.shape
    nb = S // BLK
    T = S + BLK
    f32, bf16 = jnp.float32, jnp.bfloat16

    w = w_in[0]
    w_rep = jnp.concatenate(
        [w[:, 0:2048], w[:, 2120:3144], w[:, 2048:2120],
         jnp.zeros((D, W_IN_COLS - 3144), f32)], axis=1).astype(bf16)
    meta_pad = jnp.concatenate([jnp.zeros((PAD_ROWS, D), f32), meta_tokens.astype(f32)], axis=0)
    rq = _rope_tables(T, ATTN_HEAD_DIM, LANES)
    ri = _rope_tables(T, IDX_HEAD_DIM, IDX_HEAD_DIM)

    real = lambda b, i: (b, jnp.maximum(i - 1, 0), 0)
    padded = lambda b, i: (b, i, 0)
    return pl.pallas_call(
        _proj_kernel,
        grid=(B, nb + 1),
        in_specs=[
            pl.BlockSpec((1, BLK, D), real),
            _const((BLK, D)),
            _const((1, D)),
            _const((D, W_IN_COLS)),
            pl.BlockSpec((3, BLK, LANES), lambda b, i: (0, i, 0)),
            pl.BlockSpec((3, BLK, LANES), lambda b, i: (0, i, 0)),
            _const((CONV_WIDTH, CONV_CH)),
            _const((1, CONV_CH)),
            _const((1, CONV_CH)),
            _const((1, CONV_CH)),
        ],
        out_specs=[
            pl.BlockSpec((1, BLK, ATTN_W), real),
            pl.BlockSpec((1, BLK, ATTN_W), real),
            pl.BlockSpec((1, BLK, LANES), real),
            pl.BlockSpec((1, BLK, CONV_CH), real),
            pl.BlockSpec((1, BLK, ATTN_W), padded),
            pl.BlockSpec((1, BLK, ATTN_W), padded),
            pl.BlockSpec((1, BLK, LANES), padded),
            pl.BlockSpec((1, BLK, LANES), padded),
        ],
        out_shape=[
            jax.ShapeDtypeStruct((B, S, ATTN_W), bf16),
            jax.ShapeDtypeStruct((B, S, ATTN_W), bf16),
            jax.ShapeDtypeStruct((B, S, LANES), f32),
            jax.ShapeDtypeStruct((B, S, CONV_CH), bf16),
            jax.ShapeDtypeStruct((B, T, ATTN_W), bf16),
            jax.ShapeDtypeStruct((B, T, ATTN_W), bf16),
            jax.ShapeDtypeStruct((B, T, LANES), bf16),
            jax.ShapeDtypeStruct((B, T, LANES), bf16),
        ],
        scratch_shapes=[pltpu.VMEM((CARRY + BLK, CONV_CH), f32),
                        pltpu.VMEM((CARRY + BLK, CONV_CH), f32)],
        compiler_params=pltpu.CompilerParams(
            dimension_semantics=("arbitrary", "arbitrary"), vmem_limit_bytes=VMEM_LIMIT),
        name="proj_rope_conv",
    )(x, meta_pad, attn_norm_g, w_rep, rq, ri, conv_w[0], conv_b, conv_norm_g, conv_norm_b)


def _attend(q, iq, misc, ika, ikb, k, v):
    B, S, _ = q.shape
    T = k.shape[1]
    nb = S // BLK
    f32 = jnp.float32
    blk = lambda b, g: (b, g, 0)
    whole = lambda width: pl.BlockSpec((1, T, width), lambda b, g: (b, 0, 0),
                                       pipeline_mode=pl.Buffered(1))
    return pl.pallas_call(
        functools.partial(_dsa_kernel, n_rows_total=T),
        grid=(B, nb),
        in_specs=[
            pl.BlockSpec((1, BLK, ATTN_W), blk),
            pl.BlockSpec((1, BLK, ATTN_W), blk),
            pl.BlockSpec((1, BLK, LANES), blk),
            whole(LANES), whole(LANES), whole(ATTN_W), whole(ATTN_W),
        ],
        out_specs=pl.BlockSpec((1, BLK, ATTN_W), blk),
        out_shape=jax.ShapeDtypeStruct((B, S, ATTN_W), jnp.bfloat16),
        scratch_shapes=[
            pltpu.VMEM((nb + 1, STRIPS, SUBLANES, BLK), f32),
            pltpu.VMEM((nb + 1, SUBLANES, BLK), f32),
            pltpu.VMEM((SUBLANES, BLK), f32),
            pltpu.VMEM((SUBLANES, BLK), f32),
            pltpu.VMEM((SUBLANES, BLK), f32),
            pltpu.VMEM((N_ACC, SUBLANES, BLK), f32),
            pltpu.VMEM((N_ACC, SUBLANES, BLK), f32),
            pltpu.VMEM((ATTN_HEADS, 1, LANES), f32),
            pltpu.VMEM((ATTN_HEADS, BLK, LANES), f32),
            pltpu.VMEM((ATTN_HEADS, BLK, LANES), f32),
            pltpu.VMEM((ATTN_HEADS, BLK, ATTN_HEAD_DIM), f32),
            pltpu.VMEM((ATTN_HEADS, BLK, 1), f32),
            pltpu.VMEM((ATTN_HEADS, BLK, 1), f32),
        ],
        compiler_params=pltpu.CompilerParams(
            dimension_semantics=("arbitrary", "arbitrary"), vmem_limit_bytes=VMEM_LIMIT),
        name="dsa_attention",
    )(q, iq, misc, ika, ikb, k, v)


def _out_mlp(x, attn, conv, w_out, mlp_norm_g, w_up, w_down, final_norm_g):
    B, S, D = x.shape
    bf16 = jnp.bfloat16
    wo = w_out[0].astype(bf16)
    rows = lambda b, i: (b, i, 0)
    return pl.pallas_call(
        _mlp_kernel,
        grid=(B, S // MLP_ROWS),
        in_specs=[
            pl.BlockSpec((1, MLP_ROWS, D), rows),
            pl.BlockSpec((1, MLP_ROWS, ATTN_W), rows),
            pl.BlockSpec((1, MLP_ROWS, CONV_CH), rows),
            _const((ATTN_W, D)),
            _const((CONV_CH, D)),
            _const((1, D)),
            _const((D, D_FF)),
            _const((D_FF, D)),
            _const((1, D)),
        ],
        out_specs=pl.BlockSpec((1, MLP_ROWS, D), rows),
        out_shape=jax.ShapeDtypeStruct((B, S, D), x.dtype),
        compiler_params=pltpu.CompilerParams(
            dimension_semantics=("arbitrary", "arbitrary"), vmem_limit_bytes=VMEM_LIMIT),
        name="out_mlp",
    )(x, attn, conv, wo[:ATTN_W], wo[ATTN_W:], mlp_norm_g, w_up[0].astype(bf16),
      w_down[0].astype(bf16), final_norm_g.reshape(1, D))


def kernel(x, meta_tokens, attn_norm_g, w_in, conv_w, conv_b, conv_norm_g, conv_norm_b,
           w_out, mlp_norm_g, w_up, w_down, final_norm_g):
    B, S, D = x.shape
    assert D == D_MODEL and S % MLP_ROWS == 0 and attn_norm_g.shape[0] == 1
    q, iq, misc, conv, k, v, ika, ikb = _project(
        x, meta_tokens, attn_norm_g, w_in, conv_w, conv_b, conv_norm_g, conv_norm_b)
    attn = _attend(q, iq, misc, ika, ikb, k, v)
    return _out_mlp(x, attn, conv, w_out, mlp_norm_g, w_up, w_down, final_norm_g)
```

```python
import functools

import jax
import jax.numpy as jnp
from jax import lax
from jax.experimental import pallas as pl
from jax.experimental.pallas import tpu as pltpu

D_MODEL = 1024
N_META = 16
ATTN_HEADS = 4
ATTN_HEAD_DIM = 128
ATTN_W = ATTN_HEADS * ATTN_HEAD_DIM
ROPE_THETA = 500000.0
ROPE_FRACTION = 4
IDX_HEADS = 8
IDX_HEAD_DIM = 64
INDEX_TOPK = 256
CONV_CH = 512
CONV_GROUPS = 4
CONV_WIDTH = 31
D_FF = 4 * D_MODEL
NORM_EPS = 1e-5
NEG = -1e30

LANES = 128
SUBLANES = 8
BLK = 256
STRIPS = BLK // SUBLANES
PAD_ROWS = BLK - N_META
CARRY = 32
IQ_W = IDX_HEADS * IDX_HEAD_DIM
COL_Q = 0
COL_K = COL_Q + ATTN_W
COL_V = COL_K + ATTN_W
COL_IQ = COL_V + ATTN_W
COL_GA = COL_IQ + IQ_W
COL_GG = COL_GA + CONV_CH
COL_SLAB = COL_GG + CONV_CH
W_IN_COLS = COL_SLAB + LANES
VALUE_STEPS = 24
BLIND_STEPS = 12
N_ACC = 4
TILE_UNROLL = 8
BOUND_SLACK = 1.0 + 2.0 ** -6
L_MIN = 1e-30
MLP_ROWS = 512
FF_CHUNK = 1024
VMEM_LIMIT = 60 * 1024 * 1024

_NT = (((1,), (1,)), ((), ()))


def _rms(x, g):
    return x * lax.rsqrt(jnp.mean(x * x, axis=-1, keepdims=True) + NORM_EPS) * g


def _rope(x, c, s1, s2, half):
    return x * c + pltpu.roll(x, half, 1) * s1 + pltpu.roll(x, LANES - half, 1) * s2


def _proj_kernel(x_ref, meta_ref, g_ref, w_ref, rq_ref, ri_ref, cw_ref, cb_ref, lg_ref, lb_ref,
                 q_ref, iq_ref, misc_ref, conv_ref, k_ref, v_ref, ika_ref, ikb_ref, u_scr, win_scr):
    i = pl.program_id(1)
    h = jnp.where(i == 0, meta_ref[...], x_ref[0])
    xn = _rms(h, g_ref[...]).astype(jnp.bfloat16)
    proj = jnp.dot(xn, w_ref[...], preferred_element_type=jnp.float32)

    cq, s1q, s2q = rq_ref[0], rq_ref[1], rq_ref[2]
    ci, s1i, s2i = ri_ref[0], ri_ref[1], ri_ref[2]
    q_half = ATTN_HEAD_DIM // ROPE_FRACTION // 2
    i_half = IDX_HEAD_DIM // ROPE_FRACTION // 2
    for hd in range(ATTN_HEADS):
        sl = slice(hd * LANES, (hd + 1) * LANES)
        qh = _rope(proj[:, sl], cq, s1q, s2q, q_half) * (ATTN_HEAD_DIM ** -0.5)
        q_ref[0, :, sl] = qh.astype(jnp.bfloat16)
        kh = _rope(proj[:, COL_K + hd * LANES:COL_K + (hd + 1) * LANES], cq, s1q, s2q, q_half)
        k_ref[0, :, sl] = kh.astype(jnp.bfloat16)
        ih = _rope(proj[:, COL_IQ + hd * LANES:COL_IQ + (hd + 1) * LANES], ci, s1i, s2i, i_half)
        iq_ref[0, :, sl] = (ih * (IDX_HEAD_DIM ** -0.5)).astype(jnp.bfloat16)
    v_ref[0] = proj[:, COL_V:COL_V + ATTN_W].astype(jnp.bfloat16)

    lane = lax.broadcasted_iota(jnp.int32, (BLK, LANES), 1)
    is_ik = lane < IDX_HEAD_DIM
    slab = proj[:, COL_SLAB:COL_SLAB + LANES]
    roped = _rope(slab, jnp.where(is_ik, ci, 1.0), jnp.where(is_ik, s1i, 0.0),
                  jnp.where(is_ik, s2i, 0.0), i_half)
    misc_ref[0] = jnp.where(is_ik, roped, slab * (IDX_HEADS ** -0.5))
    ika = jnp.where(is_ik, roped, 0.0)
    ika_ref[0] = ika.astype(jnp.bfloat16)
    ikb_ref[0] = pltpu.roll(ika, IDX_HEAD_DIM, 1).astype(jnp.bfloat16)

    @pl.when(i == 0)
    def _():
        u_scr[0:CARRY, :] = jnp.zeros((CARRY, CONV_CH), jnp.float32)

    ga = proj[:, COL_GA:COL_GA + CONV_CH]
    gg = proj[:, COL_GG:COL_GG + CONV_CH]
    u_scr[CARRY:CARRY + BLK, :] = ga * jax.nn.sigmoid(gg)
    y = jnp.zeros((BLK, CONV_CH), jnp.float32) + cb_ref[...]
    for b in range(SUBLANES):
        base = CARRY - (CONV_WIDTH - 1) + b
        n_a = (CONV_WIDTH - 1 - b) // SUBLANES + 1
        rows = BLK + SUBLANES * (n_a - 1)
        win_scr[0:rows, :] = u_scr[base:base + rows, :]
        for a in range(n_a):
            j = SUBLANES * a + b
            y = y + win_scr[SUBLANES * a:SUBLANES * a + BLK, :] * cw_ref[j:j + 1, :]
    u_scr[0:CARRY, :] = u_scr[BLK:BLK + CARRY, :]
    gsz = CONV_CH // CONV_GROUPS
    for gi in range(CONV_GROUPS):
        sl = slice(gi * gsz, (gi + 1) * gsz)
        yg = y[:, sl]
        mu = jnp.mean(yg, axis=-1, keepdims=True)
        d = yg - mu
        var = jnp.mean(d * d, axis=-1, keepdims=True)
        yn = d * lax.rsqrt(var + NORM_EPS) * lg_ref[:, sl] + lb_ref[:, sl]
        conv_ref[0, :, sl] = (yn * jax.nn.sigmoid(yn)).astype(jnp.bfloat16)


def _f2k(f):
    b = lax.bitcast_convert_type(f, jnp.int32)
    return b ^ ((b >> 31) & jnp.int32(0x7FFFFFFF))


def _k2f(k):
    return lax.bitcast_convert_type(k ^ ((k >> 31) & jnp.int32(0x7FFFFFFF)), jnp.float32)


def _any(mask):
    return jnp.max(jnp.where(mask, 1.0, 0.0)) > 0.5


def _tile_loop(lo, hi, fn):
    n_main = (hi - lo) // TILE_UNROLL

    def main(i, c):
        for u in range(TILE_UNROLL):
            fn(lo + TILE_UNROLL * i + u)
        return c

    lax.fori_loop(0, n_main, main, 0)
    left = (hi - lo) - TILE_UNROLL * n_main
    size = TILE_UNROLL // 2
    while size >= 1:
        start = hi - (left & (2 * size - 1))

        @pl.when((left & size) != 0)
        def _(start=start, size=size):
            for u in range(size):
                fn(start + u)

        size //= 2


def _rows8(x):
    t = jnp.sum(jnp.sum(x, axis=0), axis=0, keepdims=True)
    return jnp.broadcast_to(t, (SUBLANES, BLK))


def _dsa_kernel(q_ref, iq_ref, misc_ref, ika_ref, ikb_ref, k_ref, v_ref, o_ref,
                sc_ref, tb_ref, tau_ref, mx_ref, mn_ref, ge_ref, gt_ref, kn_ref, mb_ref, ls_ref, acc_ref,
                m_ref, l_ref, *, n_rows_total):
    f32, bf16 = jnp.float32, jnp.bfloat16
    g = pl.program_id(1)
    nt = g + 2
    q0 = (g + 1) * BLK
    ones = jnp.ones((LANES, LANES), bf16)
    shape3 = (STRIPS, SUBLANES, BLK)
    keyi = (lax.broadcasted_iota(jnp.int32, shape3, 0) * SUBLANES
            + lax.broadcasted_iota(jnp.int32, shape3, 1))
    qryi = lax.broadcasted_iota(jnp.int32, shape3, 2)

    misc_t = misc_ref[0].T
    w8 = [jnp.broadcast_to(misc_t[IDX_HEAD_DIM + hd:IDX_HEAD_DIM + hd + 1, :], (SUBLANES, BLK))
          for hd in range(IDX_HEADS)]

    def partial(hit):
        return jnp.sum(jnp.where(hit, 1.0, 0.0).reshape(STRIPS // N_ACC, N_ACC, SUBLANES, BLK), axis=0)

    def score_tile(j):
        r = pl.multiple_of(j * BLK, BLK)
        ka = ika_ref[0, pl.ds(r, BLK), :]
        kb = ikb_ref[0, pl.ds(r, BLK), :]
        acc = jnp.zeros(shape3, f32)
        for p in range(IDX_HEADS // 2):
            slab = iq_ref[0, :, p * LANES:(p + 1) * LANES]
            le = lax.dot_general(ka, slab, _NT, preferred_element_type=f32).reshape(shape3)
            lo = lax.dot_general(kb, slab, _NT, preferred_element_type=f32).reshape(shape3)
            acc = (acc + jnp.maximum(le, 0.0) * w8[2 * p][None]
                   + jnp.maximum(lo, 0.0) * w8[2 * p + 1][None])
        return acc

    def put_tile(j, s, valid):
        lo = s if valid is None else jnp.where(valid, s, -jnp.inf)
        hi = s if valid is None else jnp.where(valid, s, jnp.inf)
        sc_ref[j] = lo
        mx_ref[...] = jnp.maximum(mx_ref[...], jnp.max(lo, axis=0))
        mn_ref[...] = jnp.minimum(mn_ref[...], jnp.min(hi, axis=0))
        ge_ref[...] += partial(lo >= 0.0)
        gt_ref[...] += partial(lo > 0.0)

    mx_ref[...] = jnp.full((SUBLANES, BLK), -jnp.inf, f32)
    mn_ref[...] = jnp.full((SUBLANES, BLK), jnp.inf, f32)
    ge_ref[...] = jnp.zeros(ge_ref.shape, f32)
    gt_ref[...] = jnp.zeros(gt_ref.shape, f32)
    put_tile(0, score_tile(0), keyi >= PAD_ROWS)
    put_tile(nt - 1, score_tile(nt - 1), keyi <= qryi)

    _tile_loop(1, nt - 1, lambda j: put_tile(j, score_tile(j), None))

    kf = float(INDEX_TOPK)
    qrow = q0 + lax.broadcasted_iota(jnp.int32, (SUBLANES, BLK), 1)
    n_valid = (qrow - (PAD_ROWS - 1)).astype(f32)
    n_virtual = (n_rows_total - 1 - qrow).astype(f32)
    rowmax = jnp.broadcast_to(jnp.max(mx_ref[...], axis=0, keepdims=True), (SUBLANES, BLK))
    rowmin = jnp.broadcast_to(jnp.min(mn_ref[...], axis=0, keepdims=True), (SUBLANES, BLK))
    zacc = jnp.zeros((N_ACC, SUBLANES, BLK), f32)

    def bump(acc, hit):
        hit = hit.reshape(STRIPS // N_ACC, N_ACC, SUBLANES, BLK)
        for s in range(STRIPS // N_ACC):
            acc = jnp.where(hit[s], acc + 1.0, acc)
        return acc

    def count_ge(cand):
        def pair(i, acc):
            acc = bump(acc, sc_ref[2 * i] >= cand[None])
            return bump(acc, sc_ref[2 * i + 1] >= cand[None])

        def single(j, acc):
            return bump(acc, sc_ref[j] >= cand[None])

        acc = lax.fori_loop(0, nt // 2, pair, zacc)
        return _rows8(lax.fori_loop(2 * (nt // 2), nt, single, acc))

    def total_of(c_proc, cand):
        return c_proc + jnp.where(cand <= NEG, n_virtual, 0.0)

    def active_of(klo, khi, clo):
        return (total_of(clo, _k2f(klo)) != kf) & (khi > klo + 1)

    kneg = _f2k(jnp.full((SUBLANES, BLK), NEG, f32))
    wide = n_valid > kf
    klo = jnp.where(wide, _f2k(jnp.maximum(rowmin, NEG)), kneg)
    khi = jnp.where(wide, _f2k(rowmax) + 1, klo + 1)
    clo = n_valid
    chi = jnp.zeros((SUBLANES, BLK), f32)

    c_ge0, c_gt0 = _rows8(ge_ref[...]), _rows8(gt_ref[...])
    kzero = _f2k(jnp.zeros((SUBLANES, BLK), f32))
    inside = (klo < kzero) & (kzero < khi)
    up = inside & (c_ge0 >= kf)
    dn = inside & (c_ge0 < kf)
    pin = up & (c_gt0 < kf)
    klo = jnp.where(up, kzero, klo)
    clo = jnp.where(up, c_ge0, clo)
    khi = jnp.where(dn, kzero, jnp.where(pin, kzero + 1, khi))
    chi = jnp.where(dn, c_ge0, jnp.where(pin, c_gt0, chi))

    def narrow(it, klo, khi, clo, chi):
        lo_f, hi_f = _k2f(klo), _k2f(khi)
        k_val = _f2k(lo_f + (hi_f - lo_f) * 0.5)
        k_mid = (klo >> 1) + (khi >> 1) + (klo & khi & 1)
        kt = jnp.where(it < VALUE_STEPS, k_val, k_mid)
        kt = jnp.minimum(jnp.maximum(kt, klo + 1), khi - 1)
        cand = _k2f(kt)
        c_proc = count_ge(cand)
        ge = total_of(c_proc, cand) >= kf
        act = active_of(klo, khi, clo)
        up = act & ge
        dn = act & jnp.logical_not(ge)
        return (jnp.where(up, kt, klo), jnp.where(dn, kt, khi),
                jnp.where(up, c_proc, clo), jnp.where(dn, c_proc, chi))

    klo, khi, clo, chi = lax.fori_loop(
        0, BLIND_STEPS, lambda it, st: narrow(it, *st), (klo, khi, clo, chi))

    def rows_left(klo, khi, clo):
        return jnp.max(jnp.where(active_of(klo, khi, clo), 1.0, 0.0))

    def search_step(st):
        _, it, klo, khi, clo, chi = st
        klo, khi, clo, chi = narrow(it, klo, khi, clo, chi)
        return rows_left(klo, khi, clo), it + 1, klo, khi, clo, chi

    _, _, klo, khi, clo, chi = lax.while_loop(
        lambda st: st[0] > 0.5, search_step,
        (rows_left(klo, khi, clo), jnp.int32(BLIND_STEPS), klo, khi, clo, chi))
    tau = _k2f(klo)
    tau_ref[...] = tau

    tied = clo > kf

    @pl.when(_any(tied))
    def _():
        need = kf - chi

        def scan(j, st):
            before, cross = st
            tb_ref[j] = before
            after = before + _rows8(partial(sc_ref[j] == tau[None]))
            here = (before < need) & (need <= after)
            return after, jnp.where(here, j.astype(f32), cross)

        zero8 = jnp.zeros((SUBLANES, BLK), f32)
        _, cross = lax.fori_loop(0, nt, scan, (zero8, zero8))
        j_lo = jnp.min(jnp.where(tied, cross, float(n_rows_total // BLK))).astype(jnp.int32)
        j_hi = jnp.max(jnp.where(tied, cross, 0.0)).astype(jnp.int32)
        need1 = need[0:1, :]
        tied1 = tied[0:1, :]
        tau1 = tau[0:1, :]
        tri = (lax.broadcasted_iota(jnp.int32, (BLK, BLK), 0)
               >= lax.broadcasted_iota(jnp.int32, (BLK, BLK), 1)).astype(bf16)

        def fix(j, c):
            x = sc_ref[j].reshape(BLK, BLK)
            eq = x == tau1
            rank = jnp.dot(tri, jnp.where(eq, 1.0, 0.0).astype(bf16),
                           preferred_element_type=f32) + tb_ref[j][0:1, :]
            cut = eq & (rank > need1) & tied1
            sc_ref[j] = jnp.where(cut, -jnp.inf, x).reshape(shape3)
            return c

        lax.fori_loop(j_lo, j_hi + 1, fix, 0)

        tau_tied = jnp.where(tied, tau, jnp.nan)[None]

        def drop(j, c):
            x = sc_ref[j]
            sc_ref[j] = jnp.where(x == tau_tied, -jnp.inf, x)
            return c

        lax.fori_loop(j_hi + 1, nt, drop, 0)

    @pl.when(g == 0)
    def _():
        kn_ref[...] = jnp.zeros(kn_ref.shape, f32)

        def norms(j, c):
            r = pl.multiple_of(j * BLK, BLK)
            for hd in range(ATTN_HEADS):
                kh = k_ref[0, pl.ds(r, BLK), hd * LANES:(hd + 1) * LANES].astype(f32)
                n2 = jnp.dot((kh * kh).astype(bf16), ones, preferred_element_type=f32)
                kn_ref[hd] = jnp.maximum(kn_ref[hd], jnp.max(n2, axis=0, keepdims=True))
            return c

        lax.fori_loop(0, n_rows_total // BLK, norms, 0)

    for hd in range(ATTN_HEADS):
        qh = q_ref[0, :, hd * LANES:(hd + 1) * LANES].astype(f32)
        qn2 = jnp.dot((qh * qh).astype(bf16), ones, preferred_element_type=f32)
        mb_ref[hd] = jnp.sqrt(qn2 * kn_ref[hd]) * BOUND_SLACK
    ls_ref[...] = jnp.zeros(ls_ref.shape, f32)
    acc_ref[...] = jnp.zeros(acc_ref.shape, f32)

    def bias_tile(j):
        sel = sc_ref[j] >= tau_ref[...][None]
        return jnp.where(sel, 0.0, NEG).reshape(BLK, BLK).T

    def attend_tile(j):
        r = pl.multiple_of(j * BLK, BLK)
        bias = bias_tile(j)
        b0 = bias[:, :LANES]
        b1 = bias[:, LANES:]
        for hd in range(ATTN_HEADS):
            sl = slice(hd * LANES, (hd + 1) * LANES)
            kh = k_ref[0, pl.ds(r, BLK), sl]
            vh = v_ref[0, pl.ds(r, BLK), sl]
            s = lax.dot_general(q_ref[0, :, sl], kh, _NT, preferred_element_type=f32)
            mb = mb_ref[hd]
            p0 = jnp.exp(s[:, :LANES] - mb + b0)
            p1 = jnp.exp(s[:, LANES:] - mb + b1)
            ls_ref[hd] += p0 + p1
            p = jnp.concatenate([p0, p1], axis=1).astype(bf16)
            acc_ref[hd] += jnp.dot(p, vh, preferred_element_type=f32)

    _tile_loop(0, nt, attend_tile)
    bad = jnp.float32(0.0)
    for hd in range(ATTN_HEADS):
        l = jnp.sum(ls_ref[hd], axis=1, keepdims=True)
        bad = jnp.maximum(bad, jnp.max(jnp.where(l >= L_MIN, 0.0, 1.0)))
        o_ref[0, :, hd * LANES:(hd + 1) * LANES] = (acc_ref[hd] / l).astype(bf16)

    @pl.when(bad > 0.5)
    def _():
        m_ref[...] = jnp.full(m_ref.shape, NEG, f32)
        l_ref[...] = jnp.zeros(l_ref.shape, f32)
        acc_ref[...] = jnp.zeros(acc_ref.shape, f32)

        def attend_online(j, c):
            r = pl.multiple_of(j * BLK, BLK)
            bias = bias_tile(j)
            for hd in range(ATTN_HEADS):
                sl = slice(hd * LANES, (hd + 1) * LANES)
                kh = k_ref[0, pl.ds(r, BLK), sl]
                vh = v_ref[0, pl.ds(r, BLK), sl]
                s = lax.dot_general(q_ref[0, :, sl], kh, _NT, preferred_element_type=f32) + bias
                m_old = m_ref[hd]
                m_new = jnp.maximum(m_old, jnp.max(s, axis=1, keepdims=True))
                p = jnp.exp(s - m_new)
                alpha = jnp.exp(m_old - m_new)
                l_ref[hd] = alpha * l_ref[hd] + jnp.sum(p, axis=1, keepdims=True)
                acc_ref[hd] = alpha * acc_ref[hd] + jnp.dot(p.astype(bf16), vh,
                                                            preferred_element_type=f32)
                m_ref[hd] = m_new
            return c

        lax.fori_loop(0, nt, attend_online, 0)
        for hd in range(ATTN_HEADS):
            o_ref[0, :, hd * LANES:(hd + 1) * LANES] = (acc_ref[hd] / l_ref[hd]).astype(bf16)


def _mlp_kernel(x_ref, a_ref, c_ref, woa_ref, woc_ref, g2_ref, wu_ref, wd_ref, gf_ref, o_ref):
    h = (x_ref[0]
         + jnp.dot(a_ref[0], woa_ref[...], preferred_element_type=jnp.float32)
         + jnp.dot(c_ref[0], woc_ref[...], preferred_element_type=jnp.float32))
    hn = _rms(h, g2_ref[...]).astype(jnp.bfloat16)
    acc = jnp.zeros_like(h)
    for c in range(D_FF // FF_CHUNK):
        sl = slice(c * FF_CHUNK, (c + 1) * FF_CHUNK)
        u = jnp.dot(hn, wu_ref[:, sl], preferred_element_type=jnp.float32)
        u = jnp.square(jnp.maximum(u, 0.0)).astype(jnp.bfloat16)
        acc = acc + jnp.dot(u, wd_ref[sl, :], preferred_element_type=jnp.float32)
    o_ref[0] = _rms(h + acc, gf_ref[...])


def _rope_tables(n_rows, head_dim, lanes_per_head):
    rot = head_dim // ROPE_FRACTION
    half = rot // 2
    inv = ROPE_THETA ** (-jnp.arange(0, rot, 2, dtype=jnp.float32) / rot)
    pos = jnp.arange(n_rows, dtype=jnp.float32) - float(PAD_ROWS)
    ang = pos[:, None] * inv[None, :]
    cos, sin = jnp.cos(ang), jnp.sin(ang)
    rest = lanes_per_head - rot
    one = jnp.ones((n_rows, rest), jnp.float32)
    zero = jnp.zeros((n_rows, rest), jnp.float32)
    zh = jnp.zeros((n_rows, half), jnp.float32)
    c = jnp.concatenate([cos, cos, one], axis=1)
    s1 = jnp.concatenate([zh, sin, zero], axis=1)
    s2 = jnp.concatenate([-sin, zh, zero], axis=1)
    reps = LANES // lanes_per_head
    return jnp.stack([jnp.tile(t, (1, reps)) for t in (c, s1, s2)])


def _const(shape):
    return pl.BlockSpec(shape, lambda b, i: (0,) * len(shape), pipeline_mode=pl.Buffered(1))


def _project(x, meta_tokens, attn_norm_g, w_in, conv_w, conv_b, conv_norm_g, conv_norm_b):
    B, S, D = x.shape
    nb = S // BLK
    T = S + BLK
    f32, bf16 = jnp.float32, jnp.bfloat16

    w = w_in[0]
    n_qkvi = COL_GA
    n_small = IDX_HEAD_DIM + IDX_HEADS
    w_rep = jnp.concatenate(
        [w[:, :n_qkvi], w[:, n_qkvi + n_small:], w[:, n_qkvi:n_qkvi + n_small],
         jnp.zeros((D, LANES - n_small), f32)], axis=1).astype(bf16)
    assert w_rep.shape[1] == W_IN_COLS
    meta_pad = jnp.concatenate([jnp.zeros((PAD_ROWS, D), f32), meta_tokens.astype(f32)], axis=0)
    rq = _rope_tables(T, ATTN_HEAD_DIM, LANES)
    ri = _rope_tables(T, IDX_HEAD_DIM, IDX_HEAD_DIM)

    real = lambda b, i: (b, jnp.maximum(i - 1, 0), 0)
    padded = lambda b, i: (b, i, 0)
    return pl.pallas_call(
        _proj_kernel,
        grid=(B, nb + 1),
        in_specs=[
            pl.BlockSpec((1, BLK, D), real),
            _const((BLK, D)),
            _const((1, D)),
            _const((D, W_IN_COLS)),
            pl.BlockSpec((3, BLK, LANES), lambda b, i: (0, i, 0)),
            pl.BlockSpec((3, BLK, LANES), lambda b, i: (0, i, 0)),
            _const((CONV_WIDTH, CONV_CH)),
            _const((1, CONV_CH)),
            _const((1, CONV_CH)),
            _const((1, CONV_CH)),
        ],
        out_specs=[
            pl.BlockSpec((1, BLK, ATTN_W), real),
            pl.BlockSpec((1, BLK, ATTN_W), real),
            pl.BlockSpec((1, BLK, LANES), real),
            pl.BlockSpec((1, BLK, CONV_CH), real),
            pl.BlockSpec((1, BLK, ATTN_W), padded),
            pl.BlockSpec((1, BLK, ATTN_W), padded),
            pl.BlockSpec((1, BLK, LANES), padded),
            pl.BlockSpec((1, BLK, LANES), padded),
        ],
        out_shape=[
            jax.ShapeDtypeStruct((B, S, ATTN_W), bf16),
            jax.ShapeDtypeStruct((B, S, ATTN_W), bf16),
            jax.ShapeDtypeStruct((B, S, LANES), f32),
            jax.ShapeDtypeStruct((B, S, CONV_CH), bf16),
            jax.ShapeDtypeStruct((B, T, ATTN_W), bf16),
            jax.ShapeDtypeStruct((B, T, ATTN_W), bf16),
            jax.ShapeDtypeStruct((B, T, LANES), bf16),
            jax.ShapeDtypeStruct((B, T, LANES), bf16),
        ],
        scratch_shapes=[pltpu.VMEM((CARRY + BLK, CONV_CH), f32),
                        pltpu.VMEM((CARRY + BLK, CONV_CH), f32)],
        compiler_params=pltpu.CompilerParams(
            dimension_semantics=("arbitrary", "arbitrary"), vmem_limit_bytes=VMEM_LIMIT),
        name="proj_rope_conv",
    )(x, meta_pad, attn_norm_g, w_rep, rq, ri, conv_w[0], conv_b, conv_norm_g, conv_norm_b)


def _attend(q, iq, misc, ika, ikb, k, v):
    B, S, _ = q.shape
    T = k.shape[1]
    nb = S // BLK
    f32 = jnp.float32
    blk = lambda b, g: (b, g, 0)
    whole = lambda width: pl.BlockSpec((1, T, width), lambda b, g: (b, 0, 0),
                                       pipeline_mode=pl.Buffered(1))
    return pl.pallas_call(
        functools.partial(_dsa_kernel, n_rows_total=T),
        grid=(B, nb),
        in_specs=[
            pl.BlockSpec((1, BLK, ATTN_W), blk),
            pl.BlockSpec((1, BLK, ATTN_W), blk),
            pl.BlockSpec((1, BLK, LANES), blk),
            whole(LANES), whole(LANES), whole(ATTN_W), whole(ATTN_W),
        ],
        out_specs=pl.BlockSpec((1, BLK, ATTN_W), blk),
        out_shape=jax.ShapeDtypeStruct((B, S, ATTN_W), jnp.bfloat16),
        scratch_shapes=[
            pltpu.VMEM((nb + 1, STRIPS, SUBLANES, BLK), f32),
            pltpu.VMEM((nb + 1, SUBLANES, BLK), f32),
            pltpu.VMEM((SUBLANES, BLK), f32),
            pltpu.VMEM((SUBLANES, BLK), f32),
            pltpu.VMEM((SUBLANES, BLK), f32),
            pltpu.VMEM((N_ACC, SUBLANES, BLK), f32),
            pltpu.VMEM((N_ACC, SUBLANES, BLK), f32),
            pltpu.VMEM((ATTN_HEADS, 1, LANES), f32),
            pltpu.VMEM((ATTN_HEADS, BLK, LANES), f32),
            pltpu.VMEM((ATTN_HEADS, BLK, LANES), f32),
            pltpu.VMEM((ATTN_HEADS, BLK, ATTN_HEAD_DIM), f32),
            pltpu.VMEM((ATTN_HEADS, BLK, 1), f32),
            pltpu.VMEM((ATTN_HEADS, BLK, 1), f32),
        ],
        compiler_params=pltpu.CompilerParams(
            dimension_semantics=("arbitrary", "arbitrary"), vmem_limit_bytes=VMEM_LIMIT),
        name="dsa_attention",
    )(q, iq, misc, ika, ikb, k, v)


def _out_mlp(x, attn, conv, w_out, mlp_norm_g, w_up, w_down, final_norm_g):
    B, S, D = x.shape
    bf16 = jnp.bfloat16
    wo = w_out[0].astype(bf16)
    rows = lambda b, i: (b, i, 0)
    return pl.pallas_call(
        _mlp_kernel,
        grid=(B, S // MLP_ROWS),
        in_specs=[
            pl.BlockSpec((1, MLP_ROWS, D), rows),
            pl.BlockSpec((1, MLP_ROWS, ATTN_W), rows),
            pl.BlockSpec((1, MLP_ROWS, CONV_CH), rows),
            _const((ATTN_W, D)),
            _const((CONV_CH, D)),
            _const((1, D)),
            _const((D, D_FF)),
            _const((D_FF, D)),
            _const((1, D)),
        ],
        out_specs=pl.BlockSpec((1, MLP_ROWS, D), rows),
        out_shape=jax.ShapeDtypeStruct((B, S, D), x.dtype),
        compiler_params=pltpu.CompilerParams(
            dimension_semantics=("arbitrary", "arbitrary"), vmem_limit_bytes=VMEM_LIMIT),
        name="out_mlp",
    )(x, attn, conv, wo[:ATTN_W], wo[ATTN_W:], mlp_norm_g, w_up[0].astype(bf16),
      w_down[0].astype(bf16), final_norm_g.reshape(1, D))


def kernel(x, meta_tokens, attn_norm_g, w_in, conv_w, conv_b, conv_norm_g, conv_norm_b,
           w_out, mlp_norm_g, w_up, w_down, final_norm_g):
    B, S, D = x.shape
    assert D == D_MODEL and S % MLP_ROWS == 0 and attn_norm_g.shape[0] == 1
    q, iq, misc, conv, k, v, ika, ikb = _project(
        x, meta_tokens, attn_norm_g, w_in, conv_w, conv_b, conv_norm_g, conv_norm_b)
    attn = _attend(q, iq, misc, ika, ikb, k, v)
    return _out_mlp(x, attn, conv, w_out, mlp_norm_g, w_up, w_down, final_norm_g)
```

```python
import functools

import jax
import jax.numpy as jnp
from jax import lax
from jax.experimental import pallas as pl
from jax.experimental.pallas import tpu as pltpu

D_MODEL = 1024
N_META = 16
ATTN_HEADS = 4
ATTN_HEAD_DIM = 128
ATTN_W = ATTN_HEADS * ATTN_HEAD_DIM
ROPE_THETA = 500000.0
ROPE_FRACTION = 4
IDX_HEADS = 8
IDX_HEAD_DIM = 64
INDEX_TOPK = 256
CONV_CH = 512
CONV_GROUPS = 4
CONV_WIDTH = 31
D_FF = 4 * D_MODEL
NORM_EPS = 1e-5
NEG = -1e30

LANES = 128
SUBLANES = 8
BLK = 256
QBLK = 2 * BLK
STRIPS = BLK // SUBLANES
PAD_ROWS = BLK - N_META
CARRY = 32
IQ_W = IDX_HEADS * IDX_HEAD_DIM
COL_Q = 0
COL_K = COL_Q + ATTN_W
COL_V = COL_K + ATTN_W
COL_IQ = COL_V + ATTN_W
COL_GA = COL_IQ + IQ_W
COL_GG = COL_GA + CONV_CH
COL_SLAB = COL_GG + CONV_CH
W_IN_COLS = COL_SLAB + LANES
VALUE_STEPS = 24
BLIND_STEPS = 12
N_ACC = 4
TILE_UNROLL = 4
BOUND_SLACK = 1.0 + 2.0 ** -6
L_MIN = 1e-30
MLP_ROWS = 512
FF_CHUNK = 1024
VMEM_LIMIT = 60 * 1024 * 1024

_NT = (((1,), (1,)), ((), ()))


def _rms(x, g):
    return x * lax.rsqrt(jnp.mean(x * x, axis=-1, keepdims=True) + NORM_EPS) * g


def _rope(x, c, s1, s2, half):
    return x * c + pltpu.roll(x, half, 1) * s1 + pltpu.roll(x, LANES - half, 1) * s2


def _proj_kernel(x_ref, meta_ref, g_ref, w_ref, rq_ref, ri_ref, cw_ref, cb_ref, lg_ref, lb_ref,
                 q_ref, iq_ref, misc_ref, conv_ref, k_ref, v_ref, ika_ref, ikb_ref, u_scr, win_scr):
    i = pl.program_id(1)
    h = jnp.where(i == 0, meta_ref[...], x_ref[0])
    xn = _rms(h, g_ref[...]).astype(jnp.bfloat16)
    proj = jnp.dot(xn, w_ref[...], preferred_element_type=jnp.float32)

    cq, s1q, s2q = rq_ref[0], rq_ref[1], rq_ref[2]
    ci, s1i, s2i = ri_ref[0], ri_ref[1], ri_ref[2]
    q_half = ATTN_HEAD_DIM // ROPE_FRACTION // 2
    i_half = IDX_HEAD_DIM // ROPE_FRACTION // 2
    for hd in range(ATTN_HEADS):
        sl = slice(hd * LANES, (hd + 1) * LANES)
        qh = _rope(proj[:, sl], cq, s1q, s2q, q_half) * (ATTN_HEAD_DIM ** -0.5)
        q_ref[0, :, sl] = qh.astype(jnp.bfloat16)
        kh = _rope(proj[:, COL_K + hd * LANES:COL_K + (hd + 1) * LANES], cq, s1q, s2q, q_half)
        k_ref[0, :, sl] = kh.astype(jnp.bfloat16)
        ih = _rope(proj[:, COL_IQ + hd * LANES:COL_IQ + (hd + 1) * LANES], ci, s1i, s2i, i_half)
        iq_ref[0, :, sl] = (ih * (IDX_HEAD_DIM ** -0.5)).astype(jnp.bfloat16)
    v_ref[0] = proj[:, COL_V:COL_V + ATTN_W].astype(jnp.bfloat16)

    lane = lax.broadcasted_iota(jnp.int32, (BLK, LANES), 1)
    is_ik = lane < IDX_HEAD_DIM
    slab = proj[:, COL_SLAB:COL_SLAB + LANES]
    roped = _rope(slab, jnp.where(is_ik, ci, 1.0), jnp.where(is_ik, s1i, 0.0),
                  jnp.where(is_ik, s2i, 0.0), i_half)
    misc_ref[0] = jnp.where(is_ik, roped, slab * (IDX_HEADS ** -0.5))
    ika = jnp.where(is_ik, roped, 0.0)
    ika_ref[0] = ika.astype(jnp.bfloat16)
    ikb_ref[0] = pltpu.roll(ika, IDX_HEAD_DIM, 1).astype(jnp.bfloat16)

    @pl.when(i == 0)
    def _():
        u_scr[0:CARRY, :] = jnp.zeros((CARRY, CONV_CH), jnp.float32)

    ga = proj[:, COL_GA:COL_GA + CONV_CH]
    gg = proj[:, COL_GG:COL_GG + CONV_CH]
    u_scr[CARRY:CARRY + BLK, :] = ga * jax.nn.sigmoid(gg)
    y = jnp.zeros((BLK, CONV_CH), jnp.float32) + cb_ref[...]
    for b in range(SUBLANES):
        base = CARRY - (CONV_WIDTH - 1) + b
        n_a = (CONV_WIDTH - 1 - b) // SUBLANES + 1
        rows = BLK + SUBLANES * (n_a - 1)
        win_scr[0:rows, :] = u_scr[base:base + rows, :]
        for a in range(n_a):
            j = SUBLANES * a + b
            y = y + win_scr[SUBLANES * a:SUBLANES * a + BLK, :] * cw_ref[j:j + 1, :]
    u_scr[0:CARRY, :] = u_scr[BLK:BLK + CARRY, :]
    gsz = CONV_CH // CONV_GROUPS
    for gi in range(CONV_GROUPS):
        sl = slice(gi * gsz, (gi + 1) * gsz)
        yg = y[:, sl]
        mu = jnp.mean(yg, axis=-1, keepdims=True)
        d = yg - mu
        var = jnp.mean(d * d, axis=-1, keepdims=True)
        yn = d * lax.rsqrt(var + NORM_EPS) * lg_ref[:, sl] + lb_ref[:, sl]
        conv_ref[0, :, sl] = (yn * jax.nn.sigmoid(yn)).astype(jnp.bfloat16)


def _f2k(f):
    b = lax.bitcast_convert_type(f, jnp.int32)
    return b ^ ((b >> 31) & jnp.int32(0x7FFFFFFF))


def _k2f(k):
    return lax.bitcast_convert_type(k ^ ((k >> 31) & jnp.int32(0x7FFFFFFF)), jnp.float32)


def _any(mask):
    return jnp.max(jnp.where(mask, 1.0, 0.0)) > 0.5


def _tile_loop(lo, hi, fn):
    n_main = (hi - lo) // TILE_UNROLL

    def main(i, c):
        for u in range(TILE_UNROLL):
            fn(lo + TILE_UNROLL * i + u)
        return c

    lax.fori_loop(0, n_main, main, 0)
    left = (hi - lo) - TILE_UNROLL * n_main
    size = TILE_UNROLL // 2
    while size >= 1:
        start = hi - (left & (2 * size - 1))

        @pl.when((left & size) != 0)
        def _(start=start, size=size):
            for u in range(size):
                fn(start + u)

        size //= 2


def _rows8(x):
    t = jnp.sum(jnp.sum(x, axis=0), axis=0, keepdims=True)
    return jnp.broadcast_to(t, (SUBLANES, QBLK))


def _dsa_kernel(q_ref, iq_ref, misc_ref, ika_ref, ikb_ref, k_ref, v_ref, o_ref,
                sc_ref, tb_ref, tau_ref, mx_ref, mn_ref, ge_ref, gt_ref, kn_ref, mb_ref, ls_ref, acc_ref,
                m_ref, l_ref, *, n_rows_total):
    f32, bf16 = jnp.float32, jnp.bfloat16
    g = pl.program_id(1)
    nt = 2 * g + 3
    q0 = BLK + g * QBLK
    ones = jnp.ones((LANES, LANES), bf16)
    shape3 = (STRIPS, SUBLANES, QBLK)
    keyi = (lax.broadcasted_iota(jnp.int32, shape3, 0) * SUBLANES
            + lax.broadcasted_iota(jnp.int32, shape3, 1))
    qryi = lax.broadcasted_iota(jnp.int32, shape3, 2)

    misc_t = misc_ref[0].T
    w8 = [jnp.broadcast_to(misc_t[IDX_HEAD_DIM + hd:IDX_HEAD_DIM + hd + 1, :], (SUBLANES, QBLK))
          for hd in range(IDX_HEADS)]

    def partial(hit):
        return jnp.sum(jnp.where(hit, 1.0, 0.0).reshape(STRIPS // N_ACC, N_ACC, SUBLANES, QBLK), axis=0)

    def score_tile(j):
        r = pl.multiple_of(j * BLK, BLK)
        ka = ika_ref[0, pl.ds(r, BLK), :]
        kb = ikb_ref[0, pl.ds(r, BLK), :]
        acc = jnp.zeros(shape3, f32)
        for p in range(IDX_HEADS // 2):
            slab = iq_ref[0, :, p * LANES:(p + 1) * LANES]
            le = lax.dot_general(ka, slab, _NT, preferred_element_type=f32).reshape(shape3)
            lo = lax.dot_general(kb, slab, _NT, preferred_element_type=f32).reshape(shape3)
            acc = (acc + jnp.maximum(le, 0.0) * w8[2 * p][None]
                   + jnp.maximum(lo, 0.0) * w8[2 * p + 1][None])
        return acc

    def put_tile(j, s, valid):
        lo = s if valid is None else jnp.where(valid, s, -jnp.inf)
        hi = s if valid is None else jnp.where(valid, s, jnp.inf)
        sc_ref[j] = lo
        mx_ref[...] = jnp.maximum(mx_ref[...], jnp.max(lo, axis=0))
        mn_ref[...] = jnp.minimum(mn_ref[...], jnp.min(hi, axis=0))
        ge_ref[...] += partial(lo >= 0.0)
        gt_ref[...] += partial(lo > 0.0)

    mx_ref[...] = jnp.full((SUBLANES, QBLK), -jnp.inf, f32)
    mn_ref[...] = jnp.full((SUBLANES, QBLK), jnp.inf, f32)
    ge_ref[...] = jnp.zeros(ge_ref.shape, f32)
    gt_ref[...] = jnp.zeros(gt_ref.shape, f32)
    put_tile(0, score_tile(0), keyi >= PAD_ROWS)
    put_tile(nt - 2, score_tile(nt - 2), keyi <= qryi)
    put_tile(nt - 1, score_tile(nt - 1), keyi + BLK <= qryi)

    _tile_loop(1, nt - 2, lambda j: put_tile(j, score_tile(j), None))

    kf = float(INDEX_TOPK)
    qrow = q0 + lax.broadcasted_iota(jnp.int32, (SUBLANES, QBLK), 1)
    n_valid = (qrow - (PAD_ROWS - 1)).astype(f32)
    n_virtual = (n_rows_total - 1 - qrow).astype(f32)
    rowmax = jnp.broadcast_to(jnp.max(mx_ref[...], axis=0, keepdims=True), (SUBLANES, QBLK))
    rowmin = jnp.broadcast_to(jnp.min(mn_ref[...], axis=0, keepdims=True), (SUBLANES, QBLK))
    zacc = jnp.zeros((N_ACC, SUBLANES, QBLK), f32)

    def bump(acc, hit):
        hit = hit.reshape(STRIPS // N_ACC, N_ACC, SUBLANES, QBLK)
        for s in range(STRIPS // N_ACC):
            acc = jnp.where(hit[s], acc + 1.0, acc)
        return acc

    def count_ge(cand):
        def pair(i, acc):
            acc = bump(acc, sc_ref[2 * i] >= cand[None])
            return bump(acc, sc_ref[2 * i + 1] >= cand[None])

        def single(j, acc):
            return bump(acc, sc_ref[j] >= cand[None])

        acc = lax.fori_loop(0, nt // 2, pair, zacc)
        return _rows8(lax.fori_loop(2 * (nt // 2), nt, single, acc))

    def total_of(c_proc, cand):
        return c_proc + jnp.where(cand <= NEG, n_virtual, 0.0)

    def active_of(klo, khi, clo):
        return (total_of(clo, _k2f(klo)) != kf) & (khi > klo + 1)

    kneg = _f2k(jnp.full((SUBLANES, QBLK), NEG, f32))
    wide = n_valid > kf
    klo = jnp.where(wide, _f2k(jnp.maximum(rowmin, NEG)), kneg)
    khi = jnp.where(wide, _f2k(rowmax) + 1, klo + 1)
    clo = n_valid
    chi = jnp.zeros((SUBLANES, QBLK), f32)

    c_ge0, c_gt0 = _rows8(ge_ref[...]), _rows8(gt_ref[...])
    kzero = _f2k(jnp.zeros((SUBLANES, QBLK), f32))
    inside = (klo < kzero) & (kzero < khi)
    up = inside & (c_ge0 >= kf)
    dn = inside & (c_ge0 < kf)
    pin = up & (c_gt0 < kf)
    klo = jnp.where(up, kzero, klo)
    clo = jnp.where(up, c_ge0, clo)
    khi = jnp.where(dn, kzero, jnp.where(pin, kzero + 1, khi))
    chi = jnp.where(dn, c_ge0, jnp.where(pin, c_gt0, chi))

    def narrow(it, klo, khi, clo, chi):
        lo_f, hi_f = _k2f(klo), _k2f(khi)
        k_val = _f2k(lo_f + (hi_f - lo_f) * 0.5)
        k_mid = (klo >> 1) + (khi >> 1) + (klo & khi & 1)
        kt = jnp.where(it < VALUE_STEPS, k_val, k_mid)
        kt = jnp.minimum(jnp.maximum(kt, klo + 1), khi - 1)
        cand = _k2f(kt)
        c_proc = count_ge(cand)
        ge = total_of(c_proc, cand) >= kf
        act = active_of(klo, khi, clo)
        up = act & ge
        dn = act & jnp.logical_not(ge)
        return (jnp.where(up, kt, klo), jnp.where(dn, kt, khi),
                jnp.where(up, c_proc, clo), jnp.where(dn, c_proc, chi))

    klo, khi, clo, chi = lax.fori_loop(
        0, BLIND_STEPS, lambda it, st: narrow(it, *st), (klo, khi, clo, chi))

    def rows_left(klo, khi, clo):
        return jnp.max(jnp.where(active_of(klo, khi, clo), 1.0, 0.0))

    def search_step(st):
        _, it, klo, khi, clo, chi = st
        klo, khi, clo, chi = narrow(it, klo, khi, clo, chi)
        return rows_left(klo, khi, clo), it + 1, klo, khi, clo, chi

    _, _, klo, khi, clo, chi = lax.while_loop(
        lambda st: st[0] > 0.5, search_step,
        (rows_left(klo, khi, clo), jnp.int32(BLIND_STEPS), klo, khi, clo, chi))
    tau = _k2f(klo)
    tau_ref[...] = tau

    tied = clo > kf

    @pl.when(_any(tied))
    def _():
        need = kf - chi

        def scan(j, st):
            before, cross = st
            tb_ref[j] = before
            after = before + _rows8(partial(sc_ref[j] == tau[None]))
            here = (before < need) & (need <= after)
            return after, jnp.where(here, j.astype(f32), cross)

        zero8 = jnp.zeros((SUBLANES, QBLK), f32)
        _, cross = lax.fori_loop(0, nt, scan, (zero8, zero8))
        j_lo = jnp.min(jnp.where(tied, cross, float(n_rows_total // BLK))).astype(jnp.int32)
        j_hi = jnp.max(jnp.where(tied, cross, 0.0)).astype(jnp.int32)
        need1 = need[0:1, :]
        tied1 = tied[0:1, :]
        tau1 = tau[0:1, :]
        tri = (lax.broadcasted_iota(jnp.int32, (BLK, BLK), 0)
               >= lax.broadcasted_iota(jnp.int32, (BLK, BLK), 1)).astype(bf16)

        def fix(j, c):
            x = sc_ref[j].reshape(BLK, QBLK)
            eq = x == tau1
            rank = jnp.dot(tri, jnp.where(eq, 1.0, 0.0).astype(bf16),
                           preferred_element_type=f32) + tb_ref[j][0:1, :]
            cut = eq & (rank > need1) & tied1
            sc_ref[j] = jnp.where(cut, -jnp.inf, x).reshape(shape3)
            return c

        lax.fori_loop(j_lo, j_hi + 1, fix, 0)

        tau_tied = jnp.where(tied, tau, jnp.nan)[None]

        def drop(j, c):
            x = sc_ref[j]
            sc_ref[j] = jnp.where(x == tau_tied, -jnp.inf, x)
            return c

        lax.fori_loop(j_hi + 1, nt, drop, 0)

    @pl.when(g == 0)
    def _():
        kn_ref[...] = jnp.zeros(kn_ref.shape, f32)

        def norms(j, c):
            r = pl.multiple_of(j * BLK, BLK)
            for hd in range(ATTN_HEADS):
                kh = k_ref[0, pl.ds(r, BLK), hd * LANES:(hd + 1) * LANES].astype(f32)
                n2 = jnp.dot((kh * kh).astype(bf16), ones, preferred_element_type=f32)
                kn_ref[hd] = jnp.maximum(kn_ref[hd], jnp.max(n2, axis=0, keepdims=True))
            return c

        lax.fori_loop(0, n_rows_total // BLK, norms, 0)

    for hd in range(ATTN_HEADS):
        qh = q_ref[0, :, hd * LANES:(hd + 1) * LANES].astype(f32)
        qn2 = jnp.dot((qh * qh).astype(bf16), ones, preferred_element_type=f32)
        mb_ref[hd] = jnp.sqrt(qn2 * kn_ref[hd]) * BOUND_SLACK
    ls_ref[...] = jnp.zeros(ls_ref.shape, f32)
    acc_ref[...] = jnp.zeros(acc_ref.shape, f32)

    def bias_tile(j):
        sel = sc_ref[j] >= tau_ref[...][None]
        return jnp.where(sel, 0.0, NEG).reshape(BLK, QBLK).T

    def attend_tile(j):
        r = pl.multiple_of(j * BLK, BLK)
        bias = bias_tile(j)
        b0 = bias[:, :LANES]
        b1 = bias[:, LANES:]
        for hd in range(ATTN_HEADS):
            sl = slice(hd * LANES, (hd + 1) * LANES)
            kh = k_ref[0, pl.ds(r, BLK), sl]
            vh = v_ref[0, pl.ds(r, BLK), sl]
            s = lax.dot_general(q_ref[0, :, sl], kh, _NT, preferred_element_type=f32)
            mb = mb_ref[hd]
            p0 = jnp.exp(s[:, :LANES] - mb + b0)
            p1 = jnp.exp(s[:, LANES:] - mb + b1)
            ls_ref[hd] += p0 + p1
            p = jnp.concatenate([p0, p1], axis=1).astype(bf16)
            acc_ref[hd] += jnp.dot(p, vh, preferred_element_type=f32)

    _tile_loop(0, nt, attend_tile)
    bad = jnp.float32(0.0)
    for hd in range(ATTN_HEADS):
        l = jnp.sum(ls_ref[hd], axis=1, keepdims=True)
        bad = jnp.maximum(bad, jnp.max(jnp.where(l >= L_MIN, 0.0, 1.0)))
        o_ref[0, :, hd * LANES:(hd + 1) * LANES] = (acc_ref[hd] / l).astype(bf16)

    @pl.when(bad > 0.5)
    def _():
        m_ref[...] = jnp.full(m_ref.shape, NEG, f32)
        l_ref[...] = jnp.zeros(l_ref.shape, f32)
        acc_ref[...] = jnp.zeros(acc_ref.shape, f32)

        def attend_online(j, c):
            r = pl.multiple_of(j * BLK, BLK)
            bias = bias_tile(j)
            for hd in range(ATTN_HEADS):
                sl = slice(hd * LANES, (hd + 1) * LANES)
                kh = k_ref[0, pl.ds(r, BLK), sl]
                vh = v_ref[0, pl.ds(r, BLK), sl]
                s = lax.dot_general(q_ref[0, :, sl], kh, _NT, preferred_element_type=f32) + bias
                m_old = m_ref[hd]
                m_new = jnp.maximum(m_old, jnp.max(s, axis=1, keepdims=True))
                p = jnp.exp(s - m_new)
                alpha = jnp.exp(m_old - m_new)
                l_ref[hd] = alpha * l_ref[hd] + jnp.sum(p, axis=1, keepdims=True)
                acc_ref[hd] = alpha * acc_ref[hd] + jnp.dot(p.astype(bf16), vh,
                                                            preferred_element_type=f32)
                m_ref[hd] = m_new
            return c

        lax.fori_loop(0, nt, attend_online, 0)
        for hd in range(ATTN_HEADS):
            o_ref[0, :, hd * LANES:(hd + 1) * LANES] = (acc_ref[hd] / l_ref[hd]).astype(bf16)


def _mlp_kernel(x_ref, a_ref, c_ref, woa_ref, woc_ref, g2_ref, wu_ref, wd_ref, gf_ref, o_ref):
    h = (x_ref[0]
         + jnp.dot(a_ref[0], woa_ref[...], preferred_element_type=jnp.float32)
         + jnp.dot(c_ref[0], woc_ref[...], preferred_element_type=jnp.float32))
    hn = _rms(h, g2_ref[...]).astype(jnp.bfloat16)
    acc = jnp.zeros_like(h)
    for c in range(D_FF // FF_CHUNK):
        sl = slice(c * FF_CHUNK, (c + 1) * FF_CHUNK)
        u = jnp.dot(hn, wu_ref[:, sl], preferred_element_type=jnp.float32)
        u = jnp.square(jnp.maximum(u, 0.0)).astype(jnp.bfloat16)
        acc = acc + jnp.dot(u, wd_ref[sl, :], preferred_element_type=jnp.float32)
    o_ref[0] = _rms(h + acc, gf_ref[...])


def _rope_tables(n_rows, head_dim, lanes_per_head):
    rot = head_dim // ROPE_FRACTION
    half = rot // 2
    inv = ROPE_THETA ** (-jnp.arange(0, rot, 2, dtype=jnp.float32) / rot)
    pos = jnp.arange(n_rows, dtype=jnp.float32) - float(PAD_ROWS)
    ang = pos[:, None] * inv[None, :]
    cos, sin = jnp.cos(ang), jnp.sin(ang)
    rest = lanes_per_head - rot
    one = jnp.ones((n_rows, rest), jnp.float32)
    zero = jnp.zeros((n_rows, rest), jnp.float32)
    zh = jnp.zeros((n_rows, half), jnp.float32)
    c = jnp.concatenate([cos, cos, one], axis=1)
    s1 = jnp.concatenate([zh, sin, zero], axis=1)
    s2 = jnp.concatenate([-sin, zh, zero], axis=1)
    reps = LANES // lanes_per_head
    return jnp.stack([jnp.tile(t, (1, reps)) for t in (c, s1, s2)])


def _const(shape):
    return pl.BlockSpec(shape, lambda b, i: (0,) * len(shape), pipeline_mode=pl.Buffered(1))


def _project(x, meta_tokens, attn_norm_g, w_in, conv_w, conv_b, conv_norm_g, conv_norm_b):
    B, S, D = x.shape
    nb = S // BLK
    T = S + BLK
    f32, bf16 = jnp.float32, jnp.bfloat16

    w = w_in[0]
    n_qkvi = COL_GA
    n_small = IDX_HEAD_DIM + IDX_HEADS
    w_rep = jnp.concatenate(
        [w[:, :n_qkvi], w[:, n_qkvi + n_small:], w[:, n_qkvi:n_qkvi + n_small],
         jnp.zeros((D, LANES - n_small), f32)], axis=1).astype(bf16)
    assert w_rep.shape[1] == W_IN_COLS
    meta_pad = jnp.concatenate([jnp.zeros((PAD_ROWS, D), f32), meta_tokens.astype(f32)], axis=0)
    rq = _rope_tables(T, ATTN_HEAD_DIM, LANES)
    ri = _rope_tables(T, IDX_HEAD_DIM, IDX_HEAD_DIM)

    real = lambda b, i: (b, jnp.maximum(i - 1, 0), 0)
    padded = lambda b, i: (b, i, 0)
    return pl.pallas_call(
        _proj_kernel,
        grid=(B, nb + 1),
        in_specs=[
            pl.BlockSpec((1, BLK, D), real),
            _const((BLK, D)),
            _const((1, D)),
            _const((D, W_IN_COLS)),
            pl.BlockSpec((3, BLK, LANES), lambda b, i: (0, i, 0)),
            pl.BlockSpec((3, BLK, LANES), lambda b, i: (0, i, 0)),
            _const((CONV_WIDTH, CONV_CH)),
            _const((1, CONV_CH)),
            _const((1, CONV_CH)),
            _const((1, CONV_CH)),
        ],
        out_specs=[
            pl.BlockSpec((1, BLK, ATTN_W), real),
            pl.BlockSpec((1, BLK, ATTN_W), real),
            pl.BlockSpec((1, BLK, LANES), real),
            pl.BlockSpec((1, BLK, CONV_CH), real),
            pl.BlockSpec((1, BLK, ATTN_W), padded),
            pl.BlockSpec((1, BLK, ATTN_W), padded),
            pl.BlockSpec((1, BLK, LANES), padded),
            pl.BlockSpec((1, BLK, LANES), padded),
        ],
        out_shape=[
            jax.ShapeDtypeStruct((B, S, ATTN_W), bf16),
            jax.ShapeDtypeStruct((B, S, ATTN_W), bf16),
            jax.ShapeDtypeStruct((B, S, LANES), f32),
            jax.ShapeDtypeStruct((B, S, CONV_CH), bf16),
            jax.ShapeDtypeStruct((B, T, ATTN_W), bf16),
            jax.ShapeDtypeStruct((B, T, ATTN_W), bf16),
            jax.ShapeDtypeStruct((B, T, LANES), bf16),
            jax.ShapeDtypeStruct((B, T, LANES), bf16),
        ],
        scratch_shapes=[pltpu.VMEM((CARRY + BLK, CONV_CH), f32),
                        pltpu.VMEM((CARRY + BLK, CONV_CH), f32)],
        compiler_params=pltpu.CompilerParams(
            dimension_semantics=("arbitrary", "arbitrary"), vmem_limit_bytes=VMEM_LIMIT),
        name="proj_rope_conv",
    )(x, meta_pad, attn_norm_g, w_rep, rq, ri, conv_w[0], conv_b, conv_norm_g, conv_norm_b)


def _attend(q, iq, misc, ika, ikb, k, v):
    B, S, _ = q.shape
    T = k.shape[1]
    n_tiles = T // BLK
    f32 = jnp.float32
    blk = lambda b, g: (b, g, 0)
    whole = lambda width: pl.BlockSpec((1, T, width), lambda b, g: (b, 0, 0),
                                       pipeline_mode=pl.Buffered(1))
    return pl.pallas_call(
        functools.partial(_dsa_kernel, n_rows_total=T),
        grid=(B, S // QBLK),
        in_specs=[
            pl.BlockSpec((1, QBLK, ATTN_W), blk),
            pl.BlockSpec((1, QBLK, ATTN_W), blk),
            pl.BlockSpec((1, QBLK, LANES), blk),
            whole(LANES), whole(LANES), whole(ATTN_W), whole(ATTN_W),
        ],
        out_specs=pl.BlockSpec((1, QBLK, ATTN_W), blk),
        out_shape=jax.ShapeDtypeStruct((B, S, ATTN_W), jnp.bfloat16),
        scratch_shapes=[
            pltpu.VMEM((n_tiles, STRIPS, SUBLANES, QBLK), f32),
            pltpu.VMEM((n_tiles, SUBLANES, QBLK), f32),
            pltpu.VMEM((SUBLANES, QBLK), f32),
            pltpu.VMEM((SUBLANES, QBLK), f32),
            pltpu.VMEM((SUBLANES, QBLK), f32),
            pltpu.VMEM((N_ACC, SUBLANES, QBLK), f32),
            pltpu.VMEM((N_ACC, SUBLANES, QBLK), f32),
            pltpu.VMEM((ATTN_HEADS, 1, LANES), f32),
            pltpu.VMEM((ATTN_HEADS, QBLK, LANES), f32),
            pltpu.VMEM((ATTN_HEADS, QBLK, LANES), f32),
            pltpu.VMEM((ATTN_HEADS, QBLK, ATTN_HEAD_DIM), f32),
            pltpu.VMEM((ATTN_HEADS, QBLK, 1), f32),
            pltpu.VMEM((ATTN_HEADS, QBLK, 1), f32),
        ],
        compiler_params=pltpu.CompilerParams(
            dimension_semantics=("arbitrary", "arbitrary"), vmem_limit_bytes=VMEM_LIMIT),
        name="dsa_attention",
    )(q, iq, misc, ika, ikb, k, v)


def _out_mlp(x, attn, conv, w_out, mlp_norm_g, w_up, w_down, final_norm_g):
    B, S, D = x.shape
    bf16 = jnp.bfloat16
    wo = w_out[0].astype(bf16)
    rows = lambda b, i: (b, i, 0)
    return pl.pallas_call(
        _mlp_kernel,
        grid=(B, S // MLP_ROWS),
        in_specs=[
            pl.BlockSpec((1, MLP_ROWS, D), rows),
            pl.BlockSpec((1, MLP_ROWS, ATTN_W), rows),
            pl.BlockSpec((1, MLP_ROWS, CONV_CH), rows),
            _const((ATTN_W, D)),
            _const((CONV_CH, D)),
            _const((1, D)),
            _const((D, D_FF)),
            _const((D_FF, D)),
            _const((1, D)),
        ],
        out_specs=pl.BlockSpec((1, MLP_ROWS, D), rows),
        out_shape=jax.ShapeDtypeStruct((B, S, D), x.dtype),
        compiler_params=pltpu.CompilerParams(
            dimension_semantics=("arbitrary", "arbitrary"), vmem_limit_bytes=VMEM_LIMIT),
        name="out_mlp",
    )(x, attn, conv, wo[:ATTN_W], wo[ATTN_W:], mlp_norm_g, w_up[0].astype(bf16),
      w_down[0].astype(bf16), final_norm_g.reshape(1, D))


def kernel(x, meta_tokens, attn_norm_g, w_in, conv_w, conv_b, conv_norm_g, conv_norm_b,
           w_out, mlp_norm_g, w_up, w_down, final_norm_g):
    B, S, D = x.shape
    assert D == D_MODEL and S % MLP_ROWS == 0 and S % QBLK == 0 and attn_norm_g.shape[0] == 1
    q, iq, misc, conv, k, v, ika, ikb = _project(
        x, meta_tokens, attn_norm_g, w_in, conv_w, conv_b, conv_norm_g, conv_norm_b)
    attn = _attend(q, iq, misc, ika, ikb, k, v)
    return _out_mlp(x, attn, conv, w_out, mlp_norm_g, w_up, w_down, final_norm_g)
```

```python
import functools

import jax
import jax.numpy as jnp
from jax import lax
from jax.experimental import pallas as pl
from jax.experimental.pallas import tpu as pltpu

D_MODEL = 1024
N_META = 16
ATTN_HEADS = 4
ATTN_HEAD_DIM = 128
ATTN_W = ATTN_HEADS * ATTN_HEAD_DIM
ROPE_THETA = 500000.0
ROPE_FRACTION = 4
IDX_HEADS = 8
IDX_HEAD_DIM = 64
INDEX_TOPK = 256
CONV_CH = 512
CONV_GROUPS = 4
CONV_WIDTH = 31
D_FF = 4 * D_MODEL
NORM_EPS = 1e-5
NEG = -1e30

LANES = 128
SUBLANES = 8
BLK = 256
STRIPS = BLK // SUBLANES
PAD_ROWS = BLK - N_META
CARRY = 32
IQ_W = IDX_HEADS * IDX_HEAD_DIM
COL_Q = 0
COL_K = COL_Q + ATTN_W
COL_V = COL_K + ATTN_W
COL_IQ = COL_V + ATTN_W
COL_GA = COL_IQ + IQ_W
COL_GG = COL_GA + CONV_CH
COL_SLAB = COL_GG + CONV_CH
W_IN_COLS = COL_SLAB + LANES
VALUE_STEPS = 24
BLIND_STEPS = 12
N_ACC = 4
TILE_UNROLL = 16
BOUND_SLACK = 1.0 + 2.0 ** -6
L_MIN = 1e-30
MLP_ROWS = 512
FF_CHUNK = 1024
VMEM_LIMIT = 60 * 1024 * 1024

_NT = (((1,), (1,)), ((), ()))


def _rms(x, g):
    return x * lax.rsqrt(jnp.mean(x * x, axis=-1, keepdims=True) + NORM_EPS) * g


def _rope(x, c, s1, s2, half):
    return x * c + pltpu.roll(x, half, 1) * s1 + pltpu.roll(x, LANES - half, 1) * s2


def _proj_kernel(x_ref, meta_ref, g_ref, w_ref, rq_ref, ri_ref, cw_ref, cb_ref, lg_ref, lb_ref,
                 q_ref, iq_ref, misc_ref, conv_ref, k_ref, v_ref, ika_ref, ikb_ref, u_scr, win_scr):
    i = pl.program_id(1)
    h = jnp.where(i == 0, meta_ref[...], x_ref[0])
    xn = _rms(h, g_ref[...]).astype(jnp.bfloat16)
    proj = jnp.dot(xn, w_ref[...], preferred_element_type=jnp.float32)

    cq, s1q, s2q = rq_ref[0], rq_ref[1], rq_ref[2]
    ci, s1i, s2i = ri_ref[0], ri_ref[1], ri_ref[2]
    q_half = ATTN_HEAD_DIM // ROPE_FRACTION // 2
    i_half = IDX_HEAD_DIM // ROPE_FRACTION // 2
    for hd in range(ATTN_HEADS):
        sl = slice(hd * LANES, (hd + 1) * LANES)
        qh = _rope(proj[:, sl], cq, s1q, s2q, q_half) * (ATTN_HEAD_DIM ** -0.5)
        q_ref[0, :, sl] = qh.astype(jnp.bfloat16)
        kh = _rope(proj[:, COL_K + hd * LANES:COL_K + (hd + 1) * LANES], cq, s1q, s2q, q_half)
        k_ref[0, :, sl] = kh.astype(jnp.bfloat16)
        ih = _rope(proj[:, COL_IQ + hd * LANES:COL_IQ + (hd + 1) * LANES], ci, s1i, s2i, i_half)
        iq_ref[0, :, sl] = (ih * (IDX_HEAD_DIM ** -0.5)).astype(jnp.bfloat16)
    v_ref[0] = proj[:, COL_V:COL_V + ATTN_W].astype(jnp.bfloat16)

    lane = lax.broadcasted_iota(jnp.int32, (BLK, LANES), 1)
    is_ik = lane < IDX_HEAD_DIM
    slab = proj[:, COL_SLAB:COL_SLAB + LANES]
    roped = _rope(slab, jnp.where(is_ik, ci, 1.0), jnp.where(is_ik, s1i, 0.0),
                  jnp.where(is_ik, s2i, 0.0), i_half)
    misc_ref[0] = jnp.where(is_ik, roped, slab * (IDX_HEADS ** -0.5))
    ika = jnp.where(is_ik, roped, 0.0)
    ika_ref[0] = ika.astype(jnp.bfloat16)
    ikb_ref[0] = pltpu.roll(ika, IDX_HEAD_DIM, 1).astype(jnp.bfloat16)

    @pl.when(i == 0)
    def _():
        u_scr[0:CARRY, :] = jnp.zeros((CARRY, CONV_CH), jnp.float32)

    ga = proj[:, COL_GA:COL_GA + CONV_CH]
    gg = proj[:, COL_GG:COL_GG + CONV_CH]
    u_scr[CARRY:CARRY + BLK, :] = ga * jax.nn.sigmoid(gg)
    y = jnp.zeros((BLK, CONV_CH), jnp.float32) + cb_ref[...]
    for b in range(SUBLANES):
        base = CARRY - (CONV_WIDTH - 1) + b
        n_a = (CONV_WIDTH - 1 - b) // SUBLANES + 1
        rows = BLK + SUBLANES * (n_a - 1)
        win_scr[0:rows, :] = u_scr[base:base + rows, :]
        for a in range(n_a):
            j = SUBLANES * a + b
            y = y + win_scr[SUBLANES * a:SUBLANES * a + BLK, :] * cw_ref[j:j + 1, :]
    u_scr[0:CARRY, :] = u_scr[BLK:BLK + CARRY, :]
    gsz = CONV_CH // CONV_GROUPS
    for gi in range(CONV_GROUPS):
        sl = slice(gi * gsz, (gi + 1) * gsz)
        yg = y[:, sl]
        mu = jnp.mean(yg, axis=-1, keepdims=True)
        d = yg - mu
        var = jnp.mean(d * d, axis=-1, keepdims=True)
        yn = d * lax.rsqrt(var + NORM_EPS) * lg_ref[:, sl] + lb_ref[:, sl]
        conv_ref[0, :, sl] = (yn * jax.nn.sigmoid(yn)).astype(jnp.bfloat16)


def _f2k(f):
    b = lax.bitcast_convert_type(f, jnp.int32)
    return b ^ ((b >> 31) & jnp.int32(0x7FFFFFFF))


def _k2f(k):
    return lax.bitcast_convert_type(k ^ ((k >> 31) & jnp.int32(0x7FFFFFFF)), jnp.float32)


def _any(mask):
    return jnp.max(jnp.where(mask, 1.0, 0.0)) > 0.5


def _tile_loop(lo, hi, fn):
    n_main = (hi - lo) // TILE_UNROLL

    def main(i, c):
        for u in range(TILE_UNROLL):
            fn(lo + TILE_UNROLL * i + u)
        return c

    lax.fori_loop(0, n_main, main, 0)
    left = (hi - lo) - TILE_UNROLL * n_main
    size = TILE_UNROLL // 2
    while size >= 1:
        start = hi - (left & (2 * size - 1))

        @pl.when((left & size) != 0)
        def _(start=start, size=size):
            for u in range(size):
                fn(start + u)

        size //= 2


def _rows8(x):
    t = jnp.sum(jnp.sum(x, axis=0), axis=0, keepdims=True)
    return jnp.broadcast_to(t, (SUBLANES, BLK))


def _dsa_kernel(q_ref, iq_ref, misc_ref, ika_ref, ikb_ref, k_ref, v_ref, o_ref,
                sc_ref, tb_ref, tau_ref, mx_ref, mn_ref, ge_ref, gt_ref, kn_ref, mb_ref, ls_ref, acc_ref,
                m_ref, l_ref, *, n_rows_total):
    f32, bf16 = jnp.float32, jnp.bfloat16
    g = pl.program_id(1)
    nt = g + 2
    q0 = (g + 1) * BLK
    ones = jnp.ones((LANES, LANES), bf16)
    shape3 = (STRIPS, SUBLANES, BLK)
    keyi = (lax.broadcasted_iota(jnp.int32, shape3, 0) * SUBLANES
            + lax.broadcasted_iota(jnp.int32, shape3, 1))
    qryi = lax.broadcasted_iota(jnp.int32, shape3, 2)

    misc_t = misc_ref[0].T
    w8 = [jnp.broadcast_to(misc_t[IDX_HEAD_DIM + hd:IDX_HEAD_DIM + hd + 1, :], (SUBLANES, BLK))
          for hd in range(IDX_HEADS)]

    def partial(hit):
        return jnp.sum(jnp.where(hit, 1.0, 0.0).reshape(STRIPS // N_ACC, N_ACC, SUBLANES, BLK), axis=0)

    def score_tile(j):
        r = pl.multiple_of(j * BLK, BLK)
        ka = ika_ref[0, pl.ds(r, BLK), :]
        kb = ikb_ref[0, pl.ds(r, BLK), :]
        acc = jnp.zeros(shape3, f32)
        for p in range(IDX_HEADS // 2):
            slab = iq_ref[0, :, p * LANES:(p + 1) * LANES]
            le = lax.dot_general(ka, slab, _NT, preferred_element_type=f32).reshape(shape3)
            lo = lax.dot_general(kb, slab, _NT, preferred_element_type=f32).reshape(shape3)
            acc = (acc + jnp.maximum(le, 0.0) * w8[2 * p][None]
                   + jnp.maximum(lo, 0.0) * w8[2 * p + 1][None])
        return acc

    def put_tile(j, s, valid):
        lo = s if valid is None else jnp.where(valid, s, -jnp.inf)
        hi = s if valid is None else jnp.where(valid, s, jnp.inf)
        sc_ref[j] = lo
        mx_ref[...] = jnp.maximum(mx_ref[...], jnp.max(lo, axis=0))
        mn_ref[...] = jnp.minimum(mn_ref[...], jnp.min(hi, axis=0))
        ge_ref[...] += partial(lo >= 0.0)
        gt_ref[...] += partial(lo > 0.0)

    mx_ref[...] = jnp.full((SUBLANES, BLK), -jnp.inf, f32)
    mn_ref[...] = jnp.full((SUBLANES, BLK), jnp.inf, f32)
    ge_ref[...] = jnp.zeros(ge_ref.shape, f32)
    gt_ref[...] = jnp.zeros(gt_ref.shape, f32)
    put_tile(0, score_tile(0), keyi >= PAD_ROWS)
    put_tile(nt - 1, score_tile(nt - 1), keyi <= qryi)

    _tile_loop(1, nt - 1, lambda j: put_tile(j, score_tile(j), None))

    kf = float(INDEX_TOPK)
    qrow = q0 + lax.broadcasted_iota(jnp.int32, (SUBLANES, BLK), 1)
    n_valid = (qrow - (PAD_ROWS - 1)).astype(f32)
    n_virtual = (n_rows_total - 1 - qrow).astype(f32)
    rowmax = jnp.broadcast_to(jnp.max(mx_ref[...], axis=0, keepdims=True), (SUBLANES, BLK))
    rowmin = jnp.broadcast_to(jnp.min(mn_ref[...], axis=0, keepdims=True), (SUBLANES, BLK))
    zacc = jnp.zeros((N_ACC, SUBLANES, BLK), f32)

    def bump(acc, hit):
        hit = hit.reshape(STRIPS // N_ACC, N_ACC, SUBLANES, BLK)
        for s in range(STRIPS // N_ACC):
            acc = jnp.where(hit[s], acc + 1.0, acc)
        return acc

    def count_ge(cand):
        def pair(i, acc):
            acc = bump(acc, sc_ref[2 * i] >= cand[None])
            return bump(acc, sc_ref[2 * i + 1] >= cand[None])

        def single(j, acc):
            return bump(acc, sc_ref[j] >= cand[None])

        acc = lax.fori_loop(0, nt // 2, pair, zacc)
        return _rows8(lax.fori_loop(2 * (nt // 2), nt, single, acc))

    def total_of(c_proc, cand):
        return c_proc + jnp.where(cand <= NEG, n_virtual, 0.0)

    def active_of(klo, khi, clo):
        return (total_of(clo, _k2f(klo)) != kf) & (khi > klo + 1)

    kneg = _f2k(jnp.full((SUBLANES, BLK), NEG, f32))
    wide = n_valid > kf
    klo = jnp.where(wide, _f2k(jnp.maximum(rowmin, NEG)), kneg)
    khi = jnp.where(wide, _f2k(rowmax) + 1, klo + 1)
    clo = n_valid
    chi = jnp.zeros((SUBLANES, BLK), f32)

    c_ge0, c_gt0 = _rows8(ge_ref[...]), _rows8(gt_ref[...])
    kzero = _f2k(jnp.zeros((SUBLANES, BLK), f32))
    inside = (klo < kzero) & (kzero < khi)
    up = inside & (c_ge0 >= kf)
    dn = inside & (c_ge0 < kf)
    pin = up & (c_gt0 < kf)
    klo = jnp.where(up, kzero, klo)
    clo = jnp.where(up, c_ge0, clo)
    khi = jnp.where(dn, kzero, jnp.where(pin, kzero + 1, khi))
    chi = jnp.where(dn, c_ge0, jnp.where(pin, c_gt0, chi))

    def narrow(it, klo, khi, clo, chi):
        lo_f, hi_f = _k2f(klo), _k2f(khi)
        k_val = _f2k(lo_f + (hi_f - lo_f) * 0.5)
        k_mid = (klo >> 1) + (khi >> 1) + (klo & khi & 1)
        kt = jnp.where(it < VALUE_STEPS, k_val, k_mid)
        kt = jnp.minimum(jnp.maximum(kt, klo + 1), khi - 1)
        cand = _k2f(kt)
        c_proc = count_ge(cand)
        ge = total_of(c_proc, cand) >= kf
        act = active_of(klo, khi, clo)
        up = act & ge
        dn = act & jnp.logical_not(ge)
        return (jnp.where(up, kt, klo), jnp.where(dn, kt, khi),
                jnp.where(up, c_proc, clo), jnp.where(dn, c_proc, chi))

    klo, khi, clo, chi = lax.fori_loop(
        0, BLIND_STEPS, lambda it, st: narrow(it, *st), (klo, khi, clo, chi))

    def rows_left(klo, khi, clo):
        return jnp.max(jnp.where(active_of(klo, khi, clo), 1.0, 0.0))

    def search_step(st):
        _, it, klo, khi, clo, chi = st
        klo, khi, clo, chi = narrow(it, klo, khi, clo, chi)
        return rows_left(klo, khi, clo), it + 1, klo, khi, clo, chi

    _, _, klo, khi, clo, chi = lax.while_loop(
        lambda st: st[0] > 0.5, search_step,
        (rows_left(klo, khi, clo), jnp.int32(BLIND_STEPS), klo, khi, clo, chi))
    tau = _k2f(klo)
    tau_ref[...] = tau

    tied = clo > kf

    @pl.when(_any(tied))
    def _():
        need = kf - chi

        def scan(j, st):
            before, cross = st
            tb_ref[j] = before
            after = before + _rows8(partial(sc_ref[j] == tau[None]))
            here = (before < need) & (need <= after)
            return after, jnp.where(here, j.astype(f32), cross)

        zero8 = jnp.zeros((SUBLANES, BLK), f32)
        _, cross = lax.fori_loop(0, nt, scan, (zero8, zero8))
        j_lo = jnp.min(jnp.where(tied, cross, float(n_rows_total // BLK))).astype(jnp.int32)
        j_hi = jnp.max(jnp.where(tied, cross, 0.0)).astype(jnp.int32)
        need1 = need[0:1, :]
        tied1 = tied[0:1, :]
        tau1 = tau[0:1, :]
        tri = (lax.broadcasted_iota(jnp.int32, (BLK, BLK), 0)
               >= lax.broadcasted_iota(jnp.int32, (BLK, BLK), 1)).astype(bf16)

        def fix(j, c):
            x = sc_ref[j].reshape(BLK, BLK)
            eq = x == tau1
            rank = jnp.dot(tri, jnp.where(eq, 1.0, 0.0).astype(bf16),
                           preferred_element_type=f32) + tb_ref[j][0:1, :]
            cut = eq & (rank > need1) & tied1
            sc_ref[j] = jnp.where(cut, -jnp.inf, x).reshape(shape3)
            return c

        lax.fori_loop(j_lo, j_hi + 1, fix, 0)

        tau_tied = jnp.where(tied, tau, jnp.nan)[None]

        def drop(j, c):
            x = sc_ref[j]
            sc_ref[j] = jnp.where(x == tau_tied, -jnp.inf, x)
            return c

        lax.fori_loop(j_hi + 1, nt, drop, 0)

    @pl.when(g == 0)
    def _():
        kn_ref[...] = jnp.zeros(kn_ref.shape, f32)

        def norms(j, c):
            r = pl.multiple_of(j * BLK, BLK)
            for hd in range(ATTN_HEADS):
                kh = k_ref[0, pl.ds(r, BLK), hd * LANES:(hd + 1) * LANES].astype(f32)
                n2 = jnp.dot((kh * kh).astype(bf16), ones, preferred_element_type=f32)
                kn_ref[hd] = jnp.maximum(kn_ref[hd], jnp.max(n2, axis=0, keepdims=True))
            return c

        lax.fori_loop(0, n_rows_total // BLK, norms, 0)

    for hd in range(ATTN_HEADS):
        qh = q_ref[0, :, hd * LANES:(hd + 1) * LANES].astype(f32)
        qn2 = jnp.dot((qh * qh).astype(bf16), ones, preferred_element_type=f32)
        mb_ref[hd] = jnp.sqrt(qn2 * kn_ref[hd]) * BOUND_SLACK
    ls_ref[...] = jnp.zeros(ls_ref.shape, f32)
    acc_ref[...] = jnp.zeros(acc_ref.shape, f32)

    def bias_tile(j):
        sel = sc_ref[j] >= tau_ref[...][None]
        return jnp.where(sel, 0.0, NEG).reshape(BLK, BLK).T

    def attend_tile(j):
        r = pl.multiple_of(j * BLK, BLK)
        bias = bias_tile(j)
        b0 = bias[:, :LANES]
        b1 = bias[:, LANES:]
        for hd in range(ATTN_HEADS):
            sl = slice(hd * LANES, (hd + 1) * LANES)
            kh = k_ref[0, pl.ds(r, BLK), sl]
            vh = v_ref[0, pl.ds(r, BLK), sl]
            s = lax.dot_general(q_ref[0, :, sl], kh, _NT, preferred_element_type=f32)
            mb = mb_ref[hd]
            p0 = jnp.exp(s[:, :LANES] - mb + b0)
            p1 = jnp.exp(s[:, LANES:] - mb + b1)
            ls_ref[hd] += p0 + p1
            p = jnp.concatenate([p0, p1], axis=1).astype(bf16)
            acc_ref[hd] += jnp.dot(p, vh, preferred_element_type=f32)

    _tile_loop(0, nt, attend_tile)
    bad = jnp.float32(0.0)
    for hd in range(ATTN_HEADS):
        l = jnp.sum(ls_ref[hd], axis=1, keepdims=True)
        bad = jnp.maximum(bad, jnp.max(jnp.where(l >= L_MIN, 0.0, 1.0)))
        o_ref[0, :, hd * LANES:(hd + 1) * LANES] = (acc_ref[hd] / l).astype(bf16)

    @pl.when(bad > 0.5)
    def _():
        m_ref[...] = jnp.full(m_ref.shape, NEG, f32)
        l_ref[...] = jnp.zeros(l_ref.shape, f32)
        acc_ref[...] = jnp.zeros(acc_ref.shape, f32)

        def attend_online(j, c):
            r = pl.multiple_of(j * BLK, BLK)
            bias = bias_tile(j)
            for hd in range(ATTN_HEADS):
                sl = slice(hd * LANES, (hd + 1) * LANES)
                kh = k_ref[0, pl.ds(r, BLK), sl]
                vh = v_ref[0, pl.ds(r, BLK), sl]
                s = lax.dot_general(q_ref[0, :, sl], kh, _NT, preferred_element_type=f32) + bias
                m_old = m_ref[hd]
                m_new = jnp.maximum(m_old, jnp.max(s, axis=1, keepdims=True))
                p = jnp.exp(s - m_new)
                alpha = jnp.exp(m_old - m_new)
                l_ref[hd] = alpha * l_ref[hd] + jnp.sum(p, axis=1, keepdims=True)
                acc_ref[hd] = alpha * acc_ref[hd] + jnp.dot(p.astype(bf16), vh,
                                                            preferred_element_type=f32)
                m_ref[hd] = m_new
            return c

        lax.fori_loop(0, nt, attend_online, 0)
        for hd in range(ATTN_HEADS):
            o_ref[0, :, hd * LANES:(hd + 1) * LANES] = (acc_ref[hd] / l_ref[hd]).astype(bf16)


def _mlp_kernel(x_ref, a_ref, c_ref, woa_ref, woc_ref, g2_ref, wu_ref, wd_ref, gf_ref, o_ref):
    h = (x_ref[0]
         + jnp.dot(a_ref[0], woa_ref[...], preferred_element_type=jnp.float32)
         + jnp.dot(c_ref[0], woc_ref[...], preferred_element_type=jnp.float32))
    hn = _rms(h, g2_ref[...]).astype(jnp.bfloat16)
    acc = jnp.zeros_like(h)
    for c in range(D_FF // FF_CHUNK):
        sl = slice(c * FF_CHUNK, (c + 1) * FF_CHUNK)
        u = jnp.dot(hn, wu_ref[:, sl], preferred_element_type=jnp.float32)
        u = jnp.square(jnp.maximum(u, 0.0)).astype(jnp.bfloat16)
        acc = acc + jnp.dot(u, wd_ref[sl, :], preferred_element_type=jnp.float32)
    o_ref[0] = _rms(h + acc, gf_ref[...])


def _rope_tables(n_rows, head_dim, lanes_per_head):
    rot = head_dim // ROPE_FRACTION
    half = rot // 2
    inv = ROPE_THETA ** (-jnp.arange(0, rot, 2, dtype=jnp.float32) / rot)
    pos = jnp.arange(n_rows, dtype=jnp.float32) - float(PAD_ROWS)
    ang = pos[:, None] * inv[None, :]
    cos, sin = jnp.cos(ang), jnp.sin(ang)
    rest = lanes_per_head - rot
    one = jnp.ones((n_rows, rest), jnp.float32)
    zero = jnp.zeros((n_rows, rest), jnp.float32)
    zh = jnp.zeros((n_rows, half), jnp.float32)
    c = jnp.concatenate([cos, cos, one], axis=1)
    s1 = jnp.concatenate([zh, sin, zero], axis=1)
    s2 = jnp.concatenate([-sin, zh, zero], axis=1)
    reps = LANES // lanes_per_head
    return jnp.stack([jnp.tile(t, (1, reps)) for t in (c, s1, s2)])


def _const(shape):
    return pl.BlockSpec(shape, lambda b, i: (0,) * len(shape), pipeline_mode=pl.Buffered(1))


def _project(x, meta_tokens, attn_norm_g, w_in, conv_w, conv_b, conv_norm_g, conv_norm_b):
    B, S, D = x.shape
    nb = S // BLK
    T = S + BLK
    f32, bf16 = jnp.float32, jnp.bfloat16

    w = w_in[0]
    n_qkvi = COL_GA
    n_small = IDX_HEAD_DIM + IDX_HEADS
    w_rep = jnp.concatenate(
        [w[:, :n_qkvi], w[:, n_qkvi + n_small:], w[:, n_qkvi:n_qkvi + n_small],
         jnp.zeros((D, LANES - n_small), f32)], axis=1).astype(bf16)
    assert w_rep.shape[1] == W_IN_COLS
    meta_pad = jnp.concatenate([jnp.zeros((PAD_ROWS, D), f32), meta_tokens.astype(f32)], axis=0)
    rq = _rope_tables(T, ATTN_HEAD_DIM, LANES)
    ri = _rope_tables(T, IDX_HEAD_DIM, IDX_HEAD_DIM)

    real = lambda b, i: (b, jnp.maximum(i - 1, 0), 0)
    padded = lambda b, i: (b, i, 0)
    return pl.pallas_call(
        _proj_kernel,
        grid=(B, nb + 1),
        in_specs=[
            pl.BlockSpec((1, BLK, D), real),
            _const((BLK, D)),
            _const((1, D)),
            _const((D, W_IN_COLS)),
            pl.BlockSpec((3, BLK, LANES), lambda b, i: (0, i, 0)),
            pl.BlockSpec((3, BLK, LANES), lambda b, i: (0, i, 0)),
            _const((CONV_WIDTH, CONV_CH)),
            _const((1, CONV_CH)),
            _const((1, CONV_CH)),
            _const((1, CONV_CH)),
        ],
        out_specs=[
            pl.BlockSpec((1, BLK, ATTN_W), real),
            pl.BlockSpec((1, BLK, ATTN_W), real),
            pl.BlockSpec((1, BLK, LANES), real),
            pl.BlockSpec((1, BLK, CONV_CH), real),
            pl.BlockSpec((1, BLK, ATTN_W), padded),
            pl.BlockSpec((1, BLK, ATTN_W), padded),
            pl.BlockSpec((1, BLK, LANES), padded),
            pl.BlockSpec((1, BLK, LANES), padded),
        ],
        out_shape=[
            jax.ShapeDtypeStruct((B, S, ATTN_W), bf16),
            jax.ShapeDtypeStruct((B, S, ATTN_W), bf16),
            jax.ShapeDtypeStruct((B, S, LANES), f32),
            jax.ShapeDtypeStruct((B, S, CONV_CH), bf16),
            jax.ShapeDtypeStruct((B, T, ATTN_W), bf16),
            jax.ShapeDtypeStruct((B, T, ATTN_W), bf16),
            jax.ShapeDtypeStruct((B, T, LANES), bf16),
            jax.ShapeDtypeStruct((B, T, LANES), bf16),
        ],
        scratch_shapes=[pltpu.VMEM((CARRY + BLK, CONV_CH), f32),
                        pltpu.VMEM((CARRY + BLK, CONV_CH), f32)],
        compiler_params=pltpu.CompilerParams(
            dimension_semantics=("arbitrary", "arbitrary"), vmem_limit_bytes=VMEM_LIMIT),
        name="proj_rope_conv",
    )(x, meta_pad, attn_norm_g, w_rep, rq, ri, conv_w[0], conv_b, conv_norm_g, conv_norm_b)


def _attend(q, iq, misc, ika, ikb, k, v):
    B, S, _ = q.shape
    T = k.shape[1]
    nb = S // BLK
    f32 = jnp.float32
    blk = lambda b, g: (b, g, 0)
    whole = lambda width: pl.BlockSpec((1, T, width), lambda b, g: (b, 0, 0),
                                       pipeline_mode=pl.Buffered(1))
    return pl.pallas_call(
        functools.partial(_dsa_kernel, n_rows_total=T),
        grid=(B, nb),
        in_specs=[
            pl.BlockSpec((1, BLK, ATTN_W), blk),
            pl.BlockSpec((1, BLK, ATTN_W), blk),
            pl.BlockSpec((1, BLK, LANES), blk),
            whole(LANES), whole(LANES), whole(ATTN_W), whole(ATTN_W),
        ],
        out_specs=pl.BlockSpec((1, BLK, ATTN_W), blk),
        out_shape=jax.ShapeDtypeStruct((B, S, ATTN_W), jnp.bfloat16),
        scratch_shapes=[
            pltpu.VMEM((nb + 1, STRIPS, SUBLANES, BLK), f32),
            pltpu.VMEM((nb + 1, SUBLANES, BLK), f32),
            pltpu.VMEM((SUBLANES, BLK), f32),
            pltpu.VMEM((SUBLANES, BLK), f32),
            pltpu.VMEM((SUBLANES, BLK), f32),
            pltpu.VMEM((N_ACC, SUBLANES, BLK), f32),
            pltpu.VMEM((N_ACC, SUBLANES, BLK), f32),
            pltpu.VMEM((ATTN_HEADS, 1, LANES), f32),
            pltpu.VMEM((ATTN_HEADS, BLK, LANES), f32),
            pltpu.VMEM((ATTN_HEADS, BLK, LANES), f32),
            pltpu.VMEM((ATTN_HEADS, BLK, ATTN_HEAD_DIM), f32),
            pltpu.VMEM((ATTN_HEADS, BLK, 1), f32),
            pltpu.VMEM((ATTN_HEADS, BLK, 1), f32),
        ],
        compiler_params=pltpu.CompilerParams(
            dimension_semantics=("arbitrary", "arbitrary"), vmem_limit_bytes=VMEM_LIMIT),
        name="dsa_attention",
    )(q, iq, misc, ika, ikb, k, v)


def _out_mlp(x, attn, conv, w_out, mlp_norm_g, w_up, w_down, final_norm_g):
    B, S, D = x.shape
    bf16 = jnp.bfloat16
    wo = w_out[0].astype(bf16)
    rows = lambda b, i: (b, i, 0)
    return pl.pallas_call(
        _mlp_kernel,
        grid=(B, S // MLP_ROWS),
        in_specs=[
            pl.BlockSpec((1, MLP_ROWS, D), rows),
            pl.BlockSpec((1, MLP_ROWS, ATTN_W), rows),
            pl.BlockSpec((1, MLP_ROWS, CONV_CH), rows),
            _const((ATTN_W, D)),
            _const((CONV_CH, D)),
            _const((1, D)),
            _const((D, D_FF)),
            _const((D_FF, D)),
            _const((1, D)),
        ],
        out_specs=pl.BlockSpec((1, MLP_ROWS, D), rows),
        out_shape=jax.ShapeDtypeStruct((B, S, D), x.dtype),
        compiler_params=pltpu.CompilerParams(
            dimension_semantics=("arbitrary", "arbitrary"), vmem_limit_bytes=VMEM_LIMIT),
        name="out_mlp",
    )(x, attn, conv, wo[:ATTN_W], wo[ATTN_W:], mlp_norm_g, w_up[0].astype(bf16),
      w_down[0].astype(bf16), final_norm_g.reshape(1, D))


def kernel(x, meta_tokens, attn_norm_g, w_in, conv_w, conv_b, conv_norm_g, conv_norm_b,
           w_out, mlp_norm_g, w_up, w_down, final_norm_g):
    B, S, D = x.shape
    assert D == D_MODEL and S % MLP_ROWS == 0 and attn_norm_g.shape[0] == 1
    q, iq, misc, conv, k, v, ika, ikb = _project(
        x, meta_tokens, attn_norm_g, w_in, conv_w, conv_b, conv_norm_g, conv_norm_b)
    attn = _attend(q, iq, misc, ika, ikb, k, v)
    return _out_mlp(x, attn, conv, w_out, mlp_norm_g, w_up, w_down, final_norm_g)
```

```python
import functools

import jax
import jax.numpy as jnp
import numpy as np
from jax import lax
from jax.experimental import pallas as pl
from jax.experimental.pallas import tpu as pltpu

D_MODEL = 1024
N_META = 16
ATTN_HEADS = 4
ATTN_HEAD_DIM = 128
ATTN_W = ATTN_HEADS * ATTN_HEAD_DIM
ROPE_THETA = 500000.0
ROPE_FRACTION = 4
IDX_HEADS = 8
IDX_HEAD_DIM = 64
INDEX_TOPK = 256
CONV_CH = 512
CONV_GROUPS = 4
CONV_WIDTH = 31
D_FF = 4 * D_MODEL
NORM_EPS = 1e-5
NEG = -1e30

LANES = 128
SUBLANES = 8
BLK = 256
STRIPS = BLK // SUBLANES
PAD_ROWS = BLK - N_META
CARRY = 32
IQ_W = IDX_HEADS * IDX_HEAD_DIM
COL_Q = 0
COL_K = COL_Q + ATTN_W
COL_V = COL_K + ATTN_W
COL_IQ = COL_V + ATTN_W
COL_GA = COL_IQ + IQ_W
COL_GG = COL_GA + CONV_CH
COL_SLAB = COL_GG + CONV_CH
W_IN_COLS = COL_SLAB + LANES
VALUE_STEPS = 24
BLIND_STEPS = 12
N_ACC = 4
TILE_UNROLL = 16
BOUND_SLACK = 1.0 + 2.0 ** -6
L_MIN = 1e-30
MLP_ROWS = 512
FF_CHUNK = 1024
VMEM_LIMIT = 60 * 1024 * 1024

_NT = (((1,), (1,)), ((), ()))


def _rms(x, g):
    return x * lax.rsqrt(jnp.mean(x * x, axis=-1, keepdims=True) + NORM_EPS) * g


def _rope(x, c, s1, s2, half):
    return x * c + pltpu.roll(x, half, 1) * s1 + pltpu.roll(x, LANES - half, 1) * s2


def _proj_kernel(x_ref, meta_ref, g_ref, w_ref, rq_ref, ri_ref, cw_ref, cb_ref, lg_ref, lb_ref,
                 q_ref, iq_ref, misc_ref, conv_ref, k_ref, v_ref, ika_ref, ikb_ref, u_scr, win_scr):
    i = pl.program_id(1)
    h = jnp.where(i == 0, meta_ref[...], x_ref[0])
    xn = _rms(h, g_ref[...]).astype(jnp.bfloat16)
    proj = jnp.dot(xn, w_ref[...], preferred_element_type=jnp.float32)

    cq, s1q, s2q = rq_ref[0], rq_ref[1], rq_ref[2]
    ci, s1i, s2i = ri_ref[0], ri_ref[1], ri_ref[2]
    q_half = ATTN_HEAD_DIM // ROPE_FRACTION // 2
    i_half = IDX_HEAD_DIM // ROPE_FRACTION // 2
    for hd in range(ATTN_HEADS):
        sl = slice(hd * LANES, (hd + 1) * LANES)
        qh = _rope(proj[:, sl], cq, s1q, s2q, q_half) * (ATTN_HEAD_DIM ** -0.5)
        q_ref[0, :, sl] = qh.astype(jnp.bfloat16)
        kh = _rope(proj[:, COL_K + hd * LANES:COL_K + (hd + 1) * LANES], cq, s1q, s2q, q_half)
        k_ref[0, :, sl] = kh.astype(jnp.bfloat16)
        ih = _rope(proj[:, COL_IQ + hd * LANES:COL_IQ + (hd + 1) * LANES], ci, s1i, s2i, i_half)
        iq_ref[0, :, sl] = (ih * (IDX_HEAD_DIM ** -0.5)).astype(jnp.bfloat16)
    v_ref[0] = proj[:, COL_V:COL_V + ATTN_W].astype(jnp.bfloat16)

    lane = lax.broadcasted_iota(jnp.int32, (BLK, LANES), 1)
    is_ik = lane < IDX_HEAD_DIM
    slab = proj[:, COL_SLAB:COL_SLAB + LANES]
    roped = _rope(slab, jnp.where(is_ik, ci, 1.0), jnp.where(is_ik, s1i, 0.0),
                  jnp.where(is_ik, s2i, 0.0), i_half)
    misc_ref[0] = jnp.where(is_ik, roped, slab * (IDX_HEADS ** -0.5))
    ika = jnp.where(is_ik, roped, 0.0)
    ika_ref[0] = ika.astype(jnp.bfloat16)
    ikb_ref[0] = pltpu.roll(ika, IDX_HEAD_DIM, 1).astype(jnp.bfloat16)

    @pl.when(i == 0)
    def _():
        u_scr[0:CARRY, :] = jnp.zeros((CARRY, CONV_CH), jnp.float32)

    ga = proj[:, COL_GA:COL_GA + CONV_CH]
    gg = proj[:, COL_GG:COL_GG + CONV_CH]
    u_scr[CARRY:CARRY + BLK, :] = ga * jax.nn.sigmoid(gg)
    y = jnp.zeros((BLK, CONV_CH), jnp.float32) + cb_ref[...]
    for b in range(SUBLANES):
        base = CARRY - (CONV_WIDTH - 1) + b
        n_a = (CONV_WIDTH - 1 - b) // SUBLANES + 1
        rows = BLK + SUBLANES * (n_a - 1)
        win_scr[0:rows, :] = u_scr[base:base + rows, :]
        for a in range(n_a):
            j = SUBLANES * a + b
            y = y + win_scr[SUBLANES * a:SUBLANES * a + BLK, :] * cw_ref[j:j + 1, :]
    u_scr[0:CARRY, :] = u_scr[BLK:BLK + CARRY, :]
    gsz = CONV_CH // CONV_GROUPS
    for gi in range(CONV_GROUPS):
        sl = slice(gi * gsz, (gi + 1) * gsz)
        yg = y[:, sl]
        mu = jnp.mean(yg, axis=-1, keepdims=True)
        d = yg - mu
        var = jnp.mean(d * d, axis=-1, keepdims=True)
        yn = d * lax.rsqrt(var + NORM_EPS) * lg_ref[:, sl] + lb_ref[:, sl]
        conv_ref[0, :, sl] = (yn * jax.nn.sigmoid(yn)).astype(jnp.bfloat16)


def _f2k(f):
    b = lax.bitcast_convert_type(f, jnp.int32)
    return b ^ ((b >> 31) & jnp.int32(0x7FFFFFFF))


def _k2f(k):
    return lax.bitcast_convert_type(k ^ ((k >> 31) & jnp.int32(0x7FFFFFFF)), jnp.float32)


def _any(mask):
    return jnp.max(jnp.where(mask, 1.0, 0.0)) > 0.5


def _tile_loop(lo, hi, fn):
    n_main = (hi - lo) // TILE_UNROLL

    def main(i, c):
        for u in range(TILE_UNROLL):
            fn(lo + TILE_UNROLL * i + u)
        return c

    lax.fori_loop(0, n_main, main, 0)
    left = (hi - lo) - TILE_UNROLL * n_main
    size = TILE_UNROLL // 2
    while size >= 1:
        start = hi - (left & (2 * size - 1))

        @pl.when((left & size) != 0)
        def _(start=start, size=size):
            for u in range(size):
                fn(start + u)

        size //= 2


def _rows8(x):
    t = jnp.sum(jnp.sum(x, axis=0), axis=0, keepdims=True)
    return jnp.broadcast_to(t, (SUBLANES, BLK))


def _dsa_kernel(q_ref, iq_ref, misc_ref, ika_ref, ikb_ref, k_ref, v_ref, o_ref,
                sc_ref, tb_ref, tau_ref, mx_ref, mn_ref, ge_ref, gt_ref, kn_ref, mb_ref, ls_ref, acc_ref,
                m_ref, l_ref, *, n_rows_total):
    f32, bf16 = jnp.float32, jnp.bfloat16
    g = pl.program_id(1)
    nt = g + 2
    q0 = (g + 1) * BLK
    ones = jnp.ones((LANES, LANES), bf16)
    shape3 = (STRIPS, SUBLANES, BLK)
    keyi = (lax.broadcasted_iota(jnp.int32, shape3, 0) * SUBLANES
            + lax.broadcasted_iota(jnp.int32, shape3, 1))
    qryi = lax.broadcasted_iota(jnp.int32, shape3, 2)

    misc_t = misc_ref[0].T
    w8 = [jnp.broadcast_to(misc_t[IDX_HEAD_DIM + hd:IDX_HEAD_DIM + hd + 1, :], (SUBLANES, BLK))
          for hd in range(IDX_HEADS)]

    def partial(hit):
        return jnp.sum(jnp.where(hit, 1.0, 0.0).reshape(STRIPS // N_ACC, N_ACC, SUBLANES, BLK), axis=0)

    def score_tile(j):
        r = pl.multiple_of(j * BLK, BLK)
        ka = ika_ref[0, pl.ds(r, BLK), :]
        kb = ikb_ref[0, pl.ds(r, BLK), :]
        acc = jnp.zeros(shape3, f32)
        for p in range(IDX_HEADS // 2):
            slab = iq_ref[0, :, p * LANES:(p + 1) * LANES]
            le = lax.dot_general(ka, slab, _NT, preferred_element_type=f32).reshape(shape3)
            lo = lax.dot_general(kb, slab, _NT, preferred_element_type=f32).reshape(shape3)
            acc = (acc + jnp.maximum(le, 0.0) * w8[2 * p][None]
                   + jnp.maximum(lo, 0.0) * w8[2 * p + 1][None])
        return acc

    def put_tile(j, s, valid):
        lo = s if valid is None else jnp.where(valid, s, -jnp.inf)
        hi = s if valid is None else jnp.where(valid, s, jnp.inf)
        sc_ref[j] = lo
        mx_ref[...] = jnp.maximum(mx_ref[...], jnp.max(lo, axis=0))
        mn_ref[...] = jnp.minimum(mn_ref[...], jnp.min(hi, axis=0))
        ge_ref[...] += partial(lo >= 0.0)
        gt_ref[...] += partial(lo > 0.0)

    mx_ref[...] = jnp.full((SUBLANES, BLK), -jnp.inf, f32)
    mn_ref[...] = jnp.full((SUBLANES, BLK), jnp.inf, f32)
    ge_ref[...] = jnp.zeros(ge_ref.shape, f32)
    gt_ref[...] = jnp.zeros(gt_ref.shape, f32)
    put_tile(0, score_tile(0), keyi >= PAD_ROWS)
    put_tile(nt - 1, score_tile(nt - 1), keyi <= qryi)

    _tile_loop(1, nt - 1, lambda j: put_tile(j, score_tile(j), None))

    kf = float(INDEX_TOPK)
    qrow = q0 + lax.broadcasted_iota(jnp.int32, (SUBLANES, BLK), 1)
    n_valid = (qrow - (PAD_ROWS - 1)).astype(f32)
    n_virtual = (n_rows_total - 1 - qrow).astype(f32)
    rowmax = jnp.broadcast_to(jnp.max(mx_ref[...], axis=0, keepdims=True), (SUBLANES, BLK))
    rowmin = jnp.broadcast_to(jnp.min(mn_ref[...], axis=0, keepdims=True), (SUBLANES, BLK))
    zacc = jnp.zeros((N_ACC, SUBLANES, BLK), f32)

    def bump(acc, hit):
        hit = hit.reshape(STRIPS // N_ACC, N_ACC, SUBLANES, BLK)
        for s in range(STRIPS // N_ACC):
            acc = jnp.where(hit[s], acc + 1.0, acc)
        return acc

    def count_ge(cand):
        def pair(i, acc):
            acc = bump(acc, sc_ref[2 * i] >= cand[None])
            return bump(acc, sc_ref[2 * i + 1] >= cand[None])

        def single(j, acc):
            return bump(acc, sc_ref[j] >= cand[None])

        acc = lax.fori_loop(0, nt // 2, pair, zacc)
        return _rows8(lax.fori_loop(2 * (nt // 2), nt, single, acc))

    def total_of(c_proc, cand):
        return c_proc + jnp.where(cand <= NEG, n_virtual, 0.0)

    def active_of(klo, khi, clo):
        return (total_of(clo, _k2f(klo)) != kf) & (khi > klo + 1)

    kneg = _f2k(jnp.full((SUBLANES, BLK), NEG, f32))
    wide = n_valid > kf
    klo = jnp.where(wide, _f2k(jnp.maximum(rowmin, NEG)), kneg)
    khi = jnp.where(wide, _f2k(rowmax) + 1, klo + 1)
    clo = n_valid
    chi = jnp.zeros((SUBLANES, BLK), f32)

    c_ge0, c_gt0 = _rows8(ge_ref[...]), _rows8(gt_ref[...])
    kzero = _f2k(jnp.zeros((SUBLANES, BLK), f32))
    inside = (klo < kzero) & (kzero < khi)
    up = inside & (c_ge0 >= kf)
    dn = inside & (c_ge0 < kf)
    pin = up & (c_gt0 < kf)
    klo = jnp.where(up, kzero, klo)
    clo = jnp.where(up, c_ge0, clo)
    khi = jnp.where(dn, kzero, jnp.where(pin, kzero + 1, khi))
    chi = jnp.where(dn, c_ge0, jnp.where(pin, c_gt0, chi))

    def narrow(it, klo, khi, clo, chi):
        lo_f, hi_f = _k2f(klo), _k2f(khi)
        k_val = _f2k(lo_f + (hi_f - lo_f) * 0.5)
        k_mid = (klo >> 1) + (khi >> 1) + (klo & khi & 1)
        kt = jnp.where(it < VALUE_STEPS, k_val, k_mid)
        kt = jnp.minimum(jnp.maximum(kt, klo + 1), khi - 1)
        cand = _k2f(kt)
        c_proc = count_ge(cand)
        ge = total_of(c_proc, cand) >= kf
        act = active_of(klo, khi, clo)
        up = act & ge
        dn = act & jnp.logical_not(ge)
        return (jnp.where(up, kt, klo), jnp.where(dn, kt, khi),
                jnp.where(up, c_proc, clo), jnp.where(dn, c_proc, chi))

    klo, khi, clo, chi = lax.fori_loop(
        0, BLIND_STEPS, lambda it, st: narrow(it, *st), (klo, khi, clo, chi))

    def rows_left(klo, khi, clo):
        return jnp.max(jnp.where(active_of(klo, khi, clo), 1.0, 0.0))

    def search_step(st):
        _, it, klo, khi, clo, chi = st
        klo, khi, clo, chi = narrow(it, klo, khi, clo, chi)
        return rows_left(klo, khi, clo), it + 1, klo, khi, clo, chi

    _, _, klo, khi, clo, chi = lax.while_loop(
        lambda st: st[0] > 0.5, search_step,
        (rows_left(klo, khi, clo), jnp.int32(BLIND_STEPS), klo, khi, clo, chi))
    tau = _k2f(klo)
    tau_ref[...] = tau

    tied = clo > kf

    @pl.when(_any(tied))
    def _():
        need = kf - chi

        def scan(j, st):
            before, cross = st
            tb_ref[j] = before
            after = before + _rows8(partial(sc_ref[j] == tau[None]))
            here = (before < need) & (need <= after)
            return after, jnp.where(here, j.astype(f32), cross)

        zero8 = jnp.zeros((SUBLANES, BLK), f32)
        _, cross = lax.fori_loop(0, nt, scan, (zero8, zero8))
        j_lo = jnp.min(jnp.where(tied, cross, float(n_rows_total // BLK))).astype(jnp.int32)
        j_hi = jnp.max(jnp.where(tied, cross, 0.0)).astype(jnp.int32)
        need1 = need[0:1, :]
        tied1 = tied[0:1, :]
        tau1 = tau[0:1, :]
        tri = (lax.broadcasted_iota(jnp.int32, (BLK, BLK), 0)
               >= lax.broadcasted_iota(jnp.int32, (BLK, BLK), 1)).astype(bf16)

        def fix(j, c):
            x = sc_ref[j].reshape(BLK, BLK)
            eq = x == tau1
            rank = jnp.dot(tri, jnp.where(eq, 1.0, 0.0).astype(bf16),
                           preferred_element_type=f32) + tb_ref[j][0:1, :]
            cut = eq & (rank > need1) & tied1
            sc_ref[j] = jnp.where(cut, -jnp.inf, x).reshape(shape3)
            return c

        lax.fori_loop(j_lo, j_hi + 1, fix, 0)

        tau_tied = jnp.where(tied, tau, jnp.nan)[None]

        def drop(j, c):
            x = sc_ref[j]
            sc_ref[j] = jnp.where(x == tau_tied, -jnp.inf, x)
            return c

        lax.fori_loop(j_hi + 1, nt, drop, 0)

    @pl.when(g == 0)
    def _():
        kn_ref[...] = jnp.zeros(kn_ref.shape, f32)

        def norms(j, c):
            r = pl.multiple_of(j * BLK, BLK)
            for hd in range(ATTN_HEADS):
                kh = k_ref[0, pl.ds(r, BLK), hd * LANES:(hd + 1) * LANES].astype(f32)
                n2 = jnp.dot((kh * kh).astype(bf16), ones, preferred_element_type=f32)
                kn_ref[hd] = jnp.maximum(kn_ref[hd], jnp.max(n2, axis=0, keepdims=True))
            return c

        lax.fori_loop(0, n_rows_total // BLK, norms, 0)

    for hd in range(ATTN_HEADS):
        qh = q_ref[0, :, hd * LANES:(hd + 1) * LANES].astype(f32)
        qn2 = jnp.dot((qh * qh).astype(bf16), ones, preferred_element_type=f32)
        mb_ref[hd] = jnp.sqrt(qn2 * kn_ref[hd]) * BOUND_SLACK
    ls_ref[...] = jnp.zeros(ls_ref.shape, f32)
    acc_ref[...] = jnp.zeros(acc_ref.shape, f32)

    def bias_tile(j):
        sel = sc_ref[j] >= tau_ref[...][None]
        return jnp.where(sel, 0.0, NEG).reshape(BLK, BLK).T

    def attend_tile(j):
        r = pl.multiple_of(j * BLK, BLK)
        bias = bias_tile(j)
        b0 = bias[:, :LANES]
        b1 = bias[:, LANES:]
        for hd in range(ATTN_HEADS):
            sl = slice(hd * LANES, (hd + 1) * LANES)
            kh = k_ref[0, pl.ds(r, BLK), sl]
            vh = v_ref[0, pl.ds(r, BLK), sl]
            s = lax.dot_general(q_ref[0, :, sl], kh, _NT, preferred_element_type=f32)
            mb = mb_ref[hd]
            p0 = jnp.exp(s[:, :LANES] - mb + b0)
            p1 = jnp.exp(s[:, LANES:] - mb + b1)
            ls_ref[hd] += p0 + p1
            p = jnp.concatenate([p0, p1], axis=1).astype(bf16)
            acc_ref[hd] += jnp.dot(p, vh, preferred_element_type=f32)

    _tile_loop(0, nt, attend_tile)
    bad = jnp.float32(0.0)
    for hd in range(ATTN_HEADS):
        l = jnp.sum(ls_ref[hd], axis=1, keepdims=True)
        bad = jnp.maximum(bad, jnp.max(jnp.where(l >= L_MIN, 0.0, 1.0)))
        o_ref[0, :, hd * LANES:(hd + 1) * LANES] = (acc_ref[hd] / l).astype(bf16)

    @pl.when(bad > 0.5)
    def _():
        m_ref[...] = jnp.full(m_ref.shape, NEG, f32)
        l_ref[...] = jnp.zeros(l_ref.shape, f32)
        acc_ref[...] = jnp.zeros(acc_ref.shape, f32)

        def attend_online(j, c):
            r = pl.multiple_of(j * BLK, BLK)
            bias = bias_tile(j)
            for hd in range(ATTN_HEADS):
                sl = slice(hd * LANES, (hd + 1) * LANES)
                kh = k_ref[0, pl.ds(r, BLK), sl]
                vh = v_ref[0, pl.ds(r, BLK), sl]
                s = lax.dot_general(q_ref[0, :, sl], kh, _NT, preferred_element_type=f32) + bias
                m_old = m_ref[hd]
                m_new = jnp.maximum(m_old, jnp.max(s, axis=1, keepdims=True))
                p = jnp.exp(s - m_new)
                alpha = jnp.exp(m_old - m_new)
                l_ref[hd] = alpha * l_ref[hd] + jnp.sum(p, axis=1, keepdims=True)
                acc_ref[hd] = alpha * acc_ref[hd] + jnp.dot(p.astype(bf16), vh,
                                                            preferred_element_type=f32)
                m_ref[hd] = m_new
            return c

        lax.fori_loop(0, nt, attend_online, 0)
        for hd in range(ATTN_HEADS):
            o_ref[0, :, hd * LANES:(hd + 1) * LANES] = (acc_ref[hd] / l_ref[hd]).astype(bf16)


def _mlp_kernel(x_ref, a_ref, c_ref, woa_ref, woc_ref, g2_ref, wu_ref, wd_ref, gf_ref, o_ref):
    h = (x_ref[0]
         + jnp.dot(a_ref[0], woa_ref[...], preferred_element_type=jnp.float32)
         + jnp.dot(c_ref[0], woc_ref[...], preferred_element_type=jnp.float32))
    hn = _rms(h, g2_ref[...]).astype(jnp.bfloat16)
    acc = jnp.zeros_like(h)
    for c in range(D_FF // FF_CHUNK):
        sl = slice(c * FF_CHUNK, (c + 1) * FF_CHUNK)
        u = jnp.dot(hn, wu_ref[:, sl], preferred_element_type=jnp.float32)
        u = jnp.square(jnp.maximum(u, 0.0)).astype(jnp.bfloat16)
        acc = acc + jnp.dot(u, wd_ref[sl, :], preferred_element_type=jnp.float32)
    o_ref[0] = _rms(h + acc, gf_ref[...])


@functools.lru_cache(maxsize=None)
def _rope_tables(n_rows, head_dim, lanes_per_head):
    f32, f64 = np.float32, np.float64
    rot = head_dim // ROPE_FRACTION
    half = rot // 2
    inv = f64(ROPE_THETA) ** (-(np.arange(0, rot, 2, dtype=f64) / rot))
    pos = np.arange(n_rows, dtype=f64) - PAD_ROWS
    ang = pos[:, None] * inv[None, :]
    cos, sin = np.cos(ang), np.sin(ang)
    rest = lanes_per_head - rot
    one = np.ones((n_rows, rest), f64)
    zero = np.zeros((n_rows, rest), f64)
    zh = np.zeros((n_rows, half), f64)
    c = np.concatenate([cos, cos, one], axis=1)
    s1 = np.concatenate([zh, sin, zero], axis=1)
    s2 = np.concatenate([-sin, zh, zero], axis=1)
    reps = LANES // lanes_per_head
    return np.stack([np.tile(t, (1, reps)) for t in (c, s1, s2)]).astype(f32)


def _const(shape):
    return pl.BlockSpec(shape, lambda b, i: (0,) * len(shape), pipeline_mode=pl.Buffered(1))


def _project(x, meta_tokens, attn_norm_g, w_in, conv_w, conv_b, conv_norm_g, conv_norm_b):
    B, S, D = x.shape
    nb = S // BLK
    T = S + BLK
    f32, bf16 = jnp.float32, jnp.bfloat16

    w = w_in[0].astype(bf16)
    n_qkvi = COL_GA
    n_small = IDX_HEAD_DIM + IDX_HEADS
    w_rep = jnp.concatenate(
        [w[:, :n_qkvi], w[:, n_qkvi + n_small:], w[:, n_qkvi:n_qkvi + n_small],
         jnp.zeros((D, LANES - n_small), bf16)], axis=1)
    assert w_rep.shape[1] == W_IN_COLS
    meta_pad = jnp.concatenate([jnp.zeros((PAD_ROWS, D), f32), meta_tokens.astype(f32)], axis=0)
    rq = _rope_tables(T, ATTN_HEAD_DIM, LANES)
    ri = _rope_tables(T, IDX_HEAD_DIM, IDX_HEAD_DIM)

    real = lambda b, i: (b, jnp.maximum(i - 1, 0), 0)
    padded = lambda b, i: (b, i, 0)
    return pl.pallas_call(
        _proj_kernel,
        grid=(B, nb + 1),
        in_specs=[
            pl.BlockSpec((1, BLK, D), real),
            _const((BLK, D)),
            _const((1, D)),
            _const((D, W_IN_COLS)),
            pl.BlockSpec((3, BLK, LANES), lambda b, i: (0, i, 0)),
            pl.BlockSpec((3, BLK, LANES), lambda b, i: (0, i, 0)),
            _const((CONV_WIDTH, CONV_CH)),
            _const((1, CONV_CH)),
            _const((1, CONV_CH)),
            _const((1, CONV_CH)),
        ],
        out_specs=[
            pl.BlockSpec((1, BLK, ATTN_W), real),
            pl.BlockSpec((1, BLK, ATTN_W), real),
            pl.BlockSpec((1, BLK, LANES), real),
            pl.BlockSpec((1, BLK, CONV_CH), real),
            pl.BlockSpec((1, BLK, ATTN_W), padded),
            pl.BlockSpec((1, BLK, ATTN_W), padded),
            pl.BlockSpec((1, BLK, LANES), padded),
            pl.BlockSpec((1, BLK, LANES), padded),
        ],
        out_shape=[
            jax.ShapeDtypeStruct((B, S, ATTN_W), bf16),
            jax.ShapeDtypeStruct((B, S, ATTN_W), bf16),
            jax.ShapeDtypeStruct((B, S, LANES), f32),
            jax.ShapeDtypeStruct((B, S, CONV_CH), bf16),
            jax.ShapeDtypeStruct((B, T, ATTN_W), bf16),
            jax.ShapeDtypeStruct((B, T, ATTN_W), bf16),
            jax.ShapeDtypeStruct((B, T, LANES), bf16),
            jax.ShapeDtypeStruct((B, T, LANES), bf16),
        ],
        scratch_shapes=[pltpu.VMEM((CARRY + BLK, CONV_CH), f32),
                        pltpu.VMEM((CARRY + BLK, CONV_CH), f32)],
        compiler_params=pltpu.CompilerParams(
            dimension_semantics=("arbitrary", "arbitrary"), vmem_limit_bytes=VMEM_LIMIT),
        name="proj_rope_conv",
    )(x, meta_pad, attn_norm_g, w_rep, rq, ri, conv_w[0], conv_b, conv_norm_g, conv_norm_b)


def _attend(q, iq, misc, ika, ikb, k, v):
    B, S, _ = q.shape
    T = k.shape[1]
    nb = S // BLK
    f32 = jnp.float32
    blk = lambda b, g: (b, g, 0)
    whole = lambda width: pl.BlockSpec((1, T, width), lambda b, g: (b, 0, 0),
                                       pipeline_mode=pl.Buffered(1))
    return pl.pallas_call(
        functools.partial(_dsa_kernel, n_rows_total=T),
        grid=(B, nb),
        in_specs=[
            pl.BlockSpec((1, BLK, ATTN_W), blk),
            pl.BlockSpec((1, BLK, ATTN_W), blk),
            pl.BlockSpec((1, BLK, LANES), blk),
            whole(LANES), whole(LANES), whole(ATTN_W), whole(ATTN_W),
        ],
        out_specs=pl.BlockSpec((1, BLK, ATTN_W), blk),
        out_shape=jax.ShapeDtypeStruct((B, S, ATTN_W), jnp.bfloat16),
        scratch_shapes=[
            pltpu.VMEM((nb + 1, STRIPS, SUBLANES, BLK), f32),
            pltpu.VMEM((nb + 1, SUBLANES, BLK), f32),
            pltpu.VMEM((SUBLANES, BLK), f32),
            pltpu.VMEM((SUBLANES, BLK), f32),
            pltpu.VMEM((SUBLANES, BLK), f32),
            pltpu.VMEM((N_ACC, SUBLANES, BLK), f32),
            pltpu.VMEM((N_ACC, SUBLANES, BLK), f32),
            pltpu.VMEM((ATTN_HEADS, 1, LANES), f32),
            pltpu.VMEM((ATTN_HEADS, BLK, LANES), f32),
            pltpu.VMEM((ATTN_HEADS, BLK, LANES), f32),
            pltpu.VMEM((ATTN_HEADS, BLK, ATTN_HEAD_DIM), f32),
            pltpu.VMEM((ATTN_HEADS, BLK, 1), f32),
            pltpu.VMEM((ATTN_HEADS, BLK, 1), f32),
        ],
        compiler_params=pltpu.CompilerParams(
            dimension_semantics=("arbitrary", "arbitrary"), vmem_limit_bytes=VMEM_LIMIT),
        name="dsa_attention",
    )(q, iq, misc, ika, ikb, k, v)


def _out_mlp(x, attn, conv, w_out, mlp_norm_g, w_up, w_down, final_norm_g):
    B, S, D = x.shape
    bf16 = jnp.bfloat16
    wo = w_out[0].astype(bf16)
    rows = lambda b, i: (b, i, 0)
    return pl.pallas_call(
        _mlp_kernel,
        grid=(B, S // MLP_ROWS),
        in_specs=[
            pl.BlockSpec((1, MLP_ROWS, D), rows),
            pl.BlockSpec((1, MLP_ROWS, ATTN_W), rows),
            pl.BlockSpec((1, MLP_ROWS, CONV_CH), rows),
            _const((ATTN_W, D)),
            _const((CONV_CH, D)),
            _const((1, D)),
            _const((D, D_FF)),
            _const((D_FF, D)),
            _const((1, D)),
        ],
        out_specs=pl.BlockSpec((1, MLP_ROWS, D), rows),
        out_shape=jax.ShapeDtypeStruct((B, S, D), x.dtype),
        compiler_params=pltpu.CompilerParams(
            dimension_semantics=("arbitrary", "arbitrary"), vmem_limit_bytes=VMEM_LIMIT),
        name="out_mlp",
    )(x, attn, conv, wo[:ATTN_W], wo[ATTN_W:], mlp_norm_g, w_up[0].astype(bf16),
      w_down[0].astype(bf16), final_norm_g.reshape(1, D))


def kernel(x, meta_tokens, attn_norm_g, w_in, conv_w, conv_b, conv_norm_g, conv_norm_b,
           w_out, mlp_norm_g, w_up, w_down, final_norm_g):
    B, S, D = x.shape
    assert D == D_MODEL and S % MLP_ROWS == 0 and attn_norm_g.shape[0] == 1
    q, iq, misc, conv, k, v, ika, ikb = _project(
        x, meta_tokens, attn_norm_g, w_in, conv_w, conv_b, conv_norm_g, conv_norm_b)
    attn = _attend(q, iq, misc, ika, ikb, k, v)
    return _out_mlp(x, attn, conv, w_out, mlp_norm_g, w_up, w_down, final_norm_g)
```

```python
import functools

import jax
import jax.numpy as jnp
import numpy as np
from jax import lax
from jax.experimental import pallas as pl
from jax.experimental.pallas import tpu as pltpu

D_MODEL = 1024
N_META = 16
ATTN_HEADS = 4
ATTN_HEAD_DIM = 128
ATTN_W = ATTN_HEADS * ATTN_HEAD_DIM
ROPE_THETA = 500000.0
ROPE_FRACTION = 4
IDX_HEADS = 8
IDX_HEAD_DIM = 64
INDEX_TOPK = 256
CONV_CH = 512
CONV_GROUPS = 4
CONV_WIDTH = 31
D_FF = 4 * D_MODEL
NORM_EPS = 1e-5
NEG = -1e30

LANES = 128
SUBLANES = 8
BLK = 256
STRIPS = BLK // SUBLANES
PAD_ROWS = BLK - N_META
CARRY = 32
IQ_W = IDX_HEADS * IDX_HEAD_DIM
COL_Q = 0
COL_K = COL_Q + ATTN_W
COL_V = COL_K + ATTN_W
COL_IQ = COL_V + ATTN_W
COL_GA = COL_IQ + IQ_W
COL_GG = COL_GA + CONV_CH
COL_SLAB = COL_GG + CONV_CH
W_IN_COLS = COL_SLAB + LANES
VALUE_STEPS = 24
BLIND_STEPS = 14
N_ACC = 4
TILE_UNROLL = 16
BOUND_SLACK = 1.0 + 2.0 ** -6
L_MIN = 1e-30
MLP_ROWS = 512
FF_CHUNK = 1024
VMEM_LIMIT = 60 * 1024 * 1024

_NT = (((1,), (1,)), ((), ()))


def _rms(x, g):
    return x * lax.rsqrt(jnp.mean(x * x, axis=-1, keepdims=True) + NORM_EPS) * g


def _rope(x, c, s1, s2, half):
    return x * c + pltpu.roll(x, half, 1) * s1 + pltpu.roll(x, LANES - half, 1) * s2


def _proj_kernel(x_ref, meta_ref, g_ref, w_ref, rq_ref, ri_ref, cw_ref, cb_ref, lg_ref, lb_ref,
                 q_ref, iq_ref, misc_ref, conv_ref, k_ref, v_ref, ika_ref, ikb_ref, u_scr, win_scr):
    i = pl.program_id(1)
    h = jnp.where(i == 0, meta_ref[...], x_ref[0])
    xn = _rms(h, g_ref[...]).astype(jnp.bfloat16)
    proj = jnp.dot(xn, w_ref[...], preferred_element_type=jnp.float32)

    cq, s1q, s2q = rq_ref[0], rq_ref[1], rq_ref[2]
    ci, s1i, s2i = ri_ref[0], ri_ref[1], ri_ref[2]
    q_half = ATTN_HEAD_DIM // ROPE_FRACTION // 2
    i_half = IDX_HEAD_DIM // ROPE_FRACTION // 2
    for hd in range(ATTN_HEADS):
        sl = slice(hd * LANES, (hd + 1) * LANES)
        qh = _rope(proj[:, sl], cq, s1q, s2q, q_half) * (ATTN_HEAD_DIM ** -0.5)
        q_ref[0, :, sl] = qh.astype(jnp.bfloat16)
        kh = _rope(proj[:, COL_K + hd * LANES:COL_K + (hd + 1) * LANES], cq, s1q, s2q, q_half)
        k_ref[0, :, sl] = kh.astype(jnp.bfloat16)
        ih = _rope(proj[:, COL_IQ + hd * LANES:COL_IQ + (hd + 1) * LANES], ci, s1i, s2i, i_half)
        iq_ref[0, :, sl] = (ih * (IDX_HEAD_DIM ** -0.5)).astype(jnp.bfloat16)
    v_ref[0] = proj[:, COL_V:COL_V + ATTN_W].astype(jnp.bfloat16)

    lane = lax.broadcasted_iota(jnp.int32, (BLK, LANES), 1)
    is_ik = lane < IDX_HEAD_DIM
    slab = proj[:, COL_SLAB:COL_SLAB + LANES]
    roped = _rope(slab, jnp.where(is_ik, ci, 1.0), jnp.where(is_ik, s1i, 0.0),
                  jnp.where(is_ik, s2i, 0.0), i_half)
    misc_ref[0] = jnp.where(is_ik, roped, slab * (IDX_HEADS ** -0.5))
    ika = jnp.where(is_ik, roped, 0.0)
    ika_ref[0] = ika.astype(jnp.bfloat16)
    ikb_ref[0] = pltpu.roll(ika, IDX_HEAD_DIM, 1).astype(jnp.bfloat16)

    @pl.when(i == 0)
    def _():
        u_scr[0:CARRY, :] = jnp.zeros((CARRY, CONV_CH), jnp.float32)

    ga = proj[:, COL_GA:COL_GA + CONV_CH]
    gg = proj[:, COL_GG:COL_GG + CONV_CH]
    u_scr[CARRY:CARRY + BLK, :] = ga * jax.nn.sigmoid(gg)
    y = jnp.zeros((BLK, CONV_CH), jnp.float32) + cb_ref[...]
    for b in range(SUBLANES):
        base = CARRY - (CONV_WIDTH - 1) + b
        n_a = (CONV_WIDTH - 1 - b) // SUBLANES + 1
        rows = BLK + SUBLANES * (n_a - 1)
        win_scr[0:rows, :] = u_scr[base:base + rows, :]
        for a in range(n_a):
            j = SUBLANES * a + b
            y = y + win_scr[SUBLANES * a:SUBLANES * a + BLK, :] * cw_ref[j:j + 1, :]
    u_scr[0:CARRY, :] = u_scr[BLK:BLK + CARRY, :]
    gsz = CONV_CH // CONV_GROUPS
    for gi in range(CONV_GROUPS):
        sl = slice(gi * gsz, (gi + 1) * gsz)
        yg = y[:, sl]
        mu = jnp.mean(yg, axis=-1, keepdims=True)
        d = yg - mu
        var = jnp.mean(d * d, axis=-1, keepdims=True)
        yn = d * lax.rsqrt(var + NORM_EPS) * lg_ref[:, sl] + lb_ref[:, sl]
        conv_ref[0, :, sl] = (yn * jax.nn.sigmoid(yn)).astype(jnp.bfloat16)


def _f2k(f):
    b = lax.bitcast_convert_type(f, jnp.int32)
    return b ^ ((b >> 31) & jnp.int32(0x7FFFFFFF))


def _k2f(k):
    return lax.bitcast_convert_type(k ^ ((k >> 31) & jnp.int32(0x7FFFFFFF)), jnp.float32)


def _any(mask):
    return jnp.max(jnp.where(mask, 1.0, 0.0)) > 0.5


def _tile_loop(lo, hi, fn):
    n_main = (hi - lo) // TILE_UNROLL

    def main(i, c):
        for u in range(TILE_UNROLL):
            fn(lo + TILE_UNROLL * i + u)
        return c

    lax.fori_loop(0, n_main, main, 0)
    left = (hi - lo) - TILE_UNROLL * n_main
    size = TILE_UNROLL // 2
    while size >= 1:
        start = hi - (left & (2 * size - 1))

        @pl.when((left & size) != 0)
        def _(start=start, size=size):
            for u in range(size):
                fn(start + u)

        size //= 2


def _rows8(x):
    t = jnp.sum(jnp.sum(x, axis=0), axis=0, keepdims=True)
    return jnp.broadcast_to(t, (SUBLANES, BLK))


def _dsa_kernel(q_ref, iq_ref, misc_ref, ika_ref, ikb_ref, k_ref, v_ref, o_ref,
                sc_ref, tb_ref, tau_ref, mx_ref, mn_ref, ge_ref, gt_ref, kn_ref, mb_ref, ls_ref, acc_ref,
                m_ref, l_ref, *, n_rows_total):
    f32, bf16 = jnp.float32, jnp.bfloat16
    g = pl.program_id(1)
    nt = g + 2
    q0 = (g + 1) * BLK
    ones = jnp.ones((LANES, LANES), bf16)
    shape3 = (STRIPS, SUBLANES, BLK)
    keyi = (lax.broadcasted_iota(jnp.int32, shape3, 0) * SUBLANES
            + lax.broadcasted_iota(jnp.int32, shape3, 1))
    qryi = lax.broadcasted_iota(jnp.int32, shape3, 2)

    misc_t = misc_ref[0].T
    w8 = [jnp.broadcast_to(misc_t[IDX_HEAD_DIM + hd:IDX_HEAD_DIM + hd + 1, :], (SUBLANES, BLK))
          for hd in range(IDX_HEADS)]

    def partial(hit):
        return jnp.sum(jnp.where(hit, 1.0, 0.0).reshape(STRIPS // N_ACC, N_ACC, SUBLANES, BLK), axis=0)

    def score_tile(j):
        r = pl.multiple_of(j * BLK, BLK)
        ka = ika_ref[0, pl.ds(r, BLK), :]
        kb = ikb_ref[0, pl.ds(r, BLK), :]
        acc = jnp.zeros(shape3, f32)
        for p in range(IDX_HEADS // 2):
            slab = iq_ref[0, :, p * LANES:(p + 1) * LANES]
            le = lax.dot_general(ka, slab, _NT, preferred_element_type=f32).reshape(shape3)
            lo = lax.dot_general(kb, slab, _NT, preferred_element_type=f32).reshape(shape3)
            acc = (acc + jnp.maximum(le, 0.0) * w8[2 * p][None]
                   + jnp.maximum(lo, 0.0) * w8[2 * p + 1][None])
        return acc

    def put_tile(j, s, valid):
        lo = s if valid is None else jnp.where(valid, s, -jnp.inf)
        hi = s if valid is None else jnp.where(valid, s, jnp.inf)
        sc_ref[j] = lo
        mx_ref[...] = jnp.maximum(mx_ref[...], jnp.max(lo, axis=0))
        mn_ref[...] = jnp.minimum(mn_ref[...], jnp.min(hi, axis=0))
        ge_ref[...] += partial(lo >= 0.0)
        gt_ref[...] += partial(lo > 0.0)

    mx_ref[...] = jnp.full((SUBLANES, BLK), -jnp.inf, f32)
    mn_ref[...] = jnp.full((SUBLANES, BLK), jnp.inf, f32)
    ge_ref[...] = jnp.zeros(ge_ref.shape, f32)
    gt_ref[...] = jnp.zeros(gt_ref.shape, f32)
    put_tile(0, score_tile(0), keyi >= PAD_ROWS)
    put_tile(nt - 1, score_tile(nt - 1), keyi <= qryi)

    _tile_loop(1, nt - 1, lambda j: put_tile(j, score_tile(j), None))

    kf = float(INDEX_TOPK)
    qrow = q0 + lax.broadcasted_iota(jnp.int32, (SUBLANES, BLK), 1)
    n_valid = (qrow - (PAD_ROWS - 1)).astype(f32)
    n_virtual = (n_rows_total - 1 - qrow).astype(f32)
    rowmax = jnp.broadcast_to(jnp.max(mx_ref[...], axis=0, keepdims=True), (SUBLANES, BLK))
    rowmin = jnp.broadcast_to(jnp.min(mn_ref[...], axis=0, keepdims=True), (SUBLANES, BLK))
    zacc = jnp.zeros((N_ACC, SUBLANES, BLK), f32)

    def bump(acc, hit):
        hit = hit.reshape(STRIPS // N_ACC, N_ACC, SUBLANES, BLK)
        for s in range(STRIPS // N_ACC):
            acc = jnp.where(hit[s], acc + 1.0, acc)
        return acc

    def count_ge(cand):
        def pair(i, acc):
            acc = bump(acc, sc_ref[2 * i] >= cand[None])
            return bump(acc, sc_ref[2 * i + 1] >= cand[None])

        def single(j, acc):
            return bump(acc, sc_ref[j] >= cand[None])

        acc = lax.fori_loop(0, nt // 2, pair, zacc)
        return _rows8(lax.fori_loop(2 * (nt // 2), nt, single, acc))

    def total_of(c_proc, cand):
        return c_proc + jnp.where(cand <= NEG, n_virtual, 0.0)

    def active_of(klo, khi, clo):
        return (total_of(clo, _k2f(klo)) != kf) & (khi > klo + 1)

    kneg = _f2k(jnp.full((SUBLANES, BLK), NEG, f32))
    wide = n_valid > kf
    klo = jnp.where(wide, _f2k(jnp.maximum(rowmin, NEG)), kneg)
    khi = jnp.where(wide, _f2k(rowmax) + 1, klo + 1)
    clo = n_valid
    chi = jnp.zeros((SUBLANES, BLK), f32)

    c_ge0, c_gt0 = _rows8(ge_ref[...]), _rows8(gt_ref[...])
    kzero = _f2k(jnp.zeros((SUBLANES, BLK), f32))
    inside = (klo < kzero) & (kzero < khi)
    up = inside & (c_ge0 >= kf)
    dn = inside & (c_ge0 < kf)
    pin = up & (c_gt0 < kf)
    klo = jnp.where(up, kzero, klo)
    clo = jnp.where(up, c_ge0, clo)
    khi = jnp.where(dn, kzero, jnp.where(pin, kzero + 1, khi))
    chi = jnp.where(dn, c_ge0, jnp.where(pin, c_gt0, chi))

    def narrow(it, klo, khi, clo, chi):
        lo_f, hi_f = _k2f(klo), _k2f(khi)
        k_val = _f2k(lo_f + (hi_f - lo_f) * 0.5)
        k_mid = (klo >> 1) + (khi >> 1) + (klo & khi & 1)
        kt = jnp.where(it < VALUE_STEPS, k_val, k_mid)
        kt = jnp.minimum(jnp.maximum(kt, klo + 1), khi - 1)
        cand = _k2f(kt)
        c_proc = count_ge(cand)
        ge = total_of(c_proc, cand) >= kf
        act = active_of(klo, khi, clo)
        up = act & ge
        dn = act & jnp.logical_not(ge)
        return (jnp.where(up, kt, klo), jnp.where(dn, kt, khi),
                jnp.where(up, c_proc, clo), jnp.where(dn, c_proc, chi))

    klo, khi, clo, chi = lax.fori_loop(
        0, BLIND_STEPS, lambda it, st: narrow(it, *st), (klo, khi, clo, chi))

    def rows_left(klo, khi, clo):
        return jnp.max(jnp.where(active_of(klo, khi, clo), 1.0, 0.0))

    def search_step(st):
        _, it, klo, khi, clo, chi = st
        klo, khi, clo, chi = narrow(it, klo, khi, clo, chi)
        return rows_left(klo, khi, clo), it + 1, klo, khi, clo, chi

    _, _, klo, khi, clo, chi = lax.while_loop(
        lambda st: st[0] > 0.5, search_step,
        (rows_left(klo, khi, clo), jnp.int32(BLIND_STEPS), klo, khi, clo, chi))
    tau = _k2f(klo)
    tau_ref[...] = tau

    tied = clo > kf

    @pl.when(_any(tied))
    def _():
        need = kf - chi

        def scan(j, st):
            before, cross = st
            tb_ref[j] = before
            after = before + _rows8(partial(sc_ref[j] == tau[None]))
            here = (before < need) & (need <= after)
            return after, jnp.where(here, j.astype(f32), cross)

        zero8 = jnp.zeros((SUBLANES, BLK), f32)
        _, cross = lax.fori_loop(0, nt, scan, (zero8, zero8))
        j_lo = jnp.min(jnp.where(tied, cross, float(n_rows_total // BLK))).astype(jnp.int32)
        j_hi = jnp.max(jnp.where(tied, cross, 0.0)).astype(jnp.int32)
        need1 = need[0:1, :]
        tied1 = tied[0:1, :]
        tau1 = tau[0:1, :]
        tri = (lax.broadcasted_iota(jnp.int32, (BLK, BLK), 0)
               >= lax.broadcasted_iota(jnp.int32, (BLK, BLK), 1)).astype(bf16)

        def fix(j, c):
            x = sc_ref[j].reshape(BLK, BLK)
            eq = x == tau1
            rank = jnp.dot(tri, jnp.where(eq, 1.0, 0.0).astype(bf16),
                           preferred_element_type=f32) + tb_ref[j][0:1, :]
            cut = eq & (rank > need1) & tied1
            sc_ref[j] = jnp.where(cut, -jnp.inf, x).reshape(shape3)
            return c

        lax.fori_loop(j_lo, j_hi + 1, fix, 0)

        tau_tied = jnp.where(tied, tau, jnp.nan)[None]

        def drop(j, c):
            x = sc_ref[j]
            sc_ref[j] = jnp.where(x == tau_tied, -jnp.inf, x)
            return c

        lax.fori_loop(j_hi + 1, nt, drop, 0)

    @pl.when(g == 0)
    def _():
        kn_ref[...] = jnp.zeros(kn_ref.shape, f32)

        def norms(j, c):
            r = pl.multiple_of(j * BLK, BLK)
            for hd in range(ATTN_HEADS):
                kh = k_ref[0, pl.ds(r, BLK), hd * LANES:(hd + 1) * LANES].astype(f32)
                n2 = jnp.dot((kh * kh).astype(bf16), ones, preferred_element_type=f32)
                kn_ref[hd] = jnp.maximum(kn_ref[hd], jnp.max(n2, axis=0, keepdims=True))
            return c

        lax.fori_loop(0, n_rows_total // BLK, norms, 0)

    for hd in range(ATTN_HEADS):
        qh = q_ref[0, :, hd * LANES:(hd + 1) * LANES].astype(f32)
        qn2 = jnp.dot((qh * qh).astype(bf16), ones, preferred_element_type=f32)
        mb_ref[hd] = jnp.sqrt(qn2 * kn_ref[hd]) * BOUND_SLACK
    ls_ref[...] = jnp.zeros(ls_ref.shape, f32)
    acc_ref[...] = jnp.zeros(acc_ref.shape, f32)

    def bias_tile(j):
        sel = sc_ref[j] >= tau_ref[...][None]
        return jnp.where(sel, 0.0, NEG).reshape(BLK, BLK).T

    def attend_tile(j):
        r = pl.multiple_of(j * BLK, BLK)
        bias = bias_tile(j)
        b0 = bias[:, :LANES]
        b1 = bias[:, LANES:]
        for hd in range(ATTN_HEADS):
            sl = slice(hd * LANES, (hd + 1) * LANES)
            kh = k_ref[0, pl.ds(r, BLK), sl]
            vh = v_ref[0, pl.ds(r, BLK), sl]
            s = lax.dot_general(q_ref[0, :, sl], kh, _NT, preferred_element_type=f32)
            mb = mb_ref[hd]
            p0 = jnp.exp(s[:, :LANES] - mb + b0)
            p1 = jnp.exp(s[:, LANES:] - mb + b1)
            ls_ref[hd] += p0 + p1
            p = jnp.concatenate([p0, p1], axis=1).astype(bf16)
            acc_ref[hd] += jnp.dot(p, vh, preferred_element_type=f32)

    _tile_loop(0, nt, attend_tile)
    bad = jnp.float32(0.0)
    for hd in range(ATTN_HEADS):
        l = jnp.sum(ls_ref[hd], axis=1, keepdims=True)
        bad = jnp.maximum(bad, jnp.max(jnp.where(l >= L_MIN, 0.0, 1.0)))
        o_ref[0, :, hd * LANES:(hd + 1) * LANES] = (acc_ref[hd] / l).astype(bf16)

    @pl.when(bad > 0.5)
    def _():
        m_ref[...] = jnp.full(m_ref.shape, NEG, f32)
        l_ref[...] = jnp.zeros(l_ref.shape, f32)
        acc_ref[...] = jnp.zeros(acc_ref.shape, f32)

        def attend_online(j, c):
            r = pl.multiple_of(j * BLK, BLK)
            bias = bias_tile(j)
            for hd in range(ATTN_HEADS):
                sl = slice(hd * LANES, (hd + 1) * LANES)
                kh = k_ref[0, pl.ds(r, BLK), sl]
                vh = v_ref[0, pl.ds(r, BLK), sl]
                s = lax.dot_general(q_ref[0, :, sl], kh, _NT, preferred_element_type=f32) + bias
                m_old = m_ref[hd]
                m_new = jnp.maximum(m_old, jnp.max(s, axis=1, keepdims=True))
                p = jnp.exp(s - m_new)
                alpha = jnp.exp(m_old - m_new)
                l_ref[hd] = alpha * l_ref[hd] + jnp.sum(p, axis=1, keepdims=True)
                acc_ref[hd] = alpha * acc_ref[hd] + jnp.dot(p.astype(bf16), vh,
                                                            preferred_element_type=f32)
                m_ref[hd] = m_new
            return c

        lax.fori_loop(0, nt, attend_online, 0)
        for hd in range(ATTN_HEADS):
            o_ref[0, :, hd * LANES:(hd + 1) * LANES] = (acc_ref[hd] / l_ref[hd]).astype(bf16)


def _mlp_kernel(x_ref, a_ref, c_ref, woa_ref, woc_ref, g2_ref, wu_ref, wd_ref, gf_ref, o_ref):
    h = (x_ref[0]
         + jnp.dot(a_ref[0], woa_ref[...], preferred_element_type=jnp.float32)
         + jnp.dot(c_ref[0], woc_ref[...], preferred_element_type=jnp.float32))
    hn = _rms(h, g2_ref[...]).astype(jnp.bfloat16)
    acc = jnp.zeros_like(h)
    for c in range(D_FF // FF_CHUNK):
        sl = slice(c * FF_CHUNK, (c + 1) * FF_CHUNK)
        u = jnp.dot(hn, wu_ref[:, sl], preferred_element_type=jnp.float32)
        u = jnp.square(jnp.maximum(u, 0.0)).astype(jnp.bfloat16)
        acc = acc + jnp.dot(u, wd_ref[sl, :], preferred_element_type=jnp.float32)
    o_ref[0] = _rms(h + acc, gf_ref[...])


@functools.lru_cache(maxsize=None)
def _rope_tables(n_rows, head_dim, lanes_per_head):
    f32, f64 = np.float32, np.float64
    rot = head_dim // ROPE_FRACTION
    half = rot // 2
    inv = f64(ROPE_THETA) ** (-(np.arange(0, rot, 2, dtype=f64) / rot))
    pos = np.arange(n_rows, dtype=f64) - PAD_ROWS
    ang = pos[:, None] * inv[None, :]
    cos, sin = np.cos(ang), np.sin(ang)
    rest = lanes_per_head - rot
    one = np.ones((n_rows, rest), f64)
    zero = np.zeros((n_rows, rest), f64)
    zh = np.zeros((n_rows, half), f64)
    c = np.concatenate([cos, cos, one], axis=1)
    s1 = np.concatenate([zh, sin, zero], axis=1)
    s2 = np.concatenate([-sin, zh, zero], axis=1)
    reps = LANES // lanes_per_head
    return np.stack([np.tile(t, (1, reps)) for t in (c, s1, s2)]).astype(f32)


def _const(shape):
    return pl.BlockSpec(shape, lambda b, i: (0,) * len(shape), pipeline_mode=pl.Buffered(1))


def _project(x, meta_tokens, attn_norm_g, w_in, conv_w, conv_b, conv_norm_g, conv_norm_b):
    B, S, D = x.shape
    nb = S // BLK
    T = S + BLK
    f32, bf16 = jnp.float32, jnp.bfloat16

    w = w_in[0].astype(bf16)
    n_qkvi = COL_GA
    n_small = IDX_HEAD_DIM + IDX_HEADS
    w_rep = jnp.concatenate(
        [w[:, :n_qkvi], w[:, n_qkvi + n_small:], w[:, n_qkvi:n_qkvi + n_small],
         jnp.zeros((D, LANES - n_small), bf16)], axis=1)
    assert w_rep.shape[1] == W_IN_COLS
    meta_pad = jnp.concatenate([jnp.zeros((PAD_ROWS, D), f32), meta_tokens.astype(f32)], axis=0)
    rq = _rope_tables(T, ATTN_HEAD_DIM, LANES)
    ri = _rope_tables(T, IDX_HEAD_DIM, IDX_HEAD_DIM)

    real = lambda b, i: (b, jnp.maximum(i - 1, 0), 0)
    padded = lambda b, i: (b, i, 0)
    return pl.pallas_call(
        _proj_kernel,
        grid=(B, nb + 1),
        in_specs=[
            pl.BlockSpec((1, BLK, D), real),
            _const((BLK, D)),
            _const((1, D)),
            _const((D, W_IN_COLS)),
            pl.BlockSpec((3, BLK, LANES), lambda b, i: (0, i, 0)),
            pl.BlockSpec((3, BLK, LANES), lambda b, i: (0, i, 0)),
            _const((CONV_WIDTH, CONV_CH)),
            _const((1, CONV_CH)),
            _const((1, CONV_CH)),
            _const((1, CONV_CH)),
        ],
        out_specs=[
            pl.BlockSpec((1, BLK, ATTN_W), real),
            pl.BlockSpec((1, BLK, ATTN_W), real),
            pl.BlockSpec((1, BLK, LANES), real),
            pl.BlockSpec((1, BLK, CONV_CH), real),
            pl.BlockSpec((1, BLK, ATTN_W), padded),
            pl.BlockSpec((1, BLK, ATTN_W), padded),
            pl.BlockSpec((1, BLK, LANES), padded),
            pl.BlockSpec((1, BLK, LANES), padded),
        ],
        out_shape=[
            jax.ShapeDtypeStruct((B, S, ATTN_W), bf16),
            jax.ShapeDtypeStruct((B, S, ATTN_W), bf16),
            jax.ShapeDtypeStruct((B, S, LANES), f32),
            jax.ShapeDtypeStruct((B, S, CONV_CH), bf16),
            jax.ShapeDtypeStruct((B, T, ATTN_W), bf16),
            jax.ShapeDtypeStruct((B, T, ATTN_W), bf16),
            jax.ShapeDtypeStruct((B, T, LANES), bf16),
            jax.ShapeDtypeStruct((B, T, LANES), bf16),
        ],
        scratch_shapes=[pltpu.VMEM((CARRY + BLK, CONV_CH), f32),
                        pltpu.VMEM((CARRY + BLK, CONV_CH), f32)],
        compiler_params=pltpu.CompilerParams(
            dimension_semantics=("arbitrary", "arbitrary"), vmem_limit_bytes=VMEM_LIMIT),
        name="proj_rope_conv",
    )(x, meta_pad, attn_norm_g, w_rep, rq, ri, conv_w[0], conv_b, conv_norm_g, conv_norm_b)


def _attend(q, iq, misc, ika, ikb, k, v):
    B, S, _ = q.shape
    T = k.shape[1]
    nb = S // BLK
    f32 = jnp.float32
    blk = lambda b, g: (b, g, 0)
    whole = lambda width: pl.BlockSpec((1, T, width), lambda b, g: (b, 0, 0),
                                       pipeline_mode=pl.Buffered(1))
    return pl.pallas_call(
        functools.partial(_dsa_kernel, n_rows_total=T),
        grid=(B, nb),
        in_specs=[
            pl.BlockSpec((1, BLK, ATTN_W), blk),
            pl.BlockSpec((1, BLK, ATTN_W), blk),
            pl.BlockSpec((1, BLK, LANES), blk),
            whole(LANES), whole(LANES), whole(ATTN_W), whole(ATTN_W),
        ],
        out_specs=pl.BlockSpec((1, BLK, ATTN_W), blk),
        out_shape=jax.ShapeDtypeStruct((B, S, ATTN_W), jnp.bfloat16),
        scratch_shapes=[
            pltpu.VMEM((nb + 1, STRIPS, SUBLANES, BLK), f32),
            pltpu.VMEM((nb + 1, SUBLANES, BLK), f32),
            pltpu.VMEM((SUBLANES, BLK), f32),
            pltpu.VMEM((SUBLANES, BLK), f32),
            pltpu.VMEM((SUBLANES, BLK), f32),
            pltpu.VMEM((N_ACC, SUBLANES, BLK), f32),
            pltpu.VMEM((N_ACC, SUBLANES, BLK), f32),
            pltpu.VMEM((ATTN_HEADS, 1, LANES), f32),
            pltpu.VMEM((ATTN_HEADS, BLK, LANES), f32),
            pltpu.VMEM((ATTN_HEADS, BLK, LANES), f32),
            pltpu.VMEM((ATTN_HEADS, BLK, ATTN_HEAD_DIM), f32),
            pltpu.VMEM((ATTN_HEADS, BLK, 1), f32),
            pltpu.VMEM((ATTN_HEADS, BLK, 1), f32),
        ],
        compiler_params=pltpu.CompilerParams(
            dimension_semantics=("arbitrary", "arbitrary"), vmem_limit_bytes=VMEM_LIMIT),
        name="dsa_attention",
    )(q, iq, misc, ika, ikb, k, v)


def _out_mlp(x, attn, conv, w_out, mlp_norm_g, w_up, w_down, final_norm_g):
    B, S, D = x.shape
    bf16 = jnp.bfloat16
    wo = w_out[0].astype(bf16)
    rows = lambda b, i: (b, i, 0)
    return pl.pallas_call(
        _mlp_kernel,
        grid=(B, S // MLP_ROWS),
        in_specs=[
            pl.BlockSpec((1, MLP_ROWS, D), rows),
            pl.BlockSpec((1, MLP_ROWS, ATTN_W), rows),
            pl.BlockSpec((1, MLP_ROWS, CONV_CH), rows),
            _const((ATTN_W, D)),
            _const((CONV_CH, D)),
            _const((1, D)),
            _const((D, D_FF)),
            _const((D_FF, D)),
            _const((1, D)),
        ],
        out_specs=pl.BlockSpec((1, MLP_ROWS, D), rows),
        out_shape=jax.ShapeDtypeStruct((B, S, D), x.dtype),
        compiler_params=pltpu.CompilerParams(
            dimension_semantics=("arbitrary", "arbitrary"), vmem_limit_bytes=VMEM_LIMIT),
        name="out_mlp",
    )(x, attn, conv, wo[:ATTN_W], wo[ATTN_W:], mlp_norm_g, w_up[0].astype(bf16),
      w_down[0].astype(bf16), final_norm_g.reshape(1, D))


def kernel(x, meta_tokens, attn_norm_g, w_in, conv_w, conv_b, conv_norm_g, conv_norm_b,
           w_out, mlp_norm_g, w_up, w_down, final_norm_g):
    B, S, D = x.shape
    assert D == D_MODEL and S % MLP_ROWS == 0 and attn_norm_g.shape[0] == 1
    q, iq, misc, conv, k, v, ika, ikb = _project(
        x, meta_tokens, attn_norm_g, w_in, conv_w, conv_b, conv_norm_g, conv_norm_b)
    attn = _attend(q, iq, misc, ika, ikb, k, v)
    return _out_mlp(x, attn, conv, w_out, mlp_norm_g, w_up, w_down, final_norm_g)
```

```python
import functools

import jax
import jax.numpy as jnp
import numpy as np
from jax import lax
from jax.experimental import pallas as pl
from jax.experimental.pallas import tpu as pltpu

D_MODEL = 1024
N_META = 16
ATTN_HEADS = 4
ATTN_HEAD_DIM = 128
ATTN_W = ATTN_HEADS * ATTN_HEAD_DIM
ROPE_THETA = 500000.0
ROPE_FRACTION = 4
IDX_HEADS = 8
IDX_HEAD_DIM = 64
INDEX_TOPK = 256
CONV_CH = 512
CONV_GROUPS = 4
CONV_WIDTH = 31
D_FF = 4 * D_MODEL
NORM_EPS = 1e-5
NEG = -1e30

LANES = 128
SUBLANES = 8
BLK = 256
STRIPS = BLK // SUBLANES
PAD_ROWS = BLK - N_META
CARRY = 32
IQ_W = IDX_HEADS * IDX_HEAD_DIM
COL_Q = 0
COL_K = COL_Q + ATTN_W
COL_V = COL_K + ATTN_W
COL_IQ = COL_V + ATTN_W
COL_GA = COL_IQ + IQ_W
COL_GG = COL_GA + CONV_CH
COL_SLAB = COL_GG + CONV_CH
W_IN_COLS = COL_SLAB + LANES
VALUE_STEPS = 24
BLIND_STEPS = 16
N_ACC = 4
TILE_UNROLL = 16
BOUND_SLACK = 1.0 + 2.0 ** -6
L_MIN = 1e-30
MLP_ROWS = 512
FF_CHUNK = 1024
VMEM_LIMIT = 60 * 1024 * 1024

_NT = (((1,), (1,)), ((), ()))


def _rms(x, g):
    return x * lax.rsqrt(jnp.mean(x * x, axis=-1, keepdims=True) + NORM_EPS) * g


def _rope(x, c, s1, s2, half):
    return x * c + pltpu.roll(x, half, 1) * s1 + pltpu.roll(x, LANES - half, 1) * s2


def _proj_kernel(x_ref, meta_ref, g_ref, w_ref, rq_ref, ri_ref, cw_ref, cb_ref, lg_ref, lb_ref,
                 q_ref, iq_ref, misc_ref, conv_ref, k_ref, v_ref, ika_ref, ikb_ref, u_scr, win_scr):
    i = pl.program_id(1)
    h = jnp.where(i == 0, meta_ref[...], x_ref[0])
    xn = _rms(h, g_ref[...]).astype(jnp.bfloat16)
    proj = jnp.dot(xn, w_ref[...], preferred_element_type=jnp.float32)

    cq, s1q, s2q = rq_ref[0], rq_ref[1], rq_ref[2]
    ci, s1i, s2i = ri_ref[0], ri_ref[1], ri_ref[2]
    q_half = ATTN_HEAD_DIM // ROPE_FRACTION // 2
    i_half = IDX_HEAD_DIM // ROPE_FRACTION // 2
    for hd in range(ATTN_HEADS):
        sl = slice(hd * LANES, (hd + 1) * LANES)
        qh = _rope(proj[:, sl], cq, s1q, s2q, q_half) * (ATTN_HEAD_DIM ** -0.5)
        q_ref[0, :, sl] = qh.astype(jnp.bfloat16)
        kh = _rope(proj[:, COL_K + hd * LANES:COL_K + (hd + 1) * LANES], cq, s1q, s2q, q_half)
        k_ref[0, :, sl] = kh.astype(jnp.bfloat16)
        ih = _rope(proj[:, COL_IQ + hd * LANES:COL_IQ + (hd + 1) * LANES], ci, s1i, s2i, i_half)
        iq_ref[0, :, sl] = (ih * (IDX_HEAD_DIM ** -0.5)).astype(jnp.bfloat16)
    v_ref[0] = proj[:, COL_V:COL_V + ATTN_W].astype(jnp.bfloat16)

    lane = lax.broadcasted_iota(jnp.int32, (BLK, LANES), 1)
    is_ik = lane < IDX_HEAD_DIM
    slab = proj[:, COL_SLAB:COL_SLAB + LANES]
    roped = _rope(slab, jnp.where(is_ik, ci, 1.0), jnp.where(is_ik, s1i, 0.0),
                  jnp.where(is_ik, s2i, 0.0), i_half)
    misc_ref[0] = jnp.where(is_ik, roped, slab * (IDX_HEADS ** -0.5))
    ika = jnp.where(is_ik, roped, 0.0)
    ika_ref[0] = ika.astype(jnp.bfloat16)
    ikb_ref[0] = pltpu.roll(ika, IDX_HEAD_DIM, 1).astype(jnp.bfloat16)

    @pl.when(i == 0)
    def _():
        u_scr[0:CARRY, :] = jnp.zeros((CARRY, CONV_CH), jnp.float32)

    ga = proj[:, COL_GA:COL_GA + CONV_CH]
    gg = proj[:, COL_GG:COL_GG + CONV_CH]
    u_scr[CARRY:CARRY + BLK, :] = ga * jax.nn.sigmoid(gg)
    y = jnp.zeros((BLK, CONV_CH), jnp.float32) + cb_ref[...]
    for b in range(SUBLANES):
        base = CARRY - (CONV_WIDTH - 1) + b
        n_a = (CONV_WIDTH - 1 - b) // SUBLANES + 1
        rows = BLK + SUBLANES * (n_a - 1)
        win_scr[0:rows, :] = u_scr[base:base + rows, :]
        for a in range(n_a):
            j = SUBLANES * a + b
            y = y + win_scr[SUBLANES * a:SUBLANES * a + BLK, :] * cw_ref[j:j + 1, :]
    u_scr[0:CARRY, :] = u_scr[BLK:BLK + CARRY, :]
    gsz = CONV_CH // CONV_GROUPS
    for gi in range(CONV_GROUPS):
        sl = slice(gi * gsz, (gi + 1) * gsz)
        yg = y[:, sl]
        mu = jnp.mean(yg, axis=-1, keepdims=True)
        d = yg - mu
        var = jnp.mean(d * d, axis=-1, keepdims=True)
        yn = d * lax.rsqrt(var + NORM_EPS) * lg_ref[:, sl] + lb_ref[:, sl]
        conv_ref[0, :, sl] = (yn * jax.nn.sigmoid(yn)).astype(jnp.bfloat16)


def _f2k(f):
    b = lax.bitcast_convert_type(f, jnp.int32)
    return b ^ ((b >> 31) & jnp.int32(0x7FFFFFFF))


def _k2f(k):
    return lax.bitcast_convert_type(k ^ ((k >> 31) & jnp.int32(0x7FFFFFFF)), jnp.float32)


def _any(mask):
    return jnp.max(jnp.where(mask, 1.0, 0.0)) > 0.5


def _tile_loop(lo, hi, fn):
    n_main = (hi - lo) // TILE_UNROLL

    def main(i, c):
        for u in range(TILE_UNROLL):
            fn(lo + TILE_UNROLL * i + u)
        return c

    lax.fori_loop(0, n_main, main, 0)
    left = (hi - lo) - TILE_UNROLL * n_main
    size = TILE_UNROLL // 2
    while size >= 1:
        start = hi - (left & (2 * size - 1))

        @pl.when((left & size) != 0)
        def _(start=start, size=size):
            for u in range(size):
                fn(start + u)

        size //= 2


def _rows8(x):
    t = jnp.sum(jnp.sum(x, axis=0), axis=0, keepdims=True)
    return jnp.broadcast_to(t, (SUBLANES, BLK))


def _dsa_kernel(q_ref, iq_ref, misc_ref, ika_ref, ikb_ref, k_ref, v_ref, o_ref,
                sc_ref, tb_ref, tau_ref, mx_ref, mn_ref, ge_ref, gt_ref, kn_ref, mb_ref, ls_ref, acc_ref,
                m_ref, l_ref, *, n_rows_total):
    f32, bf16 = jnp.float32, jnp.bfloat16
    g = pl.program_id(1)
    nt = g + 2
    q0 = (g + 1) * BLK
    ones = jnp.ones((LANES, LANES), bf16)
    shape3 = (STRIPS, SUBLANES, BLK)
    keyi = (lax.broadcasted_iota(jnp.int32, shape3, 0) * SUBLANES
            + lax.broadcasted_iota(jnp.int32, shape3, 1))
    qryi = lax.broadcasted_iota(jnp.int32, shape3, 2)

    misc_t = misc_ref[0].T
    w8 = [jnp.broadcast_to(misc_t[IDX_HEAD_DIM + hd:IDX_HEAD_DIM + hd + 1, :], (SUBLANES, BLK))
          for hd in range(IDX_HEADS)]

    def partial(hit):
        return jnp.sum(jnp.where(hit, 1.0, 0.0).reshape(STRIPS // N_ACC, N_ACC, SUBLANES, BLK), axis=0)

    def score_tile(j):
        r = pl.multiple_of(j * BLK, BLK)
        ka = ika_ref[0, pl.ds(r, BLK), :]
        kb = ikb_ref[0, pl.ds(r, BLK), :]
        acc = jnp.zeros(shape3, f32)
        for p in range(IDX_HEADS // 2):
            slab = iq_ref[0, :, p * LANES:(p + 1) * LANES]
            le = lax.dot_general(ka, slab, _NT, preferred_element_type=f32).reshape(shape3)
            lo = lax.dot_general(kb, slab, _NT, preferred_element_type=f32).reshape(shape3)
            acc = (acc + jnp.maximum(le, 0.0) * w8[2 * p][None]
                   + jnp.maximum(lo, 0.0) * w8[2 * p + 1][None])
        return acc

    def put_tile(j, s, valid):
        lo = s if valid is None else jnp.where(valid, s, -jnp.inf)
        hi = s if valid is None else jnp.where(valid, s, jnp.inf)
        sc_ref[j] = lo
        mx_ref[...] = jnp.maximum(mx_ref[...], jnp.max(lo, axis=0))
        mn_ref[...] = jnp.minimum(mn_ref[...], jnp.min(hi, axis=0))
        ge_ref[...] += partial(lo >= 0.0)
        gt_ref[...] += partial(lo > 0.0)

    mx_ref[...] = jnp.full((SUBLANES, BLK), -jnp.inf, f32)
    mn_ref[...] = jnp.full((SUBLANES, BLK), jnp.inf, f32)
    ge_ref[...] = jnp.zeros(ge_ref.shape, f32)
    gt_ref[...] = jnp.zeros(gt_ref.shape, f32)
    put_tile(0, score_tile(0), keyi >= PAD_ROWS)
    put_tile(nt - 1, score_tile(nt - 1), keyi <= qryi)

    _tile_loop(1, nt - 1, lambda j: put_tile(j, score_tile(j), None))

    kf = float(INDEX_TOPK)
    qrow = q0 + lax.broadcasted_iota(jnp.int32, (SUBLANES, BLK), 1)
    n_valid = (qrow - (PAD_ROWS - 1)).astype(f32)
    n_virtual = (n_rows_total - 1 - qrow).astype(f32)
    rowmax = jnp.broadcast_to(jnp.max(mx_ref[...], axis=0, keepdims=True), (SUBLANES, BLK))
    rowmin = jnp.broadcast_to(jnp.min(mn_ref[...], axis=0, keepdims=True), (SUBLANES, BLK))
    zacc = jnp.zeros((N_ACC, SUBLANES, BLK), f32)

    def bump(acc, hit):
        hit = hit.reshape(STRIPS // N_ACC, N_ACC, SUBLANES, BLK)
        for s in range(STRIPS // N_ACC):
            acc = jnp.where(hit[s], acc + 1.0, acc)
        return acc

    def count_ge(cand):
        def pair(i, acc):
            acc = bump(acc, sc_ref[2 * i] >= cand[None])
            return bump(acc, sc_ref[2 * i + 1] >= cand[None])

        def single(j, acc):
            return bump(acc, sc_ref[j] >= cand[None])

        acc = lax.fori_loop(0, nt // 2, pair, zacc)
        return _rows8(lax.fori_loop(2 * (nt // 2), nt, single, acc))

    def total_of(c_proc, cand):
        return c_proc + jnp.where(cand <= NEG, n_virtual, 0.0)

    def active_of(klo, khi, clo):
        return (total_of(clo, _k2f(klo)) != kf) & (khi > klo + 1)

    kneg = _f2k(jnp.full((SUBLANES, BLK), NEG, f32))
    wide = n_valid > kf
    klo = jnp.where(wide, _f2k(jnp.maximum(rowmin, NEG)), kneg)
    khi = jnp.where(wide, _f2k(rowmax) + 1, klo + 1)
    clo = n_valid
    chi = jnp.zeros((SUBLANES, BLK), f32)

    c_ge0, c_gt0 = _rows8(ge_ref[...]), _rows8(gt_ref[...])
    kzero = _f2k(jnp.zeros((SUBLANES, BLK), f32))
    inside = (klo < kzero) & (kzero < khi)
    up = inside & (c_ge0 >= kf)
    dn = inside & (c_ge0 < kf)
    pin = up & (c_gt0 < kf)
    klo = jnp.where(up, kzero, klo)
    clo = jnp.where(up, c_ge0, clo)
    khi = jnp.where(dn, kzero, jnp.where(pin, kzero + 1, khi))
    chi = jnp.where(dn, c_ge0, jnp.where(pin, c_gt0, chi))

    def narrow(it, klo, khi, clo, chi):
        lo_f, hi_f = _k2f(klo), _k2f(khi)
        k_val = _f2k(lo_f + (hi_f - lo_f) * 0.5)
        k_mid = (klo >> 1) + (khi >> 1) + (klo & khi & 1)
        kt = jnp.where(it < VALUE_STEPS, k_val, k_mid)
        kt = jnp.minimum(jnp.maximum(kt, klo + 1), khi - 1)
        cand = _k2f(kt)
        c_proc = count_ge(cand)
        ge = total_of(c_proc, cand) >= kf
        act = active_of(klo, khi, clo)
        up = act & ge
        dn = act & jnp.logical_not(ge)
        return (jnp.where(up, kt, klo), jnp.where(dn, kt, khi),
                jnp.where(up, c_proc, clo), jnp.where(dn, c_proc, chi))

    klo, khi, clo, chi = lax.fori_loop(
        0, BLIND_STEPS, lambda it, st: narrow(it, *st), (klo, khi, clo, chi))

    def rows_left(klo, khi, clo):
        return jnp.max(jnp.where(active_of(klo, khi, clo), 1.0, 0.0))

    def search_step(st):
        _, it, klo, khi, clo, chi = st
        klo, khi, clo, chi = narrow(it, klo, khi, clo, chi)
        return rows_left(klo, khi, clo), it + 1, klo, khi, clo, chi

    _, _, klo, khi, clo, chi = lax.while_loop(
        lambda st: st[0] > 0.5, search_step,
        (rows_left(klo, khi, clo), jnp.int32(BLIND_STEPS), klo, khi, clo, chi))
    tau = _k2f(klo)
    tau_ref[...] = tau

    tied = clo > kf

    @pl.when(_any(tied))
    def _():
        need = kf - chi

        def scan(j, st):
            before, cross = st
            tb_ref[j] = before
            after = before + _rows8(partial(sc_ref[j] == tau[None]))
            here = (before < need) & (need <= after)
            return after, jnp.where(here, j.astype(f32), cross)

        zero8 = jnp.zeros((SUBLANES, BLK), f32)
        _, cross = lax.fori_loop(0, nt, scan, (zero8, zero8))
        j_lo = jnp.min(jnp.where(tied, cross, float(n_rows_total // BLK))).astype(jnp.int32)
        j_hi = jnp.max(jnp.where(tied, cross, 0.0)).astype(jnp.int32)
        need1 = need[0:1, :]
        tied1 = tied[0:1, :]
        tau1 = tau[0:1, :]
        tri = (lax.broadcasted_iota(jnp.int32, (BLK, BLK), 0)
               >= lax.broadcasted_iota(jnp.int32, (BLK, BLK), 1)).astype(bf16)

        def fix(j, c):
            x = sc_ref[j].reshape(BLK, BLK)
            eq = x == tau1
            rank = jnp.dot(tri, jnp.where(eq, 1.0, 0.0).astype(bf16),
                           preferred_element_type=f32) + tb_ref[j][0:1, :]
            cut = eq & (rank > need1) & tied1
            sc_ref[j] = jnp.where(cut, -jnp.inf, x).reshape(shape3)
            return c

        lax.fori_loop(j_lo, j_hi + 1, fix, 0)

        tau_tied = jnp.where(tied, tau, jnp.nan)[None]

        def drop(j, c):
            x = sc_ref[j]
            sc_ref[j] = jnp.where(x == tau_tied, -jnp.inf, x)
            return c

        lax.fori_loop(j_hi + 1, nt, drop, 0)

    @pl.when(g == 0)
    def _():
        kn_ref[...] = jnp.zeros(kn_ref.shape, f32)

        def norms(j, c):
            r = pl.multiple_of(j * BLK, BLK)
            for hd in range(ATTN_HEADS):
                kh = k_ref[0, pl.ds(r, BLK), hd * LANES:(hd + 1) * LANES].astype(f32)
                n2 = jnp.dot((kh * kh).astype(bf16), ones, preferred_element_type=f32)
                kn_ref[hd] = jnp.maximum(kn_ref[hd], jnp.max(n2, axis=0, keepdims=True))
            return c

        lax.fori_loop(0, n_rows_total // BLK, norms, 0)

    for hd in range(ATTN_HEADS):
        qh = q_ref[0, :, hd * LANES:(hd + 1) * LANES].astype(f32)
        qn2 = jnp.dot((qh * qh).astype(bf16), ones, preferred_element_type=f32)
        mb_ref[hd] = jnp.sqrt(qn2 * kn_ref[hd]) * BOUND_SLACK
    ls_ref[...] = jnp.zeros(ls_ref.shape, f32)
    acc_ref[...] = jnp.zeros(acc_ref.shape, f32)

    def bias_tile(j):
        sel = sc_ref[j] >= tau_ref[...][None]
        return jnp.where(sel, 0.0, NEG).reshape(BLK, BLK).T

    def attend_tile(j):
        r = pl.multiple_of(j * BLK, BLK)
        bias = bias_tile(j)
        b0 = bias[:, :LANES]
        b1 = bias[:, LANES:]
        for hd in range(ATTN_HEADS):
            sl = slice(hd * LANES, (hd + 1) * LANES)
            kh = k_ref[0, pl.ds(r, BLK), sl]
            vh = v_ref[0, pl.ds(r, BLK), sl]
            s = lax.dot_general(q_ref[0, :, sl], kh, _NT, preferred_element_type=f32)
            mb = mb_ref[hd]
            p0 = jnp.exp(s[:, :LANES] - mb + b0)
            p1 = jnp.exp(s[:, LANES:] - mb + b1)
            ls_ref[hd] += p0 + p1
            p = jnp.concatenate([p0, p1], axis=1).astype(bf16)
            acc_ref[hd] += jnp.dot(p, vh, preferred_element_type=f32)

    _tile_loop(0, nt, attend_tile)
    bad = jnp.float32(0.0)
    for hd in range(ATTN_HEADS):
        l = jnp.sum(ls_ref[hd], axis=1, keepdims=True)
        bad = jnp.maximum(bad, jnp.max(jnp.where(l >= L_MIN, 0.0, 1.0)))
        o_ref[0, :, hd * LANES:(hd + 1) * LANES] = (acc_ref[hd] / l).astype(bf16)

    @pl.when(bad > 0.5)
    def _():
        m_ref[...] = jnp.full(m_ref.shape, NEG, f32)
        l_ref[...] = jnp.zeros(l_ref.shape, f32)
        acc_ref[...] = jnp.zeros(acc_ref.shape, f32)

        def attend_online(j, c):
            r = pl.multiple_of(j * BLK, BLK)
            bias = bias_tile(j)
            for hd in range(ATTN_HEADS):
                sl = slice(hd * LANES, (hd + 1) * LANES)
                kh = k_ref[0, pl.ds(r, BLK), sl]
                vh = v_ref[0, pl.ds(r, BLK), sl]
                s = lax.dot_general(q_ref[0, :, sl], kh, _NT, preferred_element_type=f32) + bias
                m_old = m_ref[hd]
                m_new = jnp.maximum(m_old, jnp.max(s, axis=1, keepdims=True))
                p = jnp.exp(s - m_new)
                alpha = jnp.exp(m_old - m_new)
                l_ref[hd] = alpha * l_ref[hd] + jnp.sum(p, axis=1, keepdims=True)
                acc_ref[hd] = alpha * acc_ref[hd] + jnp.dot(p.astype(bf16), vh,
                                                            preferred_element_type=f32)
                m_ref[hd] = m_new
            return c

        lax.fori_loop(0, nt, attend_online, 0)
        for hd in range(ATTN_HEADS):
            o_ref[0, :, hd * LANES:(hd + 1) * LANES] = (acc_ref[hd] / l_ref[hd]).astype(bf16)


def _mlp_kernel(x_ref, a_ref, c_ref, woa_ref, woc_ref, g2_ref, wu_ref, wd_ref, gf_ref, o_ref):
    h = (x_ref[0]
         + jnp.dot(a_ref[0], woa_ref[...], preferred_element_type=jnp.float32)
         + jnp.dot(c_ref[0], woc_ref[...], preferred_element_type=jnp.float32))
    hn = _rms(h, g2_ref[...]).astype(jnp.bfloat16)
    acc = jnp.zeros_like(h)
    for c in range(D_FF // FF_CHUNK):
        sl = slice(c * FF_CHUNK, (c + 1) * FF_CHUNK)
        u = jnp.dot(hn, wu_ref[:, sl], preferred_element_type=jnp.float32)
        u = jnp.square(jnp.maximum(u, 0.0)).astype(jnp.bfloat16)
        acc = acc + jnp.dot(u, wd_ref[sl, :], preferred_element_type=jnp.float32)
    o_ref[0] = _rms(h + acc, gf_ref[...])


@functools.lru_cache(maxsize=None)
def _rope_tables(n_rows, head_dim, lanes_per_head):
    f32, f64 = np.float32, np.float64
    rot = head_dim // ROPE_FRACTION
    half = rot // 2
    inv = f64(ROPE_THETA) ** (-(np.arange(0, rot, 2, dtype=f64) / rot))
    pos = np.arange(n_rows, dtype=f64) - PAD_ROWS
    ang = pos[:, None] * inv[None, :]
    cos, sin = np.cos(ang), np.sin(ang)
    rest = lanes_per_head - rot
    one = np.ones((n_rows, rest), f64)
    zero = np.zeros((n_rows, rest), f64)
    zh = np.zeros((n_rows, half), f64)
    c = np.concatenate([cos, cos, one], axis=1)
    s1 = np.concatenate([zh, sin, zero], axis=1)
    s2 = np.concatenate([-sin, zh, zero], axis=1)
    reps = LANES // lanes_per_head
    return np.stack([np.tile(t, (1, reps)) for t in (c, s1, s2)]).astype(f32)


def _const(shape):
    return pl.BlockSpec(shape, lambda b, i: (0,) * len(shape), pipeline_mode=pl.Buffered(1))


def _project(x, meta_tokens, attn_norm_g, w_in, conv_w, conv_b, conv_norm_g, conv_norm_b):
    B, S, D = x.shape
    nb = S // BLK
    T = S + BLK
    f32, bf16 = jnp.float32, jnp.bfloat16

    w = w_in[0].astype(bf16)
    n_qkvi = COL_GA
    n_small = IDX_HEAD_DIM + IDX_HEADS
    w_rep = jnp.concatenate(
        [w[:, :n_qkvi], w[:, n_qkvi + n_small:], w[:, n_qkvi:n_qkvi + n_small],
         jnp.zeros((D, LANES - n_small), bf16)], axis=1)
    assert w_rep.shape[1] == W_IN_COLS
    meta_pad = jnp.concatenate([jnp.zeros((PAD_ROWS, D), f32), meta_tokens.astype(f32)], axis=0)
    rq = _rope_tables(T, ATTN_HEAD_DIM, LANES)
    ri = _rope_tables(T, IDX_HEAD_DIM, IDX_HEAD_DIM)

    real = lambda b, i: (b, jnp.maximum(i - 1, 0), 0)
    padded = lambda b, i: (b, i, 0)
    return pl.pallas_call(
        _proj_kernel,
        grid=(B, nb + 1),
        in_specs=[
            pl.BlockSpec((1, BLK, D), real),
            _const((BLK, D)),
            _const((1, D)),
            _const((D, W_IN_COLS)),
            pl.BlockSpec((3, BLK, LANES), lambda b, i: (0, i, 0)),
            pl.BlockSpec((3, BLK, LANES), lambda b, i: (0, i, 0)),
            _const((CONV_WIDTH, CONV_CH)),
            _const((1, CONV_CH)),
            _const((1, CONV_CH)),
            _const((1, CONV_CH)),
        ],
        out_specs=[
            pl.BlockSpec((1, BLK, ATTN_W), real),
            pl.BlockSpec((1, BLK, ATTN_W), real),
            pl.BlockSpec((1, BLK, LANES), real),
            pl.BlockSpec((1, BLK, CONV_CH), real),
            pl.BlockSpec((1, BLK, ATTN_W), padded),
            pl.BlockSpec((1, BLK, ATTN_W), padded),
            pl.BlockSpec((1, BLK, LANES), padded),
            pl.BlockSpec((1, BLK, LANES), padded),
        ],
        out_shape=[
            jax.ShapeDtypeStruct((B, S, ATTN_W), bf16),
            jax.ShapeDtypeStruct((B, S, ATTN_W), bf16),
            jax.ShapeDtypeStruct((B, S, LANES), f32),
            jax.ShapeDtypeStruct((B, S, CONV_CH), bf16),
            jax.ShapeDtypeStruct((B, T, ATTN_W), bf16),
            jax.ShapeDtypeStruct((B, T, ATTN_W), bf16),
            jax.ShapeDtypeStruct((B, T, LANES), bf16),
            jax.ShapeDtypeStruct((B, T, LANES), bf16),
        ],
        scratch_shapes=[pltpu.VMEM((CARRY + BLK, CONV_CH), f32),
                        pltpu.VMEM((CARRY + BLK, CONV_CH), f32)],
        compiler_params=pltpu.CompilerParams(
            dimension_semantics=("arbitrary", "arbitrary"), vmem_limit_bytes=VMEM_LIMIT),
        name="proj_rope_conv",
    )(x, meta_pad, attn_norm_g, w_rep, rq, ri, conv_w[0], conv_b, conv_norm_g, conv_norm_b)


def _attend(q, iq, misc, ika, ikb, k, v):
    B, S, _ = q.shape
    T = k.shape[1]
    nb = S // BLK
    f32 = jnp.float32
    blk = lambda b, g: (b, g, 0)
    whole = lambda width: pl.BlockSpec((1, T, width), lambda b, g: (b, 0, 0),
                                       pipeline_mode=pl.Buffered(1))
    return pl.pallas_call(
        functools.partial(_dsa_kernel, n_rows_total=T),
        grid=(B, nb),
        in_specs=[
            pl.BlockSpec((1, BLK, ATTN_W), blk),
            pl.BlockSpec((1, BLK, ATTN_W), blk),
            pl.BlockSpec((1, BLK, LANES), blk),
            whole(LANES), whole(LANES), whole(ATTN_W), whole(ATTN_W),
        ],
        out_specs=pl.BlockSpec((1, BLK, ATTN_W), blk),
        out_shape=jax.ShapeDtypeStruct((B, S, ATTN_W), jnp.bfloat16),
        scratch_shapes=[
            pltpu.VMEM((nb + 1, STRIPS, SUBLANES, BLK), f32),
            pltpu.VMEM((nb + 1, SUBLANES, BLK), f32),
            pltpu.VMEM((SUBLANES, BLK), f32),
            pltpu.VMEM((SUBLANES, BLK), f32),
            pltpu.VMEM((SUBLANES, BLK), f32),
            pltpu.VMEM((N_ACC, SUBLANES, BLK), f32),
            pltpu.VMEM((N_ACC, SUBLANES, BLK), f32),
            pltpu.VMEM((ATTN_HEADS, 1, LANES), f32),
            pltpu.VMEM((ATTN_HEADS, BLK, LANES), f32),
            pltpu.VMEM((ATTN_HEADS, BLK, LANES), f32),
            pltpu.VMEM((ATTN_HEADS, BLK, ATTN_HEAD_DIM), f32),
            pltpu.VMEM((ATTN_HEADS, BLK, 1), f32),
            pltpu.VMEM((ATTN_HEADS, BLK, 1), f32),
        ],
        compiler_params=pltpu.CompilerParams(
            dimension_semantics=("arbitrary", "arbitrary"), vmem_limit_bytes=VMEM_LIMIT),
        name="dsa_attention",
    )(q, iq, misc, ika, ikb, k, v)


def _out_mlp(x, attn, conv, w_out, mlp_norm_g, w_up, w_down, final_norm_g):
    B, S, D = x.shape
    bf16 = jnp.bfloat16
    wo = w_out[0].astype(bf16)
    rows = lambda b, i: (b, i, 0)
    return pl.pallas_call(
        _mlp_kernel,
        grid=(B, S // MLP_ROWS),
        in_specs=[
            pl.BlockSpec((1, MLP_ROWS, D), rows),
            pl.BlockSpec((1, MLP_ROWS, ATTN_W), rows),
            pl.BlockSpec((1, MLP_ROWS, CONV_CH), rows),
            _const((ATTN_W, D)),
            _const((CONV_CH, D)),
            _const((1, D)),
            _const((D, D_FF)),
            _const((D_FF, D)),
            _const((1, D)),
        ],
        out_specs=pl.BlockSpec((1, MLP_ROWS, D), rows),
        out_shape=jax.ShapeDtypeStruct((B, S, D), x.dtype),
        compiler_params=pltpu.CompilerParams(
            dimension_semantics=("arbitrary", "arbitrary"), vmem_limit_bytes=VMEM_LIMIT),
        name="out_mlp",
    )(x, attn, conv, wo[:ATTN_W], wo[ATTN_W:], mlp_norm_g, w_up[0].astype(bf16),
      w_down[0].astype(bf16), final_norm_g.reshape(1, D))


def kernel(x, meta_tokens, attn_norm_g, w_in, conv_w, conv_b, conv_norm_g, conv_norm_b,
           w_out, mlp_norm_g, w_up, w_down, final_norm_g):
    B, S, D = x.shape
    assert D == D_MODEL and S % MLP_ROWS == 0 and attn_norm_g.shape[0] == 1
    q, iq, misc, conv, k, v, ika, ikb = _project(
        x, meta_tokens, attn_norm_g, w_in, conv_w, conv_b, conv_norm_g, conv_norm_b)
    attn = _attend(q, iq, misc, ika, ikb, k, v)
    return _out_mlp(x, attn, conv, w_out, mlp_norm_g, w_up, w_down, final_norm_g)
```

```python
import functools

import jax
import jax.numpy as jnp
import numpy as np
from jax import lax
from jax.experimental import pallas as pl
from jax.experimental.pallas import tpu as pltpu

D_MODEL = 1024
N_META = 16
ATTN_HEADS = 4
ATTN_HEAD_DIM = 128
ATTN_W = ATTN_HEADS * ATTN_HEAD_DIM
ROPE_THETA = 500000.0
ROPE_FRACTION = 4
IDX_HEADS = 8
IDX_HEAD_DIM = 64
INDEX_TOPK = 256
CONV_CH = 512
CONV_GROUPS = 4
CONV_WIDTH = 31
D_FF = 4 * D_MODEL
NORM_EPS = 1e-5
NEG = -1e30

LANES = 128
SUBLANES = 8
BLK = 256
STRIPS = BLK // SUBLANES
PAD_ROWS = BLK - N_META
CARRY = 32
IQ_W = IDX_HEADS * IDX_HEAD_DIM
COL_Q = 0
COL_K = COL_Q + ATTN_W
COL_V = COL_K + ATTN_W
COL_IQ = COL_V + ATTN_W
COL_GA = COL_IQ + IQ_W
COL_GG = COL_GA + CONV_CH
COL_SLAB = COL_GG + CONV_CH
W_IN_COLS = COL_SLAB + LANES
VALUE_STEPS = 24
BLIND_STEPS = 16
N_ACC = 4
TILE_UNROLL = 16
BOUND_SLACK = 1.0 + 2.0 ** -6
L_MIN = 1e-30
MLP_ROWS = 512
FF_CHUNK = 1024
VMEM_LIMIT = 60 * 1024 * 1024

_NT = (((1,), (1,)), ((), ()))


def _rms(x, g):
    return x * lax.rsqrt(jnp.mean(x * x, axis=-1, keepdims=True) + NORM_EPS) * g


def _rope(x, c, s1, s2, half):
    return x * c + pltpu.roll(x, half, 1) * s1 + pltpu.roll(x, LANES - half, 1) * s2


def _proj_kernel(x_ref, meta_ref, g_ref, w_ref, rq_ref, ri_ref, cw_ref, cb_ref, lg_ref, lb_ref,
                 q_ref, iq_ref, misc_ref, conv_ref, k_ref, v_ref, ika_ref, ikb_ref, u_scr, win_scr):
    i = pl.program_id(1)
    h = jnp.where(i == 0, meta_ref[...], x_ref[0])
    xn = _rms(h, g_ref[...]).astype(jnp.bfloat16)
    proj = jnp.dot(xn, w_ref[...], preferred_element_type=jnp.float32)

    cq, s1q, s2q = rq_ref[0], rq_ref[1], rq_ref[2]
    ci, s1i, s2i = ri_ref[0], ri_ref[1], ri_ref[2]
    q_half = ATTN_HEAD_DIM // ROPE_FRACTION // 2
    i_half = IDX_HEAD_DIM // ROPE_FRACTION // 2
    for hd in range(ATTN_HEADS):
        sl = slice(hd * LANES, (hd + 1) * LANES)
        qh = _rope(proj[:, sl], cq, s1q, s2q, q_half) * (ATTN_HEAD_DIM ** -0.5)
        q_ref[0, :, sl] = qh.astype(jnp.bfloat16)
        kh = _rope(proj[:, COL_K + hd * LANES:COL_K + (hd + 1) * LANES], cq, s1q, s2q, q_half)
        k_ref[0, :, sl] = kh.astype(jnp.bfloat16)
        ih = _rope(proj[:, COL_IQ + hd * LANES:COL_IQ + (hd + 1) * LANES], ci, s1i, s2i, i_half)
        iq_ref[0, :, sl] = (ih * (IDX_HEAD_DIM ** -0.5)).astype(jnp.bfloat16)
    v_ref[0] = proj[:, COL_V:COL_V + ATTN_W].astype(jnp.bfloat16)

    lane = lax.broadcasted_iota(jnp.int32, (BLK, LANES), 1)
    is_ik = lane < IDX_HEAD_DIM
    slab = proj[:, COL_SLAB:COL_SLAB + LANES]
    roped = _rope(slab, jnp.where(is_ik, ci, 1.0), jnp.where(is_ik, s1i, 0.0),
                  jnp.where(is_ik, s2i, 0.0), i_half)
    misc_ref[0] = jnp.where(is_ik, roped, slab * (IDX_HEADS ** -0.5))
    ika = jnp.where(is_ik, roped, 0.0)
    ika_ref[0] = ika.astype(jnp.bfloat16)
    ikb_ref[0] = pltpu.roll(ika, IDX_HEAD_DIM, 1).astype(jnp.bfloat16)

    @pl.when(i == 0)
    def _():
        u_scr[0:CARRY, :] = jnp.zeros((CARRY, CONV_CH), jnp.float32)

    ga = proj[:, COL_GA:COL_GA + CONV_CH]
    gg = proj[:, COL_GG:COL_GG + CONV_CH]
    u_scr[CARRY:CARRY + BLK, :] = ga * jax.nn.sigmoid(gg)
    y = jnp.zeros((BLK, CONV_CH), jnp.float32) + cb_ref[...]
    for b in range(SUBLANES):
        base = CARRY - (CONV_WIDTH - 1) + b
        n_a = (CONV_WIDTH - 1 - b) // SUBLANES + 1
        rows = BLK + SUBLANES * (n_a - 1)
        win_scr[0:rows, :] = u_scr[base:base + rows, :]
        for a in range(n_a):
            j = SUBLANES * a + b
            y = y + win_scr[SUBLANES * a:SUBLANES * a + BLK, :] * cw_ref[j:j + 1, :]
    u_scr[0:CARRY, :] = u_scr[BLK:BLK + CARRY, :]
    gsz = CONV_CH // CONV_GROUPS
    for gi in range(CONV_GROUPS):
        sl = slice(gi * gsz, (gi + 1) * gsz)
        yg = y[:, sl]
        mu = jnp.mean(yg, axis=-1, keepdims=True)
        d = yg - mu
        var = jnp.mean(d * d, axis=-1, keepdims=True)
        yn = d * lax.rsqrt(var + NORM_EPS) * lg_ref[:, sl] + lb_ref[:, sl]
        conv_ref[0, :, sl] = (yn * jax.nn.sigmoid(yn)).astype(jnp.bfloat16)


def _f2k(f):
    b = lax.bitcast_convert_type(f, jnp.int32)
    return b ^ ((b >> 31) & jnp.int32(0x7FFFFFFF))


def _k2f(k):
    return lax.bitcast_convert_type(k ^ ((k >> 31) & jnp.int32(0x7FFFFFFF)), jnp.float32)


def _any(mask):
    return jnp.max(jnp.where(mask, 1.0, 0.0)) > 0.5


def _tile_loop(lo, hi, fn):
    n_main = (hi - lo) // TILE_UNROLL

    def main(i, c):
        for u in range(TILE_UNROLL):
            fn(lo + TILE_UNROLL * i + u)
        return c

    lax.fori_loop(0, n_main, main, 0)
    left = (hi - lo) - TILE_UNROLL * n_main
    size = TILE_UNROLL // 2
    while size >= 1:
        start = hi - (left & (2 * size - 1))

        @pl.when((left & size) != 0)
        def _(start=start, size=size):
            for u in range(size):
                fn(start + u)

        size //= 2


def _rows8(x):
    t = jnp.sum(jnp.sum(x, axis=0), axis=0, keepdims=True)
    return jnp.broadcast_to(t, (SUBLANES, BLK))


def _dsa_kernel(q_ref, iq_ref, misc_ref, ika_ref, ikb_ref, k_ref, v_ref, o_ref,
                sc_ref, tb_ref, tau_ref, mx_ref, mn_ref, ge_ref, gt_ref, kn_ref, mb_ref, ls_ref, acc_ref,
                m_ref, l_ref, *, n_rows_total):
    f32, bf16 = jnp.float32, jnp.bfloat16
    g = pl.program_id(1)
    nt = g + 2
    q0 = (g + 1) * BLK
    ones = jnp.ones((LANES, LANES), bf16)
    shape3 = (STRIPS, SUBLANES, BLK)
    keyi = (lax.broadcasted_iota(jnp.int32, shape3, 0) * SUBLANES
            + lax.broadcasted_iota(jnp.int32, shape3, 1))
    qryi = lax.broadcasted_iota(jnp.int32, shape3, 2)

    misc_t = misc_ref[0].T
    w8 = [jnp.broadcast_to(misc_t[IDX_HEAD_DIM + hd:IDX_HEAD_DIM + hd + 1, :], (SUBLANES, BLK))
          for hd in range(IDX_HEADS)]

    def partial(hit):
        return jnp.sum(jnp.where(hit, 1.0, 0.0).reshape(STRIPS // N_ACC, N_ACC, SUBLANES, BLK), axis=0)

    def score_tile(j):
        r = pl.multiple_of(j * BLK, BLK)
        ka = ika_ref[0, pl.ds(r, BLK), :]
        kb = ikb_ref[0, pl.ds(r, BLK), :]
        acc = jnp.zeros(shape3, f32)
        for p in range(IDX_HEADS // 2):
            slab = iq_ref[0, :, p * LANES:(p + 1) * LANES]
            le = lax.dot_general(ka, slab, _NT, preferred_element_type=f32).reshape(shape3)
            lo = lax.dot_general(kb, slab, _NT, preferred_element_type=f32).reshape(shape3)
            acc = (acc + jnp.maximum(le, 0.0) * w8[2 * p][None]
                   + jnp.maximum(lo, 0.0) * w8[2 * p + 1][None])
        return acc

    def put_tile(j, s, valid):
        lo = s if valid is None else jnp.where(valid, s, -jnp.inf)
        hi = s if valid is None else jnp.where(valid, s, jnp.inf)
        sc_ref[j] = lo
        mx_ref[...] = jnp.maximum(mx_ref[...], jnp.max(lo, axis=0))
        mn_ref[...] = jnp.minimum(mn_ref[...], jnp.min(hi, axis=0))
        ge_ref[...] += partial(lo >= 0.0)
        gt_ref[...] += partial(lo > 0.0)

    mx_ref[...] = jnp.full((SUBLANES, BLK), -jnp.inf, f32)
    mn_ref[...] = jnp.full((SUBLANES, BLK), jnp.inf, f32)
    ge_ref[...] = jnp.zeros(ge_ref.shape, f32)
    gt_ref[...] = jnp.zeros(gt_ref.shape, f32)
    put_tile(0, score_tile(0), keyi >= PAD_ROWS)
    put_tile(nt - 1, score_tile(nt - 1), keyi <= qryi)

    _tile_loop(1, nt - 1, lambda j: put_tile(j, score_tile(j), None))

    kf = float(INDEX_TOPK)
    qrow = q0 + lax.broadcasted_iota(jnp.int32, (SUBLANES, BLK), 1)
    n_valid = (qrow - (PAD_ROWS - 1)).astype(f32)
    n_virtual = (n_rows_total - 1 - qrow).astype(f32)
    rowmax = jnp.broadcast_to(jnp.max(mx_ref[...], axis=0, keepdims=True), (SUBLANES, BLK))
    rowmin = jnp.broadcast_to(jnp.min(mn_ref[...], axis=0, keepdims=True), (SUBLANES, BLK))
    zacc = jnp.zeros((N_ACC, SUBLANES, BLK), f32)

    def bump(acc, hit):
        hit = hit.reshape(STRIPS // N_ACC, N_ACC, SUBLANES, BLK)
        for s in range(STRIPS // N_ACC):
            acc = jnp.where(hit[s], acc + 1.0, acc)
        return acc

    def count_ge(cand):
        def pair(i, acc):
            acc = bump(acc, sc_ref[2 * i] >= cand[None])
            return bump(acc, sc_ref[2 * i + 1] >= cand[None])

        def single(j, acc):
            return bump(acc, sc_ref[j] >= cand[None])

        acc = lax.fori_loop(0, nt // 2, pair, zacc)
        return _rows8(lax.fori_loop(2 * (nt // 2), nt, single, acc))

    def total_of(c_proc, cand):
        return c_proc + jnp.where(cand <= NEG, n_virtual, 0.0)

    def active_of(klo, khi, clo):
        return (total_of(clo, _k2f(klo)) != kf) & (khi > klo + 1)

    kneg = _f2k(jnp.full((SUBLANES, BLK), NEG, f32))
    wide = n_valid > kf
    klo = jnp.where(wide, _f2k(jnp.maximum(rowmin, NEG)), kneg)
    khi = jnp.where(wide, _f2k(rowmax) + 1, klo + 1)
    clo = n_valid
    chi = jnp.zeros((SUBLANES, BLK), f32)

    c_ge0, c_gt0 = _rows8(ge_ref[...]), _rows8(gt_ref[...])
    kzero = _f2k(jnp.zeros((SUBLANES, BLK), f32))
    inside = (klo < kzero) & (kzero < khi)
    up = inside & (c_ge0 >= kf)
    dn = inside & (c_ge0 < kf)
    pin = up & (c_gt0 < kf)
    klo = jnp.where(up, kzero, klo)
    clo = jnp.where(up, c_ge0, clo)
    khi = jnp.where(dn, kzero, jnp.where(pin, kzero + 1, khi))
    chi = jnp.where(dn, c_ge0, jnp.where(pin, c_gt0, chi))

    def narrow(it, klo, khi, clo, chi):
        lo_f, hi_f = _k2f(klo), _k2f(khi)
        k_val = _f2k(lo_f + (hi_f - lo_f) * 0.5)
        k_mid = (klo >> 1) + (khi >> 1) + (klo & khi & 1)
        kt = jnp.where(it < VALUE_STEPS, k_val, k_mid)
        kt = jnp.minimum(jnp.maximum(kt, klo + 1), khi - 1)
        cand = _k2f(kt)
        c_proc = count_ge(cand)
        ge = total_of(c_proc, cand) >= kf
        act = active_of(klo, khi, clo)
        up = act & ge
        dn = act & jnp.logical_not(ge)
        return (jnp.where(up, kt, klo), jnp.where(dn, kt, khi),
                jnp.where(up, c_proc, clo), jnp.where(dn, c_proc, chi))

    klo, khi, clo, chi = lax.fori_loop(
        0, BLIND_STEPS, lambda it, st: narrow(it, *st), (klo, khi, clo, chi))

    def rows_left(klo, khi, clo):
        return jnp.max(jnp.where(active_of(klo, khi, clo), 1.0, 0.0))

    def search_step(st):
        _, it, klo, khi, clo, chi = st
        klo, khi, clo, chi = narrow(it, klo, khi, clo, chi)
        return rows_left(klo, khi, clo), it + 1, klo, khi, clo, chi

    _, _, klo, khi, clo, chi = lax.while_loop(
        lambda st: st[0] > 0.5, search_step,
        (rows_left(klo, khi, clo), jnp.int32(BLIND_STEPS), klo, khi, clo, chi))
    tau = _k2f(klo)
    tau_ref[...] = tau

    tied = clo > kf

    @pl.when(_any(tied))
    def _():
        need = kf - chi

        def scan(j, st):
            before, cross = st
            tb_ref[j] = before
            after = before + _rows8(partial(sc_ref[j] == tau[None]))
            here = (before < need) & (need <= after)
            return after, jnp.where(here, j.astype(f32), cross)

        zero8 = jnp.zeros((SUBLANES, BLK), f32)
        _, cross = lax.fori_loop(0, nt, scan, (zero8, zero8))
        j_lo = jnp.min(jnp.where(tied, cross, float(n_rows_total // BLK))).astype(jnp.int32)
        j_hi = jnp.max(jnp.where(tied, cross, 0.0)).astype(jnp.int32)
        need1 = need[0:1, :]
        tied1 = tied[0:1, :]
        tau1 = tau[0:1, :]
        tri = (lax.broadcasted_iota(jnp.int32, (BLK, BLK), 0)
               >= lax.broadcasted_iota(jnp.int32, (BLK, BLK), 1)).astype(bf16)

        def fix(j, c):
            x = sc_ref[j].reshape(BLK, BLK)
            eq = x == tau1
            rank = jnp.dot(tri, jnp.where(eq, 1.0, 0.0).astype(bf16),
                           preferred_element_type=f32) + tb_ref[j][0:1, :]
            cut = eq & (rank > need1) & tied1
            sc_ref[j] = jnp.where(cut, -jnp.inf, x).reshape(shape3)
            return c

        lax.fori_loop(j_lo, j_hi + 1, fix, 0)

        tau_tied = jnp.where(tied, tau, jnp.nan)[None]

        def drop(j, c):
            x = sc_ref[j]
            sc_ref[j] = jnp.where(x == tau_tied, -jnp.inf, x)
            return c

        lax.fori_loop(j_hi + 1, nt, drop, 0)

    @pl.when(g == 0)
    def _():
        kn_ref[...] = jnp.zeros(kn_ref.shape, f32)

        def norms(j, c):
            r = pl.multiple_of(j * BLK, BLK)
            for hd in range(ATTN_HEADS):
                kh = k_ref[0, pl.ds(r, BLK), hd * LANES:(hd + 1) * LANES].astype(f32)
                n2 = jnp.dot((kh * kh).astype(bf16), ones, preferred_element_type=f32)
                kn_ref[hd] = jnp.maximum(kn_ref[hd], jnp.max(n2, axis=0, keepdims=True))
            return c

        lax.fori_loop(0, n_rows_total // BLK, norms, 0)

    shift = None
    for hd in range(ATTN_HEADS):
        qh = q_ref[0, :, hd * LANES:(hd + 1) * LANES].astype(f32)
        qn2 = jnp.dot((qh * qh).astype(bf16), ones, preferred_element_type=f32)
        bound = jnp.sqrt(qn2 * kn_ref[hd]) * BOUND_SLACK
        shift = bound if shift is None else jnp.maximum(shift, bound)
    mb_ref[0] = shift
    ls_ref[...] = jnp.zeros(ls_ref.shape, f32)
    acc_ref[...] = jnp.zeros(acc_ref.shape, f32)

    def bias_tile(j):
        sel = sc_ref[j] >= tau_ref[...][None]
        return jnp.where(sel, 0.0, NEG).reshape(BLK, BLK).T

    def attend_tile(j):
        r = pl.multiple_of(j * BLK, BLK)
        bias = bias_tile(j)
        shift = mb_ref[0]
        b0 = bias[:, :LANES] - shift
        b1 = bias[:, LANES:] - shift
        for hd in range(ATTN_HEADS):
            sl = slice(hd * LANES, (hd + 1) * LANES)
            kh = k_ref[0, pl.ds(r, BLK), sl]
            vh = v_ref[0, pl.ds(r, BLK), sl]
            s = lax.dot_general(q_ref[0, :, sl], kh, _NT, preferred_element_type=f32)
            p0 = jnp.exp(s[:, :LANES] + b0)
            p1 = jnp.exp(s[:, LANES:] + b1)
            ls_ref[hd] += p0 + p1
            p = jnp.concatenate([p0, p1], axis=1).astype(bf16)
            acc_ref[hd] += jnp.dot(p, vh, preferred_element_type=f32)

    _tile_loop(0, nt, attend_tile)
    bad = jnp.float32(0.0)
    for hd in range(ATTN_HEADS):
        l = jnp.sum(ls_ref[hd], axis=1, keepdims=True)
        bad = jnp.maximum(bad, jnp.max(jnp.where(l >= L_MIN, 0.0, 1.0)))
        o_ref[0, :, hd * LANES:(hd + 1) * LANES] = (acc_ref[hd] / l).astype(bf16)

    @pl.when(bad > 0.5)
    def _():
        m_ref[...] = jnp.full(m_ref.shape, NEG, f32)
        l_ref[...] = jnp.zeros(l_ref.shape, f32)
        acc_ref[...] = jnp.zeros(acc_ref.shape, f32)

        def attend_online(j, c):
            r = pl.multiple_of(j * BLK, BLK)
            bias = bias_tile(j)
            for hd in range(ATTN_HEADS):
                sl = slice(hd * LANES, (hd + 1) * LANES)
                kh = k_ref[0, pl.ds(r, BLK), sl]
                vh = v_ref[0, pl.ds(r, BLK), sl]
                s = lax.dot_general(q_ref[0, :, sl], kh, _NT, preferred_element_type=f32) + bias
                m_old = m_ref[hd]
                m_new = jnp.maximum(m_old, jnp.max(s, axis=1, keepdims=True))
                p = jnp.exp(s - m_new)
                alpha = jnp.exp(m_old - m_new)
                l_ref[hd] = alpha * l_ref[hd] + jnp.sum(p, axis=1, keepdims=True)
                acc_ref[hd] = alpha * acc_ref[hd] + jnp.dot(p.astype(bf16), vh,
                                                            preferred_element_type=f32)
                m_ref[hd] = m_new
            return c

        lax.fori_loop(0, nt, attend_online, 0)
        for hd in range(ATTN_HEADS):
            o_ref[0, :, hd * LANES:(hd + 1) * LANES] = (acc_ref[hd] / l_ref[hd]).astype(bf16)


def _mlp_kernel(x_ref, a_ref, c_ref, woa_ref, woc_ref, g2_ref, wu_ref, wd_ref, gf_ref, o_ref):
    h = (x_ref[0]
         + jnp.dot(a_ref[0], woa_ref[...], preferred_element_type=jnp.float32)
         + jnp.dot(c_ref[0], woc_ref[...], preferred_element_type=jnp.float32))
    hn = _rms(h, g2_ref[...]).astype(jnp.bfloat16)
    acc = jnp.zeros_like(h)
    for c in range(D_FF // FF_CHUNK):
        sl = slice(c * FF_CHUNK, (c + 1) * FF_CHUNK)
        u = jnp.dot(hn, wu_ref[:, sl], preferred_element_type=jnp.float32)
        u = jnp.square(jnp.maximum(u, 0.0)).astype(jnp.bfloat16)
        acc = acc + jnp.dot(u, wd_ref[sl, :], preferred_element_type=jnp.float32)
    o_ref[0] = _rms(h + acc, gf_ref[...])


@functools.lru_cache(maxsize=None)
def _rope_tables(n_rows, head_dim, lanes_per_head):
    f32, f64 = np.float32, np.float64
    rot = head_dim // ROPE_FRACTION
    half = rot // 2
    inv = f64(ROPE_THETA) ** (-(np.arange(0, rot, 2, dtype=f64) / rot))
    pos = np.arange(n_rows, dtype=f64) - PAD_ROWS
    ang = pos[:, None] * inv[None, :]
    cos, sin = np.cos(ang), np.sin(ang)
    rest = lanes_per_head - rot
    one = np.ones((n_rows, rest), f64)
    zero = np.zeros((n_rows, rest), f64)
    zh = np.zeros((n_rows, half), f64)
    c = np.concatenate([cos, cos, one], axis=1)
    s1 = np.concatenate([zh, sin, zero], axis=1)
    s2 = np.concatenate([-sin, zh, zero], axis=1)
    reps = LANES // lanes_per_head
    return np.stack([np.tile(t, (1, reps)) for t in (c, s1, s2)]).astype(f32)


def _const(shape):
    return pl.BlockSpec(shape, lambda b, i: (0,) * len(shape), pipeline_mode=pl.Buffered(1))


def _project(x, meta_tokens, attn_norm_g, w_in, conv_w, conv_b, conv_norm_g, conv_norm_b):
    B, S, D = x.shape
    nb = S // BLK
    T = S + BLK
    f32, bf16 = jnp.float32, jnp.bfloat16

    w = w_in[0].astype(bf16)
    n_qkvi = COL_GA
    n_small = IDX_HEAD_DIM + IDX_HEADS
    w_rep = jnp.concatenate(
        [w[:, :n_qkvi], w[:, n_qkvi + n_small:], w[:, n_qkvi:n_qkvi + n_small],
         jnp.zeros((D, LANES - n_small), bf16)], axis=1)
    assert w_rep.shape[1] == W_IN_COLS
    meta_pad = jnp.concatenate([jnp.zeros((PAD_ROWS, D), f32), meta_tokens.astype(f32)], axis=0)
    rq = _rope_tables(T, ATTN_HEAD_DIM, LANES)
    ri = _rope_tables(T, IDX_HEAD_DIM, IDX_HEAD_DIM)

    real = lambda b, i: (b, jnp.maximum(i - 1, 0), 0)
    padded = lambda b, i: (b, i, 0)
    return pl.pallas_call(
        _proj_kernel,
        grid=(B, nb + 1),
        in_specs=[
            pl.BlockSpec((1, BLK, D), real),
            _const((BLK, D)),
            _const((1, D)),
            _const((D, W_IN_COLS)),
            pl.BlockSpec((3, BLK, LANES), lambda b, i: (0, i, 0)),
            pl.BlockSpec((3, BLK, LANES), lambda b, i: (0, i, 0)),
            _const((CONV_WIDTH, CONV_CH)),
            _const((1, CONV_CH)),
            _const((1, CONV_CH)),
            _const((1, CONV_CH)),
        ],
        out_specs=[
            pl.BlockSpec((1, BLK, ATTN_W), real),
            pl.BlockSpec((1, BLK, ATTN_W), real),
            pl.BlockSpec((1, BLK, LANES), real),
            pl.BlockSpec((1, BLK, CONV_CH), real),
            pl.BlockSpec((1, BLK, ATTN_W), padded),
            pl.BlockSpec((1, BLK, ATTN_W), padded),
            pl.BlockSpec((1, BLK, LANES), padded),
            pl.BlockSpec((1, BLK, LANES), padded),
        ],
        out_shape=[
            jax.ShapeDtypeStruct((B, S, ATTN_W), bf16),
            jax.ShapeDtypeStruct((B, S, ATTN_W), bf16),
            jax.ShapeDtypeStruct((B, S, LANES), f32),
            jax.ShapeDtypeStruct((B, S, CONV_CH), bf16),
            jax.ShapeDtypeStruct((B, T, ATTN_W), bf16),
            jax.ShapeDtypeStruct((B, T, ATTN_W), bf16),
            jax.ShapeDtypeStruct((B, T, LANES), bf16),
            jax.ShapeDtypeStruct((B, T, LANES), bf16),
        ],
        scratch_shapes=[pltpu.VMEM((CARRY + BLK, CONV_CH), f32),
                        pltpu.VMEM((CARRY + BLK, CONV_CH), f32)],
        compiler_params=pltpu.CompilerParams(
            dimension_semantics=("arbitrary", "arbitrary"), vmem_limit_bytes=VMEM_LIMIT),
        name="proj_rope_conv",
    )(x, meta_pad, attn_norm_g, w_rep, rq, ri, conv_w[0], conv_b, conv_norm_g, conv_norm_b)


def _attend(q, iq, misc, ika, ikb, k, v):
    B, S, _ = q.shape
    T = k.shape[1]
    nb = S // BLK
    f32 = jnp.float32
    blk = lambda b, g: (b, g, 0)
    whole = lambda width: pl.BlockSpec((1, T, width), lambda b, g: (b, 0, 0),
                                       pipeline_mode=pl.Buffered(1))
    return pl.pallas_call(
        functools.partial(_dsa_kernel, n_rows_total=T),
        grid=(B, nb),
        in_specs=[
            pl.BlockSpec((1, BLK, ATTN_W), blk),
            pl.BlockSpec((1, BLK, ATTN_W), blk),
            pl.BlockSpec((1, BLK, LANES), blk),
            whole(LANES), whole(LANES), whole(ATTN_W), whole(ATTN_W),
        ],
        out_specs=pl.BlockSpec((1, BLK, ATTN_W), blk),
        out_shape=jax.ShapeDtypeStruct((B, S, ATTN_W), jnp.bfloat16),
        scratch_shapes=[
            pltpu.VMEM((nb + 1, STRIPS, SUBLANES, BLK), f32),
            pltpu.VMEM((nb + 1, SUBLANES, BLK), f32),
            pltpu.VMEM((SUBLANES, BLK), f32),
            pltpu.VMEM((SUBLANES, BLK), f32),
            pltpu.VMEM((SUBLANES, BLK), f32),
            pltpu.VMEM((N_ACC, SUBLANES, BLK), f32),
            pltpu.VMEM((N_ACC, SUBLANES, BLK), f32),
            pltpu.VMEM((ATTN_HEADS, 1, LANES), f32),
            pltpu.VMEM((ATTN_HEADS, BLK, LANES), f32),
            pltpu.VMEM((ATTN_HEADS, BLK, LANES), f32),
            pltpu.VMEM((ATTN_HEADS, BLK, ATTN_HEAD_DIM), f32),
            pltpu.VMEM((ATTN_HEADS, BLK, 1), f32),
            pltpu.VMEM((ATTN_HEADS, BLK, 1), f32),
        ],
        compiler_params=pltpu.CompilerParams(
            dimension_semantics=("arbitrary", "arbitrary"), vmem_limit_bytes=VMEM_LIMIT),
        name="dsa_attention",
    )(q, iq, misc, ika, ikb, k, v)


def _out_mlp(x, attn, conv, w_out, mlp_norm_g, w_up, w_down, final_norm_g):
    B, S, D = x.shape
    bf16 = jnp.bfloat16
    wo = w_out[0].astype(bf16)
    rows = lambda b, i: (b, i, 0)
    return pl.pallas_call(
        _mlp_kernel,
        grid=(B, S // MLP_ROWS),
        in_specs=[
            pl.BlockSpec((1, MLP_ROWS, D), rows),
            pl.BlockSpec((1, MLP_ROWS, ATTN_W), rows),
            pl.BlockSpec((1, MLP_ROWS, CONV_CH), rows),
            _const((ATTN_W, D)),
            _const((CONV_CH, D)),
            _const((1, D)),
            _const((D, D_FF)),
            _const((D_FF, D)),
            _const((1, D)),
        ],
        out_specs=pl.BlockSpec((1, MLP_ROWS, D), rows),
        out_shape=jax.ShapeDtypeStruct((B, S, D), x.dtype),
        compiler_params=pltpu.CompilerParams(
            dimension_semantics=("arbitrary", "arbitrary"), vmem_limit_bytes=VMEM_LIMIT),
        name="out_mlp",
    )(x, attn, conv, wo[:ATTN_W], wo[ATTN_W:], mlp_norm_g, w_up[0].astype(bf16),
      w_down[0].astype(bf16), final_norm_g.reshape(1, D))


def kernel(x, meta_tokens, attn_norm_g, w_in, conv_w, conv_b, conv_norm_g, conv_norm_b,
           w_out, mlp_norm_g, w_up, w_down, final_norm_g):
    B, S, D = x.shape
    assert D == D_MODEL and S % MLP_ROWS == 0 and attn_norm_g.shape[0] == 1
    q, iq, misc, conv, k, v, ika, ikb = _project(
        x, meta_tokens, attn_norm_g, w_in, conv_w, conv_b, conv_norm_g, conv_norm_b)
    attn = _attend(q, iq, misc, ika, ikb, k, v)
    return _out_mlp(x, attn, conv, w_out, mlp_norm_g, w_up, w_down, final_norm_g)
```
